```python
import jax
import jax.numpy as jnp
from jax import lax
import numpy as np

D_MODEL = 1024
BATCH = 32
SEQ = 256
DEPTH = 2
DEC_BATCH = 4
DEC_SEQ = 1024
PAST_LEN = 256

GRID_W = 64
N_BRANCH = 3
D_A = D_MODEL // 2
N_A = 64
H_A = D_A // N_A
W_LORA = 64
A_LORA = 64
G_LORA = 128
D_B = D_MODEL // 2
DH_B = 64
H_B = D_B // DH_B
NA_KH = 8
NA_KW = 16
D_C = D_MODEL // 2
DH_C = 64
H_C = D_C // DH_C
RET_CHUNK = 64
D_FF = 2816
ROPE_BASE = 10000.0
QBLOCK = 128
RWKV_COLS = 3 * D_A + W_LORA + A_LORA + G_LORA
NA_COLS = 3 * D_B
RET_COLS = 4 * D_C
GATE_COLS = N_BRANCH * D_MODEL
IN_COLS = RWKV_COLS + NA_COLS + RET_COLS + GATE_COLS
NEG_BIG = -1e9
RWKV_LN_EPS = 64e-5
RET_GN_EPS = 1e-5
F32 = jnp.float32

kernel_name = 'hybrid_rwkv7_natten_retention_diffusion_step'


def rmsnorm(x, g, eps=1e-6):
    xf = x.astype(F32)
    y = xf * lax.rsqrt(jnp.mean(xf * xf, axis=-1, keepdims=True) + eps)
    return (y * g.astype(F32)).astype(x.dtype)


def head_groupnorm(y, eps):
    B, T, H, Dh = y.shape
    yf = y.astype(F32)
    mu = jnp.mean(yf, axis=-1, keepdims=True)
    var = jnp.mean(jnp.square(yf - mu), axis=-1, keepdims=True)
    return ((yf - mu) * lax.rsqrt(var + eps)).reshape(B, T, H * Dh)


def dwconv3(x, w):
    xp = jnp.pad(x, ((0, 0), (1, 1), (0, 0)))
    return xp[:, :-2] * w[0] + xp[:, 1:-1] * w[1] + xp[:, 2:] * w[2]


def ada_modulation(cvec, w, b):
    m = jax.nn.silu(cvec) @ w + b
    return m.reshape(m.shape[0], 6, D_MODEL)


def axial_rope(x):
    B, T, H, Dh = x.shape
    t = jnp.arange(T)
    half = Dh // 2
    nf = half // 2
    inv = ROPE_BASE ** (-jnp.arange(nf, dtype=F32) / nf)

    def rot(xp, pos):
        ang = pos.astype(F32)[:, None] * inv[None, :]
        cos = jnp.cos(ang)[None, :, None, :]
        sin = jnp.sin(ang)[None, :, None, :]
        x1 = xp[..., :nf].astype(F32)
        x2 = xp[..., nf:].astype(F32)
        return jnp.concatenate([x1 * cos - x2 * sin, x1 * sin + x2 * cos], axis=-1)

    out = jnp.concatenate([rot(x[..., :half], t // GRID_W), rot(x[..., half:], t % GRID_W)], axis=-1)
    return out.astype(x.dtype)


def wkv7_scan(r, w, k, v, a, b, s0, reverse):
    xs = tuple(jnp.moveaxis(t.astype(F32), 1, 0) for t in (r, w, k, v, a, b))

    def step(S, inp):
        r_t, w_t, k_t, v_t, a_t, b_t = inp
        sa = jnp.einsum('bhvk,bhk->bhv', S, a_t)
        S = S * w_t[:, :, None, :] + sa[..., None] * b_t[:, :, None, :] + v_t[..., None] * k_t[:, :, None, :]
        return S, jnp.einsum('bhvk,bhk->bhv', S, r_t)

    S, ys = lax.scan(step, s0.astype(F32), xs, reverse=reverse)
    return jnp.moveaxis(ys, 0, 1), S


def rwkv7_branch(z, s0, lp):
    B, T, _ = z.shape
    r, k, v, wl, al, gl = jnp.split(z, [D_A, 2 * D_A, 3 * D_A, 3 * D_A + W_LORA, 3 * D_A + W_LORA + A_LORA], axis=-1)
    heads = lambda t: t.reshape(B, T, H_A, N_A)
    kk = heads(k * lp['rwkv_k_k']).astype(F32)
    kk = kk / jnp.maximum(jnp.sqrt(jnp.sum(kk * kk, axis=-1, keepdims=True)), 1e-12)
    g = jax.nn.sigmoid(gl) @ lp['rwkv_g_up']
    wt = jnp.tanh(wl)
    ys = []
    states = []
    for d in range(2):
        w_log = -jax.nn.softplus(-(lp['rwkv_w0'][d] + wt @ lp['rwkv_w_up'][d]).astype(F32)) - 0.5
        decay = jnp.exp(-jnp.exp(w_log))
        a = jax.nn.sigmoid((lp['rwkv_a0'][d] + al @ lp['rwkv_a_up'][d]).astype(F32))
        k_d = k.astype(F32) * (1.0 + (a - 1.0) * lp['rwkv_k_a'].astype(F32))
        y_d, s_d = wkv7_scan(heads(r), heads(decay), heads(k_d), heads(v), -kk, kk * heads(a), s0[:, d], d == 1)
        ys.append(y_d)
        states.append(s_d)
    y = ys[0] + ys[1]
    bonus = jnp.sum(heads(r) * heads(k) * lp['rwkv_r_k'], axis=-1, keepdims=True) * heads(v)
    o = head_groupnorm(y, RWKV_LN_EPS) * lp['rwkv_ln_g'].astype(F32) + lp['rwkv_ln_b'].astype(F32)
    o = (o.astype(z.dtype) + bonus.reshape(B, T, D_A)) * g
    return o @ lp['w_o_rwkv'], jnp.stack(states, axis=1).astype(s0.dtype)


def dense_attention(q, k, v):
    B, L, H, D = q.shape
    qb = q.reshape(B, L // QBLOCK, QBLOCK, H, D).transpose(1, 0, 2, 3, 4)

    def block(qi):
        s = jnp.einsum('bqhd,bhkd->bhqk', qi, k).astype(F32) * (D ** -0.5)
        p = jax.nn.softmax(s, axis=-1).astype(v.dtype)
        return jnp.einsum('bhqk,bhkd->bqhd', p, v)

    o = lax.map(block, qb)
    return o.transpose(1, 0, 2, 3, 4).reshape(B, L, H, D)


def neighbourhood_attention(q, k, v, k_ctx, v_ctx, rpb):
    B, T, H, D = q.shape
    rows = T // GRID_W
    kh = min(NA_KH, rows)
    scale = D ** -0.5
    qg = q.reshape(B, rows, GRID_W, H, D).transpose(1, 0, 2, 3, 4)
    kg = k.reshape(B, rows, GRID_W, H, D)
    vg = v.reshape(B, rows, GRID_W, H, D)
    cols = jnp.arange(GRID_W)
    c_start = jnp.clip(cols - NA_KW // 2, 0, GRID_W - NA_KW)
    col_valid = (cols[None, :] >= c_start[:, None]) & (cols[None, :] < c_start[:, None] + NA_KW)
    dc = jnp.clip(cols[None, :] - cols[:, None], -(NA_KW - 1), NA_KW - 1) + NA_KW - 1
    r_idx = jnp.arange(rows)
    r_start = jnp.clip(r_idx - kh // 2, 0, rows - kh)

    def row_block(args):
        q_r, r, rs = args
        k_r = lax.dynamic_slice_in_dim(kg, rs, kh, axis=1)
        v_r = lax.dynamic_slice_in_dim(vg, rs, kh, axis=1)
        dr = rs + jnp.arange(kh) - r + NA_KH - 1
        bias = rpb[:, dr[None, :, None], dc[:, None, :]].astype(F32)
        s_loc = jnp.einsum('bqhd,bjwhd->bhqjw', q_r, k_r).astype(F32) * scale + bias[None]
        s_loc = jnp.where(col_valid[:, None, :], s_loc, NEG_BIG).reshape(B, H, GRID_W, kh * GRID_W)
        s_ctx = jnp.einsum('bqhd,bhld->bhql', q_r, k_ctx).astype(F32) * scale
        p = jax.nn.softmax(jnp.concatenate([s_loc, s_ctx], axis=-1), axis=-1).astype(v.dtype)
        o = jnp.einsum('bhqn,bnhd->bqhd', p[..., :kh * GRID_W], v_r.reshape(B, kh * GRID_W, H, D))
        return o + jnp.einsum('bhql,bhld->bqhd', p[..., kh * GRID_W:], v_ctx)

    o = lax.map(row_block, (qg, r_idx, r_start))
    return o.transpose(1, 0, 2, 3, 4).reshape(B, T, H, D)


def retention_chunkwise(q, k, v, log_gamma, s0):
    B, T, H, Dh = q.shape
    n = T // RET_CHUNK
    j = jnp.arange(RET_CHUNK, dtype=F32)
    diff = j[:, None] - j[None, :]
    dmat = jnp.where(diff >= 0, jnp.exp(log_gamma[:, None, None] * jnp.maximum(diff, 0.0)), 0.0)
    xi = jnp.exp(log_gamma[:, None] * (j + 1.0))[:, :, None]
    zeta = jnp.exp(log_gamma[:, None] * (RET_CHUNK - 1.0 - j))[:, :, None]
    g_chunk = jnp.exp(log_gamma * RET_CHUNK)[:, None, None]

    def chunks(t):
        return t.astype(F32).reshape(B, n, RET_CHUNK, H, Dh).transpose(1, 0, 3, 2, 4)

    def step(S, inp):
        qi, ki, vi = inp
        inner = jnp.einsum('bhid,bhjd->bhij', qi, ki) * dmat
        o = jnp.einsum('bhij,bhjv->bhiv', inner, vi) + jnp.einsum('bhid,bhdv->bhiv', qi, S) * xi
        S = S * g_chunk + jnp.einsum('bhjd,bhjv->bhdv', ki * zeta, vi)
        return S, o

    S, o = lax.scan(step, s0.astype(F32), (chunks(q), chunks(k), chunks(v)))
    return o.transpose(1, 0, 3, 2, 4).reshape(B, T, H, Dh), S


def retention_branch(z, s0, lp, use_rope):
    B, T, _ = z.shape
    q, k, v, g = jnp.split(z, 4, axis=-1)
    q, k, v = (t.reshape(B, T, H_C, DH_C) for t in (q, k, v))
    if use_rope:
        q, k = axial_rope(q), axial_rope(k)
    k = k * (DH_C ** -0.5)
    lg = jax.nn.log_sigmoid(lp['ret_decay'].astype(F32))
    y_f, s_f = retention_chunkwise(q, k, v, lg[0], s0[:, 0])
    y_b, s_b = retention_chunkwise(jnp.flip(q, 1), jnp.flip(k, 1), jnp.flip(v, 1), lg[1], s0[:, 1])
    y = y_f + jnp.flip(y_b, 1)
    o = (head_groupnorm(y, RET_GN_EPS) * lp['ret_gn'].astype(F32)).astype(z.dtype) * jax.nn.silu(g)
    return o @ lp['w_o_ret'], jnp.stack([s_f, s_b], axis=1).astype(s0.dtype)


def conv_ffn(h, w_up, w_conv, w_down):
    u = dwconv3(h @ w_up, w_conv)
    val, gate = jnp.split(u, 2, axis=-1)
    return (jax.nn.gelu(gate) * val) @ w_down


def trunk_layer(x, m, lp, s_rwkv, s_ret, k_ctx, v_ctx, is_ctx):
    B, T, _ = x.shape
    sh1, sc1, g1, sh2, sc2, g2 = (m[:, i][:, None, :] for i in range(6))
    h = rmsnorm(x, lp['norm_mix_pre']) * (1.0 + sc1) + sh1
    z = h @ lp['w_in']
    o0 = RWKV_COLS
    o1 = o0 + NA_COLS
    o2 = o1 + RET_COLS
    z_rwkv = dwconv3(z[..., :o0], lp['rwkv_conv'])
    z_na = z[..., o0:o1]
    z_ret = z[..., o1:o2]
    gates = jax.nn.sigmoid(z[..., o2:]).reshape(B, T, N_BRANCH, D_MODEL)
    out_a, st_rwkv = rwkv7_branch(z_rwkv, s_rwkv, lp)
    q, k, v = (t.reshape(B, T, H_B, DH_B) for t in jnp.split(z_na, 3, axis=-1))
    if is_ctx:
        k_ctx = k.transpose(0, 2, 1, 3)
        v_ctx = v.transpose(0, 2, 1, 3)
        o_na = dense_attention(q, k_ctx, v_ctx)
    else:
        o_na = neighbourhood_attention(q, k, v, k_ctx, v_ctx, lp['na_rpb'])
    out_b = o_na.reshape(B, T, D_B) @ lp['w_o_na']
    out_c, st_ret = retention_branch(z_ret, s_ret, lp, not is_ctx)
    merged = gates[:, :, 0] * out_a + gates[:, :, 1] * out_b + gates[:, :, 2] * out_c
    x = x + g1 * rmsnorm(merged @ lp['w_out'], lp['norm_mix_post'])
    h2 = rmsnorm(x, lp['norm_ffn_pre']) * (1.0 + sc2) + sh2
    x = x + g2 * rmsnorm(conv_ffn(h2, lp['ffn_up'], lp['ffn_conv'], lp['ffn_down']), lp['norm_ffn_post'])
    return x, st_rwkv, st_ret, k_ctx, v_ctx


def setup_inputs(seed: int = 0) -> dict:
    key = jax.random.key(seed)
    ks = jax.random.split(key, 48)
    cnt = [0]

    def nxt():
        cnt[0] += 1
        return ks[cnt[0] - 1]

    def nrm(shape, scale):
        return jax.random.normal(nxt(), shape, F32) * scale

    def gain(shape):
        return 1.0 + nrm(shape, 0.05)

    def conv3(ch):
        return nrm((DEPTH, 3, ch), 0.2).at[:, 1].add(1.0)

    ret_base = jnp.log(2.0 ** (5.0 + jnp.arange(H_C, dtype=F32)) - 1.0)
    return {
        'x_prompt': nrm((BATCH, SEQ, D_MODEL), 1.0),
        'x_sample': nrm((DEC_BATCH, DEC_SEQ, D_MODEL), 1.0),
        'cache_na_k': nrm((DEC_BATCH, DEPTH, H_B, PAST_LEN, DH_B), 1.0),
        'cache_na_v': nrm((DEC_BATCH, DEPTH, H_B, PAST_LEN, DH_B), 1.0),
        'state_rwkv': nrm((DEC_BATCH, DEPTH, 2, H_A, N_A, N_A), 0.3),
        'state_ret': nrm((DEC_BATCH, DEPTH, 2, H_C, DH_C, DH_C), 0.3),
        'c': nrm((DEC_BATCH, D_MODEL), 1.0),
        'c_ctx': nrm((D_MODEL,), 1.0),
        'ada_w': nrm((DEPTH, D_MODEL, 6 * D_MODEL), D_MODEL ** -0.5),
        'ada_b': nrm((DEPTH, 6 * D_MODEL), 0.01),
        'norm_mix_pre': gain((DEPTH, D_MODEL)),
        'norm_mix_post': gain((DEPTH, D_MODEL)),
        'norm_ffn_pre': gain((DEPTH, D_MODEL)),
        'norm_ffn_post': gain((DEPTH, D_MODEL)),
        'w_in': nrm((DEPTH, D_MODEL, IN_COLS), D_MODEL ** -0.5),
        'rwkv_conv': conv3(RWKV_COLS),
        'rwkv_w0': jax.random.uniform(nxt(), (DEPTH, 2, D_A), F32, -6.0, 1.0),
        'rwkv_w_up': nrm((DEPTH, 2, W_LORA, D_A), 0.5 * W_LORA ** -0.5),
        'rwkv_a0': nrm((DEPTH, 2, D_A), 0.3),
        'rwkv_a_up': nrm((DEPTH, 2, A_LORA, D_A), 0.5 * A_LORA ** -0.5),
        'rwkv_g_up': nrm((DEPTH, G_LORA, D_A), G_LORA ** -0.5),
        'rwkv_k_k': 0.85 + nrm((DEPTH, D_A), 0.05),
        'rwkv_k_a': gain((DEPTH, D_A)),
        'rwkv_r_k': nrm((DEPTH, H_A, N_A), 0.1),
        'rwkv_ln_g': gain((DEPTH, D_A)),
        'rwkv_ln_b': nrm((DEPTH, D_A), 0.01),
        'na_rpb': nrm((DEPTH, H_B, 2 * NA_KH - 1, 2 * NA_KW - 1), 0.1),
        'ret_decay': ret_base + nrm((DEPTH, 2, H_C), 0.05),
        'ret_gn': gain((DEPTH, D_C)),
        'w_o_rwkv': nrm((DEPTH, D_A, D_MODEL), D_A ** -0.5),
        'w_o_na': nrm((DEPTH, D_B, D_MODEL), D_B ** -0.5),
        'w_o_ret': nrm((DEPTH, D_C, D_MODEL), D_C ** -0.5),
        'w_out': nrm((DEPTH, D_MODEL, D_MODEL), D_MODEL ** -0.5),
        'ffn_up': nrm((DEPTH, D_MODEL, 2 * D_FF), D_MODEL ** -0.5),
        'ffn_conv': conv3(2 * D_FF),
        'ffn_down': nrm((DEPTH, D_FF, D_MODEL), D_FF ** -0.5),
    }


def reference(x_prompt, x_sample, cache_na_k, cache_na_v, state_rwkv, state_ret, c,
              c_ctx, ada_w, ada_b, norm_mix_pre, norm_mix_post, norm_ffn_pre, norm_ffn_post,
              w_in, rwkv_conv, rwkv_w0, rwkv_w_up, rwkv_a0, rwkv_a_up, rwkv_g_up,
              rwkv_k_k, rwkv_k_a, rwkv_r_k, rwkv_ln_g, rwkv_ln_b, na_rpb, ret_decay, ret_gn,
              w_o_rwkv, w_o_na, w_o_ret, w_out, ffn_up, ffn_conv, ffn_down):
    bp = x_prompt.shape[0]
    xp = x_prompt
    xs = x_sample
    new_k, new_v, new_rw, new_rt = [], [], [], []
    for l in range(DEPTH):
        lp = {
            'norm_mix_pre': norm_mix_pre[l], 'norm_mix_post': norm_mix_post[l],
            'norm_ffn_pre': norm_ffn_pre[l], 'norm_ffn_post': norm_ffn_post[l],
            'w_in': w_in[l], 'rwkv_conv': rwkv_conv[l], 'rwkv_w0': rwkv_w0[l], 'rwkv_w_up': rwkv_w_up[l],
            'rwkv_a0': rwkv_a0[l], 'rwkv_a_up': rwkv_a_up[l], 'rwkv_g_up': rwkv_g_up[l],
            'rwkv_k_k': rwkv_k_k[l], 'rwkv_k_a': rwkv_k_a[l], 'rwkv_r_k': rwkv_r_k[l],
            'rwkv_ln_g': rwkv_ln_g[l], 'rwkv_ln_b': rwkv_ln_b[l], 'na_rpb': na_rpb[l],
            'ret_decay': ret_decay[l], 'ret_gn': ret_gn[l], 'w_o_rwkv': w_o_rwkv[l], 'w_o_na': w_o_na[l],
            'w_o_ret': w_o_ret[l], 'w_out': w_out[l], 'ffn_up': ffn_up[l], 'ffn_conv': ffn_conv[l],
            'ffn_down': ffn_down[l],
        }
        m_ctx = ada_modulation(c_ctx[None, :], ada_w[l], ada_b[l])
        zero_rw = jnp.zeros((bp, 2, H_A, N_A, N_A), xp.dtype)
        zero_rt = jnp.zeros((bp, 2, H_C, DH_C, DH_C), xp.dtype)
        xp, st_rw, st_rt, k_c, v_c = trunk_layer(xp, m_ctx, lp, zero_rw, zero_rt, None, None, True)
        new_k.append(k_c)
        new_v.append(v_c)
        new_rw.append(st_rw)
        new_rt.append(st_rt)
        m_lat = ada_modulation(c, ada_w[l], ada_b[l])
        xs, _, _, _, _ = trunk_layer(xs, m_lat, lp, state_rwkv[:, l], state_ret[:, l],
                                     cache_na_k[:, l], cache_na_v[:, l], False)
    return (xp, xs, jnp.stack(new_k, axis=1), jnp.stack(new_v, axis=1), jnp.stack(new_rw, axis=1), jnp.stack(new_rt, axis=1))
```

```python
import functools
import math

import jax
import jax.numpy as jnp
from jax import lax
from jax.experimental import pallas as pl
from jax.experimental.pallas import tpu as pltpu

F32 = jnp.float32
BF16 = jnp.bfloat16

D_MODEL = 1024
DEPTH = 2
N_HEAD = 8
HEAD_DIM = 64
D_BRANCH = N_HEAD * HEAD_DIM
GRID_W = 64
NA_KH = 8
NA_KW = 16
W_LORA = 64
A_LORA = 64
G_LORA = 128
D_FF = 2816
ROPE_BASE = 10000.0
NEG_BIG = -1e9
RWKV_LN_EPS = 64e-5
RET_GN_EPS = 1e-5
RMS_EPS = 1e-6

RWKV_COLS = 3 * D_BRANCH + W_LORA + A_LORA + G_LORA
RWKV_PAD = 2048
OFF_GATE = 0
OFF_RWKV = 3 * D_MODEL
OFF_NA = OFF_RWKV + RWKV_PAD
OFF_RET = OFF_NA + 3 * D_BRANCH
Z_COLS = OFF_RET + 4 * D_BRANCH

LANES = 128
TOKEN_TILE = 1024
MOD_ROWS = 8
VMEM_LIMIT = 56 * 1024 * 1024


def _cparams(sem):
    return pltpu.CompilerParams(dimension_semantics=sem, vmem_limit_bytes=VMEM_LIMIT)


def _rms(x, g):
    return x * lax.rsqrt(jnp.mean(x * x, axis=-1, keepdims=True) + RMS_EPS) * g


def _softplus(x):
    return jnp.maximum(x, 0.0) + jnp.log1p(jnp.exp(-jnp.abs(x)))


def _dot(a, b):
    return jnp.dot(a, b, preferred_element_type=F32)


def _dot_nt(a, b):
    return lax.dot_general(a, b, (((1,), (1,)), ((), ())), preferred_element_type=F32)


def _dot_tn(a, b):
    return lax.dot_general(a, b, (((0,), (0,)), ((), ())), preferred_element_type=F32)


def _mod_index(i, n_ctx_tiles, tiles_per_lat):
    return jnp.where(i < n_ctx_tiles, 0, 1 + jnp.maximum(i - n_ctx_tiles, 0) // tiles_per_lat)


def _seq_edges(i, n_ctx_tiles, t_ctx, t_lat, tm):
    seqlen = jnp.where(i < n_ctx_tiles, t_ctx, t_lat)
    pos = lax.broadcasted_iota(jnp.int32, (tm, 1), 0) & (seqlen - 1)
    return pos == 0, pos == seqlen - 1


def _dwconv3(u, w, first, last):
    tm = u.shape[0]
    prev = jnp.where(first, 0.0, pltpu.roll(u, 1, 0))
    nxt = jnp.where(last, 0.0, pltpu.roll(u, tm - 1, 0))
    return prev * w[0:1] + u * w[1:2] + nxt * w[2:3]


def _ada_kernel(c_ref, w_ref, b_ref, o_ref):
    c = c_ref[...]
    s = (c * jax.nn.sigmoid(c)).astype(BF16)
    o_ref[0] = _dot(s, w_ref[0].astype(BF16)) + b_ref[0]


def _ada_modulation(cvec, ada_w, ada_b):
    tn = 512
    n_out = 6 * D_MODEL
    return pl.pallas_call(
        _ada_kernel,
        grid=(DEPTH, n_out // tn),
        in_specs=[
            pl.BlockSpec((MOD_ROWS, D_MODEL), lambda l, j: (0, 0)),
            pl.BlockSpec((1, D_MODEL, tn), lambda l, j: (l, 0, j)),
            pl.BlockSpec((1, 1, tn), lambda l, j: (l, 0, j)),
        ],
        out_specs=pl.BlockSpec((1, MOD_ROWS, tn), lambda l, j: (l, 0, j)),
        out_shape=jax.ShapeDtypeStruct((DEPTH, MOD_ROWS, n_out), F32),
        compiler_params=_cparams(("parallel", "parallel")),
    )(cvec, ada_w, ada_b.reshape(DEPTH, 1, n_out))


def _in_proj_kernel(x_ref, m_ref, g_ref, w_ref, o_ref, h_scr):
    @pl.when(pl.program_id(1) == 0)
    def _():
        y = _rms(x_ref[...], g_ref[...])
        h = y * (1.0 + m_ref[0, :, D_MODEL:2 * D_MODEL]) + m_ref[0, :, 0:D_MODEL]
        h_scr[...] = h.astype(BF16)

    o_ref[...] = _dot(h_scr[...], w_ref[...])


def _in_proj(x, mod, gain, w_bf16, n_ctx_tiles):
    n = x.shape[0]
    tm, tn = TOKEN_TILE, 512
    midx = functools.partial(_mod_index, n_ctx_tiles=n_ctx_tiles, tiles_per_lat=1)
    return pl.pallas_call(
        _in_proj_kernel,
        grid=(n // tm, Z_COLS // tn),
        in_specs=[
            pl.BlockSpec((tm, D_MODEL), lambda i, j: (i, 0)),
            pl.BlockSpec((1, 1, 6 * D_MODEL), lambda i, j: (midx(i), 0, 0)),
            pl.BlockSpec((1, D_MODEL), lambda i, j: (0, 0)),
            pl.BlockSpec((D_MODEL, tn), lambda i, j: (0, j)),
        ],
        out_specs=pl.BlockSpec((tm, tn), lambda i, j: (i, j)),
        out_shape=jax.ShapeDtypeStruct((n, Z_COLS), F32),
        scratch_shapes=[pltpu.VMEM((tm, D_MODEL), BF16)],
        compiler_params=_cparams(("parallel", "arbitrary")),
    )(x, mod, gain, w_bf16)


def _rwkv_conv_kernel(z_ref, w_ref, o_ref, *, n_ctx_tiles, t_ctx, t_lat):
    first, last = _seq_edges(pl.program_id(0), n_ctx_tiles, t_ctx, t_lat, z_ref.shape[0])
    o_ref[...] = _dwconv3(z_ref[...], w_ref[...], first, last)


def _rwkv_conv(z, conv_w, n_ctx_tiles, t_ctx, t_lat):
    n = z.shape[0]
    tm, tn = TOKEN_TILE, D_BRANCH
    cb = OFF_RWKV // tn
    return pl.pallas_call(
        functools.partial(_rwkv_conv_kernel, n_ctx_tiles=n_ctx_tiles, t_ctx=t_ctx, t_lat=t_lat),
        grid=(n // tm, 3),
        in_specs=[
            pl.BlockSpec((tm, tn), lambda i, j: (i, cb + j)),
            pl.BlockSpec((3, tn), lambda i, j: (0, j)),
        ],
        out_specs=pl.BlockSpec((tm, tn), lambda i, j: (i, j)),
        out_shape=jax.ShapeDtypeStruct((n, 3 * D_BRANCH), F32),
        compiler_params=_cparams(("parallel", "parallel")),
    )(z, conv_w)


def _rwkv_lora_kernel(z_ref, cw_ref, wup_ref, aup_ref, gup_ref, w0_ref, a0_ref,
                      dec0_ref, dec1_ref, al0_ref, al1_ref, g_ref, *, n_ctx_tiles, t_ctx, t_lat):
    first, last = _seq_edges(pl.program_id(0), n_ctx_tiles, t_ctx, t_lat, z_ref.shape[0])
    u = _dwconv3(z_ref[...], cw_ref[...], first, last)
    wa = u[:, 0:LANES]
    tw = jnp.tanh(wa).astype(BF16)
    ab = wa.astype(BF16)
    sg = jax.nn.sigmoid(u[:, LANES:2 * LANES]).astype(BF16)
    for d, (dec_ref, al_ref) in enumerate(((dec0_ref, al0_ref), (dec1_ref, al1_ref))):
        x = w0_ref[d:d + 1, :] + _dot(tw, wup_ref[d])
        w_log = -_softplus(-x) - 0.5
        dec_ref[...] = jnp.exp(-jnp.exp(w_log))
        al_ref[...] = jax.nn.sigmoid(a0_ref[d:d + 1, :] + _dot(ab, aup_ref[d]))
    g_ref[...] = _dot(sg, gup_ref[...])


def _rwkv_lora(z, conv_w_lora, wup_pad, aup_pad, gup, w0, a0, n_ctx_tiles, t_ctx, t_lat):
    n = z.shape[0]
    tm = TOKEN_TILE
    wl = 2 * LANES
    cb = (OFF_RWKV + 3 * D_BRANCH) // wl
    out = jax.ShapeDtypeStruct((n, D_BRANCH), F32)
    ospec = pl.BlockSpec((tm, D_BRANCH), lambda i: (i, 0))
    full2 = lambda shape: pl.BlockSpec(shape, lambda i: (0, 0))
    full3 = lambda shape: pl.BlockSpec(shape, lambda i: (0, 0, 0))
    return pl.pallas_call(
        functools.partial(_rwkv_lora_kernel, n_ctx_tiles=n_ctx_tiles, t_ctx=t_ctx, t_lat=t_lat),
        grid=(n // tm,),
        in_specs=[
            pl.BlockSpec((tm, wl), lambda i: (i, cb)),
            full2((3, wl)),
            full3((2, LANES, D_BRANCH)),
            full3((2, LANES, D_BRANCH)),
            full2((G_LORA, D_BRANCH)),
            full2((2, D_BRANCH)),
            full2((2, D_BRANCH)),
        ],
        out_specs=[ospec] * 5,
        out_shape=[out] * 5,
        compiler_params=_cparams(("parallel",)),
    )(z, conv_w_lora, wup_pad, aup_pad, gup, w0, a0)


def _wkv_kernel(xs_ref, p_ref, s0_ref, y_ref, st_ref, s_scr, *, tb, unroll):
    t = pl.program_id(1)

    @pl.when(t == 0)
    def _():
        s_scr[...] = s0_ref[...]

    k_k = p_ref[0]
    k_a = p_ref[1]

    def step(j, carry):
        r = xs_ref[j, 0]
        k = xs_ref[j, 1]
        w = xs_ref[j, 3]
        a = xs_ref[j, 4]
        kk = k * k_k
        kk = kk / jnp.maximum(jnp.sqrt(jnp.sum(kk * kk, axis=0, keepdims=True)), 1e-12)
        av = -kk
        bv = kk * a
        kd = k * (1.0 + (a - 1.0) * k_a)

        def vstep(vi, c):
            sv = s_scr[vi]
            sa = jnp.sum(sv * av, axis=0, keepdims=True)
            vrow = xs_ref[j, 2, pl.ds(vi, 1), :]
            sn = sv * w + sa * bv + vrow * kd
            s_scr[vi] = sn
            y_ref[j, pl.ds(vi, 1), :] = jnp.sum(sn * r, axis=0, keepdims=True)
            return c

        lax.fori_loop(0, HEAD_DIM, vstep, 0, unroll=unroll)
        return carry

    lax.fori_loop(0, tb, step, 0)

    @pl.when(t == pl.num_programs(1) - 1)
    def _():
        st_ref[...] = s_scr[...]


def _wkv_scan(xs, params, s0):
    t_len, _, _, n_lane = xs.shape
    tb = 8
    return pl.pallas_call(
        functools.partial(_wkv_kernel, tb=tb, unroll=4),
        grid=(n_lane // LANES, t_len // tb),
        in_specs=[
            pl.BlockSpec((tb, 5, HEAD_DIM, LANES), lambda c, t: (t, 0, 0, c)),
            pl.BlockSpec((2, HEAD_DIM, LANES), lambda c, t: (0, 0, c)),
            pl.BlockSpec((HEAD_DIM, HEAD_DIM, LANES), lambda c, t: (0, 0, c)),
        ],
        out_specs=[
            pl.BlockSpec((tb, HEAD_DIM, LANES), lambda c, t: (t, 0, c)),
            pl.BlockSpec((HEAD_DIM, HEAD_DIM, LANES), lambda c, t: (0, 0, c)),
        ],
        out_shape=[
            jax.ShapeDtypeStruct((t_len, HEAD_DIM, n_lane), F32),
            jax.ShapeDtypeStruct((HEAD_DIM, HEAD_DIM, n_lane), F32),
        ],
        scratch_shapes=[pltpu.VMEM((HEAD_DIM, HEAD_DIM, LANES), F32)],
        compiler_params=_cparams(("parallel", "arbitrary")),
    )(xs, params, s0)


def _rwkv_post_kernel(yf_ref, yb_ref, xs_ref, p_ref, o_ref, *, tb, bwd_lane_shift):
    ln_g = p_ref[0]
    ln_b = p_ref[1]
    r_k = p_ref[2]
    for j in range(tb):
        yb = yb_ref[tb - 1 - j]
        if bwd_lane_shift:
            yb = pltpu.roll(yb, LANES - bwd_lane_shift, 1)
        y = yf_ref[j] + yb
        mu = jnp.mean(y, axis=0, keepdims=True)
        dlt = y - mu
        var = jnp.mean(dlt * dlt, axis=0, keepdims=True)
        o = dlt * lax.rsqrt(var + RWKV_LN_EPS) * ln_g + ln_b
        bonus = jnp.sum(xs_ref[j, 0] * xs_ref[j, 1] * r_k, axis=0, keepdims=True) * xs_ref[j, 2]
        o_ref[j] = o + bonus


def _rwkv_post(y, xs, params, n_fwd_lanes):
    t_len, _, n_lane = y.shape
    tb = 8
    nt = t_len // tb
    if n_fwd_lanes % LANES == 0:
        lb, bwd_blk, shift = n_fwd_lanes, 1, 0
    else:
        lb, bwd_blk, shift = LANES, 0, n_fwd_lanes
    return pl.pallas_call(
        functools.partial(_rwkv_post_kernel, tb=tb, bwd_lane_shift=shift),
        grid=(nt,),
        in_specs=[
            pl.BlockSpec((tb, HEAD_DIM, lb), lambda i: (i, 0, 0)),
            pl.BlockSpec((tb, HEAD_DIM, lb), lambda i: (nt - 1 - i, 0, bwd_blk)),
            pl.BlockSpec((tb, 5, HEAD_DIM, lb), lambda i: (i, 0, 0, 0)),
            pl.BlockSpec((3, HEAD_DIM, lb), lambda i: (0, 0, 0)),
        ],
        out_specs=pl.BlockSpec((tb, HEAD_DIM, lb), lambda i: (i, 0, 0)),
        out_shape=jax.ShapeDtypeStruct((t_len, HEAD_DIM, lb), F32),
        compiler_params=_cparams(("parallel",)),
    )(y, y, xs, params)


def _ctx_attn_kernel(x_ref, o_ref):
    scale = HEAD_DIM ** -0.5
    for h in range(N_HEAD):
        q = x_ref[0, 0, h].astype(BF16)
        k = x_ref[1, 0, h].astype(BF16)
        v = x_ref[2, 0, h].astype(BF16)
        s = _dot_nt(q, k) * scale
        p = jnp.exp(s - jnp.max(s, axis=-1, keepdims=True))
        p = p / jnp.sum(p, axis=-1, keepdims=True)
        o_ref[0, h] = _dot(p.astype(BF16), v)


def _ctx_attention(qkv):
    _, b, h, t, dh = qkv.shape
    return pl.pallas_call(
        _ctx_attn_kernel,
        grid=(b,),
        in_specs=[pl.BlockSpec((3, 1, h, t, dh), lambda i: (0, i, 0, 0, 0))],
        out_specs=pl.BlockSpec((1, h, t, dh), lambda i: (i, 0, 0, 0)),
        out_shape=jax.ShapeDtypeStruct((b, h, t, dh), F32),
        compiler_params=_cparams(("parallel",)),
    )(qkv)


def _na_kernel(x_ref, kc_ref, vc_ref, bias_ref, o_ref, *, rows):
    scale = HEAD_DIM ** -0.5
    kh = min(NA_KH, rows)
    win = kh * GRID_W
    kc = kc_ref[0, 0, 0].astype(BF16)
    vc = vc_ref[0, 0, 0].astype(BF16)
    qcol = lax.broadcasted_iota(jnp.int32, (GRID_W, win), 0)
    kcol = lax.broadcasted_iota(jnp.int32, (GRID_W, win), 1) & (GRID_W - 1)
    c_start = jnp.clip(qcol - NA_KW // 2, 0, GRID_W - NA_KW)
    col_valid = (kcol >= c_start) & (kcol < c_start + NA_KW)

    def row_block(r, carry):
        rs = jnp.clip(r - kh // 2, 0, rows - kh)
        q = x_ref[0, 0, 0, pl.ds(pl.multiple_of(r * GRID_W, GRID_W), GRID_W), :].astype(BF16)
        kw = x_ref[1, 0, 0, pl.ds(pl.multiple_of(rs * GRID_W, GRID_W), win), :].astype(BF16)
        vw = x_ref[2, 0, 0, pl.ds(pl.multiple_of(rs * GRID_W, GRID_W), win), :].astype(BF16)
        s_loc = _dot_nt(q, kw) * scale + bias_ref[0, rs - r + NA_KH - 1]
        s_loc = jnp.where(col_valid, s_loc, NEG_BIG)
        s_ctx = _dot_nt(q, kc) * scale
        m = jnp.maximum(jnp.max(s_loc, axis=-1, keepdims=True), jnp.max(s_ctx, axis=-1, keepdims=True))
        p_loc = jnp.exp(s_loc - m)
        p_ctx = jnp.exp(s_ctx - m)
        den = jnp.sum(p_loc, axis=-1, keepdims=True) + jnp.sum(p_ctx, axis=-1, keepdims=True)
        o = _dot((p_loc / den).astype(BF16), vw) + _dot((p_ctx / den).astype(BF16), vc)
        o_ref[0, 0, pl.ds(pl.multiple_of(r * GRID_W, GRID_W), GRID_W), :] = o
        return carry

    lax.fori_loop(0, rows, row_block, 0)


def _na_attention(qkv, cache_k, cache_v, bias_tab, layer):
    _, b, h, t, dh = qkv.shape
    past = cache_k.shape[3]
    rows = t // GRID_W
    return pl.pallas_call(
        functools.partial(_na_kernel, rows=rows),
        grid=(b, h),
        in_specs=[
            pl.BlockSpec((3, 1, 1, t, dh), lambda i, j: (0, i, j, 0, 0)),
            pl.BlockSpec((1, 1, 1, past, dh), lambda i, j: (i, layer, j, 0, 0)),
            pl.BlockSpec((1, 1, 1, past, dh), lambda i, j: (i, layer, j, 0, 0)),
            pl.BlockSpec((1,) + bias_tab.shape[1:], lambda i, j: (j, 0, 0, 0)),
        ],
        out_specs=pl.BlockSpec((1, 1, t, dh), lambda i, j: (i, j, 0, 0)),
        out_shape=jax.ShapeDtypeStruct((b, h, t, dh), F32),
        compiler_params=_cparams(("parallel", "parallel")),
    )(qkv, cache_k, cache_v, bias_tab)


def _na_bias_table(rpb, rows):
    kh = min(NA_KH, rows)
    cols = jnp.arange(GRID_W)
    dc = jnp.clip(cols[None, :] - cols[:, None], -(NA_KW - 1), NA_KW - 1) + NA_KW - 1
    dr = jnp.arange(NA_KH)[:, None] + jnp.arange(kh)[None, :]
    dr = jnp.clip(dr, 0, 2 * NA_KH - 2)
    tab = rpb[:, dr[:, :, None, None], dc[None, None, :, :]]
    return tab.transpose(0, 1, 3, 2, 4).reshape(rpb.shape[0], NA_KH, GRID_W, kh * GRID_W)


def _rope_kernel(x_ref, c_ref, s_ref, o_ref):
    lo = (lax.broadcasted_iota(jnp.int32, (1, LANES), 1) & 31) < 16
    for cb in range(x_ref.shape[1] // LANES):
        sl = slice(cb * LANES, (cb + 1) * LANES)
        x = x_ref[:, sl]
        partner = jnp.where(lo, pltpu.roll(x, LANES - 16, 1), pltpu.roll(x, 16, 1))
        o_ref[:, sl] = x * c_ref[:, sl] + partner * s_ref[:, sl]


def _rope_tables(t_len):
    half = HEAD_DIM // 2
    nf = half // 2
    inv = ROPE_BASE ** (-jnp.arange(nf, dtype=F32) / nf)
    t = jnp.arange(t_len)

    def tab(pos):
        ang = pos.astype(F32)[:, None] * inv[None, :]
        cos, sin = jnp.cos(ang), jnp.sin(ang)
        return jnp.concatenate([cos, cos], -1), jnp.concatenate([-sin, sin], -1)

    c_row, s_row = tab(t // GRID_W)
    c_col, s_col = tab(t % GRID_W)
    cos = jnp.tile(jnp.concatenate([c_row, c_col], -1), (1, N_HEAD))
    sin = jnp.tile(jnp.concatenate([s_row, s_col], -1), (1, N_HEAD))
    return cos, sin


def _rope_qk(z, cos, sin, row_tile0, n_seq):
    t_len = cos.shape[0]
    cb = OFF_RET // D_BRANCH
    return pl.pallas_call(
        _rope_kernel,
        grid=(n_seq, 2),
        in_specs=[
            pl.BlockSpec((t_len, D_BRANCH), lambda i, j: (row_tile0 + i, cb + j)),
            pl.BlockSpec((t_len, D_BRANCH), lambda i, j: (0, 0)),
            pl.BlockSpec((t_len, D_BRANCH), lambda i, j: (0, 0)),
        ],
        out_specs=pl.BlockSpec((t_len, D_BRANCH), lambda i, j: (i, j)),
        out_shape=jax.ShapeDtypeStruct((n_seq * t_len, 2 * D_BRANCH), F32),
        compiler_params=_cparams(("parallel", "parallel")),
    )(z, cos, sin)


def _log_sigmoid(x):
    return -_softplus(-x)


def _ret_kernel(x_ref, dec_ref, gn_ref, s0_ref, o_ref, st_ref, *, t_len, hb, qb):
    for hh in range(hb):
        k = x_ref[1, 0, hh] * (HEAD_DIM ** -0.5)
        kb = k.astype(BF16)
        vb = x_ref[2, 0, hh].astype(BF16)
        lgf = _log_sigmoid(dec_ref[0, hh])[0:1, :]
        lgb = _log_sigmoid(dec_ref[1, hh])[0:1, :]
        lgf_h = lgf[:, 0:HEAD_DIM]
        lgb_h = lgb[:, 0:HEAD_DIM]
        s0f = s0_ref[0, 0, hh]
        s0b = s0_ref[0, 1, hh]
        s0f_b = s0f.astype(BF16)
        s0b_b = s0b.astype(BF16)
        for qi in range(t_len // qb):
            rows = slice(qi * qb, (qi + 1) * qb)
            q = x_ref[0, 0, hh, rows, :].astype(BF16)
            g = x_ref[3, 0, hh, rows, :]
            s = _dot_nt(q, kb)
            diff = (lax.broadcasted_iota(jnp.int32, (qb, t_len), 0) + qi * qb
                    - lax.broadcasted_iota(jnp.int32, (qb, t_len), 1)).astype(F32)
            dmat = (jnp.where(diff >= 0, jnp.exp(lgf * jnp.maximum(diff, 0.0)), 0.0)
                    + jnp.where(diff <= 0, jnp.exp(lgb * jnp.maximum(-diff, 0.0)), 0.0))
            y = _dot((s * dmat).astype(BF16), vb)
            pos = (lax.broadcasted_iota(jnp.int32, (qb, HEAD_DIM), 0) + qi * qb).astype(F32)
            y = y + _dot(q, s0f_b) * jnp.exp(lgf_h * (pos + 1.0))
            y = y + _dot(q, s0b_b) * jnp.exp(lgb_h * (t_len - pos))
            mu = jnp.mean(y, axis=-1, keepdims=True)
            dlt = y - mu
            var = jnp.mean(dlt * dlt, axis=-1, keepdims=True)
            yn = dlt * lax.rsqrt(var + RET_GN_EPS)
            o_ref[0, hh, rows, :] = (yn * gn_ref[hh]) * (g * jax.nn.sigmoid(g))
        pos = lax.broadcasted_iota(jnp.int32, (t_len, HEAD_DIM), 0).astype(F32)
        kzf = (k * jnp.exp(lgf_h * (t_len - 1.0 - pos))).astype(BF16)
        kzb = (k * jnp.exp(lgb_h * pos)).astype(BF16)
        st_ref[0, 0, hh] = s0f * jnp.exp(lgf_h * float(t_len)) + _dot_tn(kzf, vb)
        st_ref[0, 1, hh] = s0b * jnp.exp(lgb_h * float(t_len)) + _dot_tn(kzb, vb)


def _retention(x, dec, gn, s0, hb):
    _, b, h, t, dh = x.shape
    qb = min(t, 256)
    return pl.pallas_call(
        functools.partial(_ret_kernel, t_len=t, hb=hb, qb=qb),
        grid=(b, h // hb),
        in_specs=[
            pl.BlockSpec((4, 1, hb, t, dh), lambda i, j: (0, i, j, 0, 0)),
            pl.BlockSpec((2, hb, 8, t), lambda i, j: (0, j, 0, 0)),
            pl.BlockSpec((hb, 1, dh), lambda i, j: (j, 0, 0)),
            pl.BlockSpec((1, 2, hb, dh, dh), lambda i, j: (i, 0, j, 0, 0)),
        ],
        out_specs=[
            pl.BlockSpec((1, hb, t, dh), lambda i, j: (i, j, 0, 0)),
            pl.BlockSpec((1, 2, hb, dh, dh), lambda i, j: (i, 0, j, 0, 0)),
        ],
        out_shape=[
            jax.ShapeDtypeStruct((b, h, t, dh), F32),
            jax.ShapeDtypeStruct((b, 2, h, dh, dh), F32),
        ],
        compiler_params=_cparams(("parallel", "parallel")),
    )(x, dec, gn, s0)


def _mix_out_kernel(x_ref, m_ref, oa_ref, ga_ref, ob_ref, oc_ref, g0_ref, g1_ref, g2_ref,
                    wa_ref, wb_ref, wc_ref, wo_ref, gain_ref, o_ref):
    out_a = _dot((oa_ref[...] * ga_ref[...]).astype(BF16), wa_ref[...])
    out_b = _dot(ob_ref[...].astype(BF16), wb_ref[...])
    out_c = _dot(oc_ref[...].astype(BF16), wc_ref[...])
    merged = (jax.nn.sigmoid(g0_ref[...]) * out_a + jax.nn.sigmoid(g1_ref[...]) * out_b
              + jax.nn.sigmoid(g2_ref[...]) * out_c)
    y = _dot(merged.astype(BF16), wo_ref[...])
    gate = m_ref[0, :, 2 * D_MODEL:3 * D_MODEL]
    o_ref[...] = x_ref[...] + gate * _rms(y, gain_ref[...])


def _mix_out(x, mod, o_rwkv, g_rwkv, o_na, o_ret, z, wa, wb, wc, wo, gain, n_ctx_tiles):
    n = x.shape[0]
    tm = 256
    per = TOKEN_TILE // tm
    midx = functools.partial(_mod_index, n_ctx_tiles=n_ctx_tiles * per, tiles_per_lat=per)
    row = lambda w: pl.BlockSpec((tm, w), lambda i: (i, 0))
    gate_spec = lambda g: pl.BlockSpec((tm, D_MODEL), lambda i: (i, OFF_GATE // D_MODEL + g))
    wspec = lambda a, b: pl.BlockSpec((a, b), lambda i: (0, 0))
    return pl.pallas_call(
        _mix_out_kernel,
        grid=(n // tm,),
        in_specs=[
            row(D_MODEL),
            pl.BlockSpec((1, 1, 6 * D_MODEL), lambda i: (midx(i), 0, 0)),
            row(D_BRANCH), row(D_BRANCH), row(D_BRANCH), row(D_BRANCH),
            gate_spec(0), gate_spec(1), gate_spec(2),
            wspec(D_BRANCH, D_MODEL), wspec(D_BRANCH, D_MODEL), wspec(D_BRANCH, D_MODEL),
            wspec(D_MODEL, D_MODEL), wspec(1, D_MODEL),
        ],
        out_specs=row(D_MODEL),
        out_shape=jax.ShapeDtypeStruct((n, D_MODEL), F32),
        compiler_params=_cparams(("parallel",)),
    )(x, mod, o_rwkv, g_rwkv, o_na, o_ret, z, z, z, wa, wb, wc, wo, gain)


def _gelu_tanh(x):
    return x * (0.5 * (1.0 + jnp.tanh(math.sqrt(2.0 / math.pi) * (x + 0.044715 * (x * x * x)))))


def _ffn_kernel(x_ref, m_ref, gpre_ref, wv_ref, wg_ref, cv_ref, cg_ref, wd_ref, gpost_ref, o_ref,
                h_scr, acc_scr, *, n_ctx_tiles, t_ctx, t_lat):
    j = pl.program_id(1)

    @pl.when(j == 0)
    def _():
        y = _rms(x_ref[...], gpre_ref[...])
        h = y * (1.0 + m_ref[0, :, 4 * D_MODEL:5 * D_MODEL]) + m_ref[0, :, 3 * D_MODEL:4 * D_MODEL]
        h_scr[...] = h.astype(BF16)
        acc_scr[...] = jnp.zeros_like(acc_scr)

    first, last = _seq_edges(pl.program_id(0), n_ctx_tiles, t_ctx, t_lat, x_ref.shape[0])
    h = h_scr[...]
    val = _dwconv3(_dot(h, wv_ref[...]), cv_ref[...], first, last)
    gate = _dwconv3(_dot(h, wg_ref[...]), cg_ref[...], first, last)
    acc_scr[...] += _dot((_gelu_tanh(gate) * val).astype(BF16), wd_ref[...])

    @pl.when(j == pl.num_programs(1) - 1)
    def _():
        gate2 = m_ref[0, :, 5 * D_MODEL:6 * D_MODEL]
        o_ref[...] = x_ref[...] + gate2 * _rms(acc_scr[...], gpost_ref[...])


def _ffn(x, mod, gpre, w_up, conv_w, w_down, gpost, n_ctx_tiles, t_ctx, t_lat):
    n = x.shape[0]
    tm, tf = TOKEN_TILE, 256
    nf = D_FF // tf
    midx = functools.partial(_mod_index, n_ctx_tiles=n_ctx_tiles, tiles_per_lat=1)
    return pl.pallas_call(
        functools.partial(_ffn_kernel, n_ctx_tiles=n_ctx_tiles, t_ctx=t_ctx, t_lat=t_lat),
        grid=(n // tm, nf),
        in_specs=[
            pl.BlockSpec((tm, D_MODEL), lambda i, j: (i, 0)),
            pl.BlockSpec((1, 1, 6 * D_MODEL), lambda i, j: (midx(i), 0, 0)),
            pl.BlockSpec((1, D_MODEL), lambda i, j: (0, 0)),
            pl.BlockSpec((D_MODEL, tf), lambda i, j: (0, j)),
            pl.BlockSpec((D_MODEL, tf), lambda i, j: (0, nf + j)),
            pl.BlockSpec((3, tf), lambda i, j: (0, j)),
            pl.BlockSpec((3, tf), lambda i, j: (0, nf + j)),
            pl.BlockSpec((tf, D_MODEL), lambda i, j: (j, 0)),
            pl.BlockSpec((1, D_MODEL), lambda i, j: (0, 0)),
        ],
        out_specs=pl.BlockSpec((tm, D_MODEL), lambda i, j: (i, 0)),
        out_shape=jax.ShapeDtypeStruct((n, D_MODEL), F32),
        scratch_shapes=[pltpu.VMEM((tm, D_MODEL), BF16), pltpu.VMEM((tm, D_MODEL), F32)],
        compiler_params=_cparams(("parallel", "arbitrary")),
    )(x, mod, gpre, w_up, w_up, conv_w, conv_w, w_down, gpost)


def _to_heads(x, b, t, parts):
    return x.reshape(b, t, parts, N_HEAD, HEAD_DIM).transpose(2, 0, 3, 1, 4)


def _from_heads(x):
    b, h, t, dh = x.shape
    return x.transpose(0, 2, 1, 3).reshape(b * t, h * dh)


def _to_scan(x, b, t):
    return x.reshape(b, t, N_HEAD, HEAD_DIM).transpose(1, 3, 0, 2).reshape(t, HEAD_DIM, b * N_HEAD)


def _from_scan(x, b):
    t = x.shape[0]
    x = x[:, :, :b * N_HEAD].reshape(t, HEAD_DIM, b, N_HEAD)
    return x.transpose(2, 0, 3, 1).reshape(b * t, D_BRANCH)


def _scan_inputs(rkv, dec, alpha, b, t):
    r, k, v = (_to_scan(rkv[:, i * D_BRANCH:(i + 1) * D_BRANCH], b, t) for i in range(3))
    fwd = (r, k, v, _to_scan(dec[0], b, t), _to_scan(alpha[0], b, t))
    bwd = (r, k, v, _to_scan(dec[1], b, t), _to_scan(alpha[1], b, t))
    xs = jnp.stack([jnp.concatenate([f, jnp.flip(w, 0)], axis=-1) for f, w in zip(fwd, bwd)], axis=1)
    pad = (-xs.shape[-1]) % LANES
    if pad:
        xs = jnp.pad(xs, ((0, 0), (0, 0), (0, 0), (0, pad)))
    return xs


def _lane_param(p, b, n_lane, dirs):
    x = jnp.tile(p.reshape(N_HEAD, HEAD_DIM).T[:, None, :], (1, dirs * b, 1)).reshape(HEAD_DIM, dirs * b * N_HEAD)
    return jnp.pad(x, ((0, 0), (0, n_lane - x.shape[1])))


def _rwkv_branch(rkv, dec, alpha, lp, s0, b, t):
    xs = _scan_inputs(rkv, dec, alpha, b, t)
    n_lane = xs.shape[-1]
    n_fwd = b * N_HEAD
    scan_p = jnp.stack([_lane_param(lp['k_k'], b, n_lane, 2), _lane_param(lp['k_a'], b, n_lane, 2)])
    if s0 is None:
        s0_l = jnp.zeros((HEAD_DIM, HEAD_DIM, n_lane), F32)
    else:
        s0_l = s0.transpose(3, 4, 1, 0, 2).reshape(HEAD_DIM, HEAD_DIM, 2 * n_fwd)
        s0_l = jnp.pad(s0_l, ((0, 0), (0, 0), (0, n_lane - 2 * n_fwd)))
    y, s_fin = _wkv_scan(xs, scan_p, s0_l)
    lb = n_fwd if n_fwd % LANES == 0 else LANES
    post_p = jnp.stack([_lane_param(lp['ln_g'], b, lb, 1), _lane_param(lp['ln_b'], b, lb, 1),
                        _lane_param(lp['r_k'], b, lb, 1)])
    o = _from_scan(_rwkv_post(y, xs, post_p, n_fwd), b)
    s_fin = s_fin[:, :, :2 * n_fwd].reshape(HEAD_DIM, HEAD_DIM, 2, b, N_HEAD).transpose(3, 2, 4, 0, 1)
    return o, s_fin


def kernel(x_prompt, x_sample, cache_na_k, cache_na_v, state_rwkv, state_ret, c, c_ctx, ada_w, ada_b, norm_mix_pre, norm_mix_post, norm_ffn_pre, norm_ffn_post, w_in, rwkv_conv, rwkv_w0, rwkv_w_up, rwkv_a0, rwkv_a_up, rwkv_g_up, rwkv_k_k, rwkv_k_a, rwkv_r_k, rwkv_ln_g, rwkv_ln_b, na_rpb, ret_decay, ret_gn, w_o_rwkv, w_o_na, w_o_ret, w_out, ffn_up, ffn_conv, ffn_down):
    bc, tc, _ = x_prompt.shape
    bl, tl, _ = x_sample.shape
    nc, nl = bc * tc, bl * tl
    assert tl == TOKEN_TILE and TOKEN_TILE % tc == 0 and nc % TOKEN_TILE == 0 and 1 + bl <= MOD_ROWS
    assert tc & (tc - 1) == 0 and tl % GRID_W == 0
    n_ctx_tiles = nc // TOKEN_TILE

    x = jnp.concatenate([x_prompt.reshape(nc, D_MODEL), x_sample.reshape(nl, D_MODEL)], axis=0)
    cvec = jnp.concatenate([c_ctx[None, :], c, jnp.zeros((MOD_ROWS - 1 - bl, D_MODEL), F32)], axis=0)
    mods = _ada_modulation(cvec, ada_w, ada_b).reshape(DEPTH, MOD_ROWS, 1, 6 * D_MODEL)
    rope_cos, rope_sin = _rope_tables(tl)
    row = lambda p: p.reshape(1, -1)

    new_k, new_v, new_rw, new_rt = [], [], [], []
    for l in range(DEPTH):
        mod = mods[l]
        zpad = jnp.zeros((D_MODEL, RWKV_PAD - RWKV_COLS), F32)
        o_rw, o_na_, o_rt = RWKV_COLS, RWKV_COLS + 3 * D_BRANCH, RWKV_COLS + 7 * D_BRANCH
        w_in_p = jnp.concatenate([w_in[l][:, o_rt:], w_in[l][:, :o_rw], zpad, w_in[l][:, o_rw:o_rt]], axis=1).astype(BF16)
        z = _in_proj(x, mod, row(norm_mix_pre[l]), w_in_p, n_ctx_tiles)

        conv_w = jnp.pad(rwkv_conv[l], ((0, 0), (0, RWKV_PAD - RWKV_COLS)))
        rkv = _rwkv_conv(z, conv_w, n_ctx_tiles, tc, tl)
        wup_pad = jnp.pad(rwkv_w_up[l], ((0, 0), (0, LANES - W_LORA), (0, 0))).astype(BF16)
        aup_pad = jnp.pad(rwkv_a_up[l], ((0, 0), (W_LORA, LANES - W_LORA - A_LORA), (0, 0))).astype(BF16)
        dec0, dec1, al0, al1, g_rwkv = _rwkv_lora(
            z, conv_w[:, 3 * D_BRANCH:3 * D_BRANCH + 2 * LANES], wup_pad, aup_pad, rwkv_g_up[l].astype(BF16),
            rwkv_w0[l], rwkv_a0[l], n_ctx_tiles, tc, tl)
        lp = {'k_k': rwkv_k_k[l], 'k_a': rwkv_k_a[l], 'ln_g': rwkv_ln_g[l], 'ln_b': rwkv_ln_b[l],
              'r_k': rwkv_r_k[l].reshape(-1)}
        o_a_ctx, st_rw = _rwkv_branch(rkv[:nc], (dec0[:nc], dec1[:nc]), (al0[:nc], al1[:nc]), lp, None, bc, tc)
        o_a_lat, _ = _rwkv_branch(rkv[nc:], (dec0[nc:], dec1[nc:]), (al0[nc:], al1[nc:]), lp, state_rwkv[:, l], bl, tl)
        o_rwkv = jnp.concatenate([o_a_ctx, o_a_lat], axis=0)

        qkv_ctx = _to_heads(z[:nc, OFF_NA:OFF_NA + 3 * D_BRANCH], bc, tc, 3)
        qkv_lat = _to_heads(z[nc:, OFF_NA:OFF_NA + 3 * D_BRANCH], bl, tl, 3)
        o_b_ctx = _ctx_attention(qkv_ctx)
        bias_tab = _na_bias_table(na_rpb[l], tl // GRID_W)
        o_b_lat = _na_attention(qkv_lat, cache_na_k, cache_na_v, bias_tab, l)
        o_na = jnp.concatenate([_from_heads(o_b_ctx), _from_heads(o_b_lat)], axis=0)

        dec_ret = jnp.broadcast_to(ret_decay[l][:, :, None, None], (2, N_HEAD, 8, tl))
        gn = ret_gn[l].reshape(N_HEAD, 1, HEAD_DIM)
        x_ctx = _to_heads(z[:nc, OFF_RET:OFF_RET + 4 * D_BRANCH], bc, tc, 4)
        o_c_ctx, st_rt = _retention(x_ctx, dec_ret[..., :tc], gn, jnp.zeros((bc, 2, N_HEAD, HEAD_DIM, HEAD_DIM), F32), N_HEAD)
        qk_rot = _rope_qk(z, rope_cos, rope_sin, n_ctx_tiles, bl)
        x_lat = _to_heads(jnp.concatenate([qk_rot, z[nc:, OFF_RET + 2 * D_BRANCH:OFF_RET + 4 * D_BRANCH]], axis=1), bl, tl, 4)
        o_c_lat, _ = _retention(x_lat, dec_ret, gn, state_ret[:, l], 1)
        o_ret = jnp.concatenate([_from_heads(o_c_ctx), _from_heads(o_c_lat)], axis=0)

        x = _mix_out(x, mod, o_rwkv, g_rwkv, o_na, o_ret, z, w_o_rwkv[l].astype(BF16), w_o_na[l].astype(BF16),
                     w_o_ret[l].astype(BF16), w_out[l].astype(BF16), row(norm_mix_post[l]), n_ctx_tiles)
        x = _ffn(x, mod, row(norm_ffn_pre[l]), ffn_up[l].astype(BF16), ffn_conv[l], ffn_down[l].astype(BF16),
                 row(norm_ffn_post[l]), n_ctx_tiles, tc, tl)

        new_k.append(qkv_ctx[1])
        new_v.append(qkv_ctx[2])
        new_rw.append(st_rw)
        new_rt.append(st_rt)

    return (x[:nc].reshape(bc, tc, D_MODEL), x[nc:].reshape(bl, tl, D_MODEL), jnp.stack(new_k, axis=1),
            jnp.stack(new_v, axis=1), jnp.stack(new_rw, axis=1), jnp.stack(new_rt, axis=1))
```

```python
import functools
import math

import jax
import jax.numpy as jnp
from jax import lax
from jax.experimental import pallas as pl
from jax.experimental.pallas import tpu as pltpu

F32 = jnp.float32
BF16 = jnp.bfloat16

D_MODEL = 1024
DEPTH = 2
N_HEAD = 8
HEAD_DIM = 64
D_BRANCH = N_HEAD * HEAD_DIM
GRID_W = 64
NA_KH = 8
NA_KW = 16
W_LORA = 64
A_LORA = 64
G_LORA = 128
D_FF = 2816
ROPE_BASE = 10000.0
NEG_BIG = -1e9
RWKV_LN_EPS = 64e-5
RET_GN_EPS = 1e-5
RMS_EPS = 1e-6

RWKV_COLS = 3 * D_BRANCH + W_LORA + A_LORA + G_LORA
RWKV_PAD = 2048
OFF_GATE = 0
OFF_RWKV = 3 * D_MODEL
OFF_NA = OFF_RWKV + RWKV_PAD
OFF_RET = OFF_NA + 3 * D_BRANCH
Z_COLS = OFF_RET + 4 * D_BRANCH

LANES = 128
TOKEN_TILE = 1024
MOD_ROWS = 8
VMEM_LIMIT = 56 * 1024 * 1024


def _cparams(sem):
    return pltpu.CompilerParams(dimension_semantics=sem, vmem_limit_bytes=VMEM_LIMIT)


def _rms(x, g):
    return x * lax.rsqrt(jnp.mean(x * x, axis=-1, keepdims=True) + RMS_EPS) * g


def _softplus(x):
    return jnp.maximum(x, 0.0) + jnp.log1p(jnp.exp(-jnp.abs(x)))


def _dot(a, b):
    return jnp.dot(a, b, preferred_element_type=F32)


def _dot_nt(a, b):
    return lax.dot_general(a, b, (((1,), (1,)), ((), ())), preferred_element_type=F32)


def _dot_tn(a, b):
    return lax.dot_general(a, b, (((0,), (0,)), ((), ())), preferred_element_type=F32)


def _mod_index(i, n_ctx_tiles, tiles_per_lat):
    return jnp.where(i < n_ctx_tiles, 0, 1 + jnp.maximum(i - n_ctx_tiles, 0) // tiles_per_lat)


def _seq_edges(i, n_ctx_tiles, t_ctx, t_lat, tm):
    seqlen = jnp.where(i < n_ctx_tiles, t_ctx, t_lat)
    pos = lax.broadcasted_iota(jnp.int32, (tm, 1), 0) & (seqlen - 1)
    return pos == 0, pos == seqlen - 1


def _dwconv3(u, w, first, last):
    tm = u.shape[0]
    prev = jnp.where(first, 0.0, pltpu.roll(u, 1, 0))
    nxt = jnp.where(last, 0.0, pltpu.roll(u, tm - 1, 0))
    return prev * w[0:1] + u * w[1:2] + nxt * w[2:3]


def _ada_kernel(c_ref, w_ref, b_ref, o_ref):
    c = c_ref[...]
    s = (c * jax.nn.sigmoid(c)).astype(BF16)
    o_ref[0] = _dot(s, w_ref[0].astype(BF16)) + b_ref[0]


def _ada_modulation(cvec, ada_w, ada_b):
    tn = 512
    n_out = 6 * D_MODEL
    return pl.pallas_call(
        _ada_kernel,
        grid=(DEPTH, n_out // tn),
        in_specs=[
            pl.BlockSpec((MOD_ROWS, D_MODEL), lambda l, j: (0, 0)),
            pl.BlockSpec((1, D_MODEL, tn), lambda l, j: (l, 0, j)),
            pl.BlockSpec((1, 1, tn), lambda l, j: (l, 0, j)),
        ],
        out_specs=pl.BlockSpec((1, MOD_ROWS, tn), lambda l, j: (l, 0, j)),
        out_shape=jax.ShapeDtypeStruct((DEPTH, MOD_ROWS, n_out), F32),
        compiler_params=_cparams(("parallel", "parallel")),
    )(cvec, ada_w, ada_b.reshape(DEPTH, 1, n_out))


def _in_proj_kernel(x_ref, m_ref, g_ref, w_ref, o_ref, h_scr):
    @pl.when(pl.program_id(1) == 0)
    def _():
        y = _rms(x_ref[...], g_ref[...])
        h = y * (1.0 + m_ref[0, :, D_MODEL:2 * D_MODEL]) + m_ref[0, :, 0:D_MODEL]
        h_scr[...] = h.astype(BF16)

    o_ref[...] = _dot(h_scr[...], w_ref[...])


def _in_proj(x, mod, gain, w_bf16, n_ctx_tiles):
    n = x.shape[0]
    tm, tn = TOKEN_TILE, 512
    midx = functools.partial(_mod_index, n_ctx_tiles=n_ctx_tiles, tiles_per_lat=1)
    return pl.pallas_call(
        _in_proj_kernel,
        grid=(n // tm, Z_COLS // tn),
        in_specs=[
            pl.BlockSpec((tm, D_MODEL), lambda i, j: (i, 0)),
            pl.BlockSpec((1, 1, 6 * D_MODEL), lambda i, j: (midx(i), 0, 0)),
            pl.BlockSpec((1, D_MODEL), lambda i, j: (0, 0)),
            pl.BlockSpec((D_MODEL, tn), lambda i, j: (0, j)),
        ],
        out_specs=pl.BlockSpec((tm, tn), lambda i, j: (i, j)),
        out_shape=jax.ShapeDtypeStruct((n, Z_COLS), F32),
        scratch_shapes=[pltpu.VMEM((tm, D_MODEL), BF16)],
        compiler_params=_cparams(("parallel", "arbitrary")),
    )(x, mod, gain, w_bf16)


def _rwkv_conv_kernel(z_ref, w_ref, o_ref, *, n_ctx_tiles, t_ctx, t_lat):
    first, last = _seq_edges(pl.program_id(0), n_ctx_tiles, t_ctx, t_lat, z_ref.shape[0])
    o_ref[...] = _dwconv3(z_ref[...], w_ref[...], first, last)


def _rwkv_conv(z, conv_w, n_ctx_tiles, t_ctx, t_lat):
    n = z.shape[0]
    tm, tn = TOKEN_TILE, D_BRANCH
    cb = OFF_RWKV // tn
    return pl.pallas_call(
        functools.partial(_rwkv_conv_kernel, n_ctx_tiles=n_ctx_tiles, t_ctx=t_ctx, t_lat=t_lat),
        grid=(n // tm, 3),
        in_specs=[
            pl.BlockSpec((tm, tn), lambda i, j: (i, cb + j)),
            pl.BlockSpec((3, tn), lambda i, j: (0, j)),
        ],
        out_specs=pl.BlockSpec((tm, tn), lambda i, j: (i, j)),
        out_shape=jax.ShapeDtypeStruct((n, 3 * D_BRANCH), F32),
        compiler_params=_cparams(("parallel", "parallel")),
    )(z, conv_w)


def _rwkv_lora_kernel(z_ref, cw_ref, wup_ref, aup_ref, gup_ref, w0_ref, a0_ref,
                      dec_ref, al_ref, g_ref, *, n_ctx_tiles, t_ctx, t_lat):
    first, last = _seq_edges(pl.program_id(0), n_ctx_tiles, t_ctx, t_lat, z_ref.shape[0])
    u = _dwconv3(z_ref[...], cw_ref[...], first, last)
    wa = u[:, 0:LANES]
    tw = jnp.tanh(wa).astype(BF16)
    ab = wa.astype(BF16)
    sg = jax.nn.sigmoid(u[:, LANES:2 * LANES]).astype(BF16)
    for d in range(2):
        x = w0_ref[d:d + 1, :] + _dot(tw, wup_ref[d])
        w_log = -_softplus(-x) - 0.5
        dec_ref[d] = jnp.exp(-jnp.exp(w_log))
        al_ref[d] = jax.nn.sigmoid(a0_ref[d:d + 1, :] + _dot(ab, aup_ref[d]))
    g_ref[...] = _dot(sg, gup_ref[...])


def _rwkv_lora(z, conv_w_lora, wup_pad, aup_pad, gup, w0, a0, n_ctx_tiles, t_ctx, t_lat):
    n = z.shape[0]
    tm = TOKEN_TILE
    wl = 2 * LANES
    cb = (OFF_RWKV + 3 * D_BRANCH) // wl
    out2 = jax.ShapeDtypeStruct((2, n, D_BRANCH), F32)
    ospec2 = pl.BlockSpec((2, tm, D_BRANCH), lambda i: (0, i, 0))
    full2 = lambda shape: pl.BlockSpec(shape, lambda i: (0, 0))
    full3 = lambda shape: pl.BlockSpec(shape, lambda i: (0, 0, 0))
    return pl.pallas_call(
        functools.partial(_rwkv_lora_kernel, n_ctx_tiles=n_ctx_tiles, t_ctx=t_ctx, t_lat=t_lat),
        grid=(n // tm,),
        in_specs=[
            pl.BlockSpec((tm, wl), lambda i: (i, cb)),
            full2((3, wl)),
            full3((2, LANES, D_BRANCH)),
            full3((2, LANES, D_BRANCH)),
            full2((G_LORA, D_BRANCH)),
            full2((2, D_BRANCH)),
            full2((2, D_BRANCH)),
        ],
        out_specs=[ospec2, ospec2, pl.BlockSpec((tm, D_BRANCH), lambda i: (i, 0))],
        out_shape=[out2, out2, jax.ShapeDtypeStruct((n, D_BRANCH), F32)],
        compiler_params=_cparams(("parallel",)),
    )(z, conv_w_lora, wup_pad, aup_pad, gup, w0, a0)


HALF = LANES // 2


def _fold_sum(x, lane, fold):
    s = HALF // fold
    while s < HALF:
        x = x + jnp.where((lane & s) != 0, pltpu.roll(x, s, 1), pltpu.roll(x, LANES - s, 1))
        s *= 2
    return x


def _wkv_kernel(xkf_ref, xkb_ref, xvf_ref, xvb_ref, p_ref, s0_ref, yf_ref, yb_ref, st_ref, s_scr, op_scr,
                *, tb, fold, unroll):
    t = pl.program_id(1)

    @pl.when(t == 0)
    def _():
        s_scr[...] = s0_ref[...]

    lane = lax.broadcasted_iota(jnp.int32, (1, LANES), 1)
    is_fwd = lane < HALF
    ksum = lambda x: _fold_sum(jnp.sum(x, axis=0, keepdims=True), lane, fold)
    k_k = p_ref[0]
    k_a = p_ref[1]

    def step(j, carry):
        jb = tb - 1 - j
        r = jnp.where(is_fwd, xkf_ref[j, 0], xkb_ref[jb, 0])
        k = jnp.where(is_fwd, xkf_ref[j, 1], xkb_ref[jb, 1])
        w = jnp.where(is_fwd, xkf_ref[j, 2], xkb_ref[jb, 2])
        a = jnp.where(is_fwd, xkf_ref[j, 3], xkb_ref[jb, 3])
        kk = k * k_k
        kk = kk / jnp.maximum(jnp.sqrt(ksum(kk * kk)), 1e-12)
        op_scr[0] = -kk
        op_scr[1] = w
        op_scr[2] = kk * a
        op_scr[3] = k * (1.0 + (a - 1.0) * k_a)
        op_scr[4] = r

        def vstep(vi, c):
            sv = s_scr[vi]
            sa = ksum(sv * op_scr[0])
            vrow = jnp.where(is_fwd, xvf_ref[j, pl.ds(vi, 1), :], xvb_ref[jb, pl.ds(vi, 1), :])
            sn = sv * op_scr[1] + sa * op_scr[2] + vrow * op_scr[3]
            s_scr[vi] = sn
            yrow = ksum(sn * op_scr[4])
            yf_ref[j, pl.ds(vi, 1), :] = yrow
            yb_ref[jb, pl.ds(vi, 1), :] = yrow
            return c

        lax.fori_loop(0, HEAD_DIM, vstep, 0, unroll=unroll)
        return carry

    lax.fori_loop(0, tb, step, 0)

    @pl.when(t == pl.num_programs(1) - 1)
    def _():
        st_ref[...] = s_scr[...]


def _wkv_scan(xk, xv, params, s0, fold):
    t_len, _, ks, n_lane = xk.shape
    tb = 8
    nt = t_len // tb
    y_shape = jax.ShapeDtypeStruct((t_len, HEAD_DIM, n_lane), F32)
    return pl.pallas_call(
        functools.partial(_wkv_kernel, tb=tb, fold=fold, unroll=8),
        grid=(n_lane // LANES, nt),
        in_specs=[
            pl.BlockSpec((tb, 4, ks, LANES), lambda c, t: (t, 0, 0, c)),
            pl.BlockSpec((tb, 4, ks, LANES), lambda c, t: (nt - 1 - t, 0, 0, c)),
            pl.BlockSpec((tb, HEAD_DIM, LANES), lambda c, t: (t, 0, c)),
            pl.BlockSpec((tb, HEAD_DIM, LANES), lambda c, t: (nt - 1 - t, 0, c)),
            pl.BlockSpec((2, ks, LANES), lambda c, t: (0, 0, c)),
            pl.BlockSpec((HEAD_DIM, ks, LANES), lambda c, t: (0, 0, c)),
        ],
        out_specs=[
            pl.BlockSpec((tb, HEAD_DIM, LANES), lambda c, t: (t, 0, c)),
            pl.BlockSpec((tb, HEAD_DIM, LANES), lambda c, t: (nt - 1 - t, 0, c)),
            pl.BlockSpec((HEAD_DIM, ks, LANES), lambda c, t: (0, 0, c)),
        ],
        out_shape=[y_shape, y_shape, jax.ShapeDtypeStruct((HEAD_DIM, ks, n_lane), F32)],
        scratch_shapes=[pltpu.VMEM((HEAD_DIM, ks, LANES), F32), pltpu.VMEM((5, ks, LANES), F32)],
        compiler_params=_cparams(("parallel", "arbitrary")),
    )(xk, xk, xv, xv, params, s0)


def _rwkv_post_kernel(yf_ref, yb_ref, xk_ref, xv_ref, pk_ref, pv_ref, o_ref, *, tb, fold):
    lane = lax.broadcasted_iota(jnp.int32, (1, LANES), 1)
    ln_g = pv_ref[0]
    ln_b = pv_ref[1]
    r_k = pk_ref[...]
    for j in range(tb):
        y = yf_ref[j] + pltpu.roll(yb_ref[j], HALF, 1)
        mu = jnp.mean(y, axis=0, keepdims=True)
        dlt = y - mu
        var = jnp.mean(dlt * dlt, axis=0, keepdims=True)
        o = dlt * lax.rsqrt(var + RWKV_LN_EPS) * ln_g + ln_b
        rk = _fold_sum(jnp.sum(xk_ref[j, 0] * xk_ref[j, 1] * r_k, axis=0, keepdims=True), lane, fold)
        o_ref[j] = o + rk * xv_ref[j]


def _rwkv_post(yf, yb, xk, xv, pk, pv, fold):
    t_len, _, n_lane = yf.shape
    ks = xk.shape[2]
    tb = 8
    yspec = pl.BlockSpec((tb, HEAD_DIM, LANES), lambda c, i: (i, 0, c))
    return pl.pallas_call(
        functools.partial(_rwkv_post_kernel, tb=tb, fold=fold),
        grid=(n_lane // LANES, t_len // tb),
        in_specs=[
            yspec, yspec,
            pl.BlockSpec((tb, 4, ks, LANES), lambda c, i: (i, 0, 0, c)),
            yspec,
            pl.BlockSpec((ks, LANES), lambda c, i: (0, c)),
            pl.BlockSpec((2, HEAD_DIM, LANES), lambda c, i: (0, 0, c)),
        ],
        out_specs=yspec,
        out_shape=jax.ShapeDtypeStruct((t_len, HEAD_DIM, n_lane), F32),
        compiler_params=_cparams(("parallel", "parallel")),
    )(yf, yb, xk, xv, pk, pv)


def _head(h):
    return slice(h * HEAD_DIM, (h + 1) * HEAD_DIM)


def _ctx_attn_kernel(q_ref, k_ref, v_ref, o_ref, ko_ref, vo_ref):
    scale = HEAD_DIM ** -0.5
    for h in range(N_HEAD):
        k = k_ref[:, _head(h)]
        v = v_ref[:, _head(h)]
        ko_ref[0, h] = k
        vo_ref[0, h] = v
        s = _dot_nt(q_ref[:, _head(h)].astype(BF16), k.astype(BF16)) * scale
        p = jnp.exp(s - jnp.max(s, axis=-1, keepdims=True))
        p = p / jnp.sum(p, axis=-1, keepdims=True)
        o_ref[:, _head(h)] = _dot(p.astype(BF16), v.astype(BF16))


def _ctx_attention(z, n_rows, b, t):
    cb = OFF_NA // D_BRANCH
    zspec = lambda c: pl.BlockSpec((t, D_BRANCH), lambda i: (i, cb + c))
    kv_shape = jax.ShapeDtypeStruct((b, N_HEAD, t, HEAD_DIM), F32)
    kv_spec = pl.BlockSpec((1, N_HEAD, t, HEAD_DIM), lambda i: (i, 0, 0, 0))
    return pl.pallas_call(
        _ctx_attn_kernel,
        grid=(b,),
        in_specs=[zspec(0), zspec(1), zspec(2)],
        out_specs=[pl.BlockSpec((t, D_BRANCH), lambda i: (i, 0)), kv_spec, kv_spec],
        out_shape=[jax.ShapeDtypeStruct((n_rows, D_BRANCH), F32), kv_shape, kv_shape],
        compiler_params=_cparams(("parallel",)),
    )(z, z, z)


def _na_kernel(q_ref, k_ref, v_ref, kc_ref, vc_ref, tab_ref, prev_ref, o_ref, *, rows):
    del prev_ref
    scale = HEAD_DIM ** -0.5
    win = NA_KH * GRID_W
    qcol = lax.broadcasted_iota(jnp.int32, (GRID_W, win), 0)
    kcol = lax.broadcasted_iota(jnp.int32, (GRID_W, win), 1) & (GRID_W - 1)
    c_start = jnp.clip(qcol - NA_KW // 2, 0, GRID_W - NA_KW)
    col_valid = (kcol >= c_start) & (kcol < c_start + NA_KW)
    for h in range(N_HEAD):
        kc = kc_ref[0, 0, h].astype(BF16)
        vc = vc_ref[0, 0, h].astype(BF16)

        def row_block(r, carry):
            rs = jnp.clip(r - NA_KH // 2, 0, rows - NA_KH)
            q_rows = pl.ds(pl.multiple_of(r * GRID_W, GRID_W), GRID_W)
            w_rows = pl.ds(pl.multiple_of(rs * GRID_W, GRID_W), win)
            q = q_ref[q_rows, _head(h)].astype(BF16)
            kw = k_ref[w_rows, _head(h)].astype(BF16)
            vw = v_ref[w_rows, _head(h)].astype(BF16)
            first = rs - r + NA_KH - 1
            bias = jnp.concatenate([tab_ref[h, first + 2 * i] for i in range(NA_KH // 2)], axis=1)
            s_loc = jnp.where(col_valid, _dot_nt(q, kw) * scale + bias, NEG_BIG)
            s_ctx = _dot_nt(q, kc) * scale
            m = jnp.maximum(jnp.max(s_loc, axis=-1, keepdims=True), jnp.max(s_ctx, axis=-1, keepdims=True))
            p_loc = jnp.exp(s_loc - m)
            p_ctx = jnp.exp(s_ctx - m)
            den = jnp.sum(p_loc, axis=-1, keepdims=True) + jnp.sum(p_ctx, axis=-1, keepdims=True)
            o_ref[q_rows, _head(h)] = _dot((p_loc / den).astype(BF16), vw) + _dot((p_ctx / den).astype(BF16), vc)
            return carry

        lax.fori_loop(0, rows, row_block, 0)


def _na_attention(z, o_prev, cache_k, cache_v, bias_tab, layer, row_tile0, b, t):
    past = cache_k.shape[3]
    rows = t // GRID_W
    assert rows >= NA_KH
    cb = OFF_NA // D_BRANCH
    zspec = lambda c: pl.BlockSpec((t, D_BRANCH), lambda i: (row_tile0 + i, cb + c))
    cspec = pl.BlockSpec((1, 1, N_HEAD, past, HEAD_DIM), lambda i: (i, layer, 0, 0, 0))
    return pl.pallas_call(
        functools.partial(_na_kernel, rows=rows),
        grid=(b,),
        in_specs=[zspec(0), zspec(1), zspec(2), cspec, cspec,
                  pl.BlockSpec(bias_tab.shape, lambda i: (0, 0, 0, 0)),
                  pl.BlockSpec(memory_space=pl.ANY)],
        out_specs=pl.BlockSpec((t, D_BRANCH), lambda i: (row_tile0 + i, 0)),
        out_shape=jax.ShapeDtypeStruct(o_prev.shape, F32),
        input_output_aliases={6: 0},
        compiler_params=_cparams(("parallel",)),
    )(z, z, z, cache_k, cache_v, bias_tab, o_prev)


def _na_bias_table(rpb):
    h, ndr, _ = rpb.shape
    edge = GRID_W - NA_KW
    ext = jnp.concatenate([jnp.broadcast_to(rpb[..., :1], (h, ndr, edge)), rpb,
                           jnp.broadcast_to(rpb[..., -1:], (h, ndr, edge + 1))], axis=-1)
    skew = jnp.tile(ext, (1, 1, GRID_W))[..., :GRID_W * (2 * GRID_W - 1)].reshape(h, ndr, GRID_W, 2 * GRID_W - 1)
    toep = skew[..., GRID_W - 1:]
    return jnp.concatenate([toep[:, :-1], toep[:, 1:]], axis=-1)


def _rope_tables(t_len):
    half = HEAD_DIM // 2
    nf = half // 2
    inv = ROPE_BASE ** (-jnp.arange(nf, dtype=F32) / nf)
    t = jnp.arange(t_len)

    def tab(pos):
        ang = pos.astype(F32)[:, None] * inv[None, :]
        cos, sin = jnp.cos(ang), jnp.sin(ang)
        return jnp.concatenate([cos, cos], -1), jnp.concatenate([-sin, sin], -1)

    c_row, s_row = tab(t // GRID_W)
    c_col, s_col = tab(t % GRID_W)
    cos = jnp.tile(jnp.concatenate([c_row, c_col], -1), (1, N_HEAD))
    sin = jnp.tile(jnp.concatenate([s_row, s_col], -1), (1, N_HEAD))
    return cos, sin


def _log_sigmoid(x):
    return -_softplus(-x)


def _ret_heads(q_ref, k_ref, v_ref, g_ref, dec_ref, gn_ref, o_ref, *, t_len, qb, rope_refs=None, s0_ref=None,
               st_ref=None):
    lo = (lax.broadcasted_iota(jnp.int32, (1, LANES), 1) & 31) < 16
    for hp in range(N_HEAD // 2):
        pair = slice(hp * LANES, (hp + 1) * LANES)
        q2 = q_ref[:, pair]
        k2 = k_ref[:, pair]
        if rope_refs is not None:
            cos = rope_refs[0][:, pair]
            sin = rope_refs[1][:, pair]
            rot = lambda x: x * cos + jnp.where(lo, pltpu.roll(x, LANES - 16, 1), pltpu.roll(x, 16, 1)) * sin
            q2, k2 = rot(q2), rot(k2)
        k2 = k2 * (HEAD_DIM ** -0.5)
        for hh in range(2):
            h = 2 * hp + hh
            half = slice(hh * HEAD_DIM, (hh + 1) * HEAD_DIM)
            qh = q2[:, half].astype(BF16)
            k = k2[:, half]
            kb = k.astype(BF16)
            vb = v_ref[:, _head(h)].astype(BF16)
            lgf = _log_sigmoid(dec_ref[0, h])[0:1, :]
            lgb = _log_sigmoid(dec_ref[1, h])[0:1, :]
            lgf_h = lgf[:, 0:HEAD_DIM]
            lgb_h = lgb[:, 0:HEAD_DIM]
            for qi in range(t_len // qb):
                rows = slice(qi * qb, (qi + 1) * qb)
                q = qh[rows]
                s = _dot_nt(q, kb)
                diff = (lax.broadcasted_iota(jnp.int32, (qb, t_len), 0) + qi * qb
                        - lax.broadcasted_iota(jnp.int32, (qb, t_len), 1)).astype(F32)
                dmat = (jnp.where(diff >= 0, jnp.exp(lgf * jnp.maximum(diff, 0.0)), 0.0)
                        + jnp.where(diff <= 0, jnp.exp(lgb * jnp.maximum(-diff, 0.0)), 0.0))
                y = _dot((s * dmat).astype(BF16), vb)
                if s0_ref is not None:
                    pos = (lax.broadcasted_iota(jnp.int32, (qb, HEAD_DIM), 0) + qi * qb).astype(F32)
                    y = y + _dot(q, s0_ref[0, 0, h].astype(BF16)) * jnp.exp(lgf_h * (pos + 1.0))
                    y = y + _dot(q, s0_ref[0, 1, h].astype(BF16)) * jnp.exp(lgb_h * (t_len - pos))
                mu = jnp.mean(y, axis=-1, keepdims=True)
                dlt = y - mu
                var = jnp.mean(dlt * dlt, axis=-1, keepdims=True)
                yn = dlt * lax.rsqrt(var + RET_GN_EPS)
                g = g_ref[rows, _head(h)]
                o_ref[rows, _head(h)] = (yn * gn_ref[:, _head(h)]) * (g * jax.nn.sigmoid(g))
            if st_ref is not None:
                pos = lax.broadcasted_iota(jnp.int32, (t_len, HEAD_DIM), 0).astype(F32)
                kzf = (k * jnp.exp(lgf_h * (t_len - 1.0 - pos))).astype(BF16)
                kzb = (k * jnp.exp(lgb_h * pos)).astype(BF16)
                sf = _dot_tn(kzf, vb)
                sb = _dot_tn(kzb, vb)
                if s0_ref is not None:
                    sf = sf + s0_ref[0, 0, h] * jnp.exp(lgf_h * float(t_len))
                    sb = sb + s0_ref[0, 1, h] * jnp.exp(lgb_h * float(t_len))
                st_ref[0, 0, h] = sf
                st_ref[0, 1, h] = sb


def _ret_ctx_kernel(q_ref, k_ref, v_ref, g_ref, dec_ref, gn_ref, o_ref, st_ref, *, t_len, qb):
    _ret_heads(q_ref, k_ref, v_ref, g_ref, dec_ref, gn_ref, o_ref, t_len=t_len, qb=qb, st_ref=st_ref)


def _ret_lat_kernel(q_ref, k_ref, v_ref, g_ref, dec_ref, gn_ref, cos_ref, sin_ref, s0_ref, prev_ref, o_ref,
                    *, t_len, qb):
    del prev_ref
    _ret_heads(q_ref, k_ref, v_ref, g_ref, dec_ref, gn_ref, o_ref, t_len=t_len, qb=qb,
               rope_refs=(cos_ref, sin_ref), s0_ref=s0_ref.at[0])


def _ret_specs(t, row_tile0):
    cb = OFF_RET // D_BRANCH
    zspec = lambda c: pl.BlockSpec((t, D_BRANCH), lambda i: (row_tile0 + i, cb + c))
    return [zspec(0), zspec(1), zspec(2), zspec(3),
            pl.BlockSpec((2, N_HEAD, 8, t), lambda i: (0, 0, 0, 0)),
            pl.BlockSpec((1, D_BRANCH), lambda i: (0, 0))]


def _retention_ctx(z, dec, gn, n_rows, b, t):
    st_shape = (b, 2, N_HEAD, HEAD_DIM, HEAD_DIM)
    return pl.pallas_call(
        functools.partial(_ret_ctx_kernel, t_len=t, qb=min(t, 256)),
        grid=(b,),
        in_specs=_ret_specs(t, 0),
        out_specs=[pl.BlockSpec((t, D_BRANCH), lambda i: (i, 0)),
                   pl.BlockSpec((1,) + st_shape[1:], lambda i: (i, 0, 0, 0, 0))],
        out_shape=[jax.ShapeDtypeStruct((n_rows, D_BRANCH), F32), jax.ShapeDtypeStruct(st_shape, F32)],
        compiler_params=_cparams(("parallel",)),
    )(z, z, z, z, dec, gn)


def _retention_lat(z, o_prev, dec, gn, cos, sin, s0, layer, row_tile0, b, t):
    tab = pl.BlockSpec((t, D_BRANCH), lambda i: (0, 0))
    return pl.pallas_call(
        functools.partial(_ret_lat_kernel, t_len=t, qb=min(t, 256)),
        grid=(b,),
        in_specs=_ret_specs(t, row_tile0) + [
            tab, tab,
            pl.BlockSpec((1, 1, 2, N_HEAD, HEAD_DIM, HEAD_DIM), lambda i: (i, layer, 0, 0, 0, 0)),
            pl.BlockSpec(memory_space=pl.ANY)],
        out_specs=pl.BlockSpec((t, D_BRANCH), lambda i: (row_tile0 + i, 0)),
        out_shape=jax.ShapeDtypeStruct(o_prev.shape, F32),
        input_output_aliases={9: 0},
        compiler_params=_cparams(("parallel",)),
    )(z, z, z, z, dec, gn, cos, sin, s0, o_prev)


def _mix_out_kernel(x_ref, m_ref, oa_ref, ga_ref, ob_ref, oc_ref, g0_ref, g1_ref, g2_ref,
                    wa_ref, wb_ref, wc_ref, wo_ref, gain_ref, o_ref):
    out_a = _dot((oa_ref[...] * ga_ref[...]).astype(BF16), wa_ref[...])
    out_b = _dot(ob_ref[...].astype(BF16), wb_ref[...])
    out_c = _dot(oc_ref[...].astype(BF16), wc_ref[...])
    merged = (jax.nn.sigmoid(g0_ref[...]) * out_a + jax.nn.sigmoid(g1_ref[...]) * out_b
              + jax.nn.sigmoid(g2_ref[...]) * out_c)
    y = _dot(merged.astype(BF16), wo_ref[...])
    gate = m_ref[0, :, 2 * D_MODEL:3 * D_MODEL]
    o_ref[...] = x_ref[...] + gate * _rms(y, gain_ref[...])


def _mix_out(x, mod, o_rwkv, g_rwkv, o_na, o_ret, z, wa, wb, wc, wo, gain, n_ctx_tiles):
    n = x.shape[0]
    tm = 256
    per = TOKEN_TILE // tm
    midx = functools.partial(_mod_index, n_ctx_tiles=n_ctx_tiles * per, tiles_per_lat=per)
    row = lambda w: pl.BlockSpec((tm, w), lambda i: (i, 0))
    gate_spec = lambda g: pl.BlockSpec((tm, D_MODEL), lambda i: (i, OFF_GATE // D_MODEL + g))
    wspec = lambda a, b: pl.BlockSpec((a, b), lambda i: (0, 0))
    return pl.pallas_call(
        _mix_out_kernel,
        grid=(n // tm,),
        in_specs=[
            row(D_MODEL),
            pl.BlockSpec((1, 1, 6 * D_MODEL), lambda i: (midx(i), 0, 0)),
            row(D_BRANCH), row(D_BRANCH), row(D_BRANCH), row(D_BRANCH),
            gate_spec(0), gate_spec(1), gate_spec(2),
            wspec(D_BRANCH, D_MODEL), wspec(D_BRANCH, D_MODEL), wspec(D_BRANCH, D_MODEL),
            wspec(D_MODEL, D_MODEL), wspec(1, D_MODEL),
        ],
        out_specs=row(D_MODEL),
        out_shape=jax.ShapeDtypeStruct((n, D_MODEL), F32),
        compiler_params=_cparams(("parallel",)),
    )(x, mod, o_rwkv, g_rwkv, o_na, o_ret, z, z, z, wa, wb, wc, wo, gain)


def _gelu_tanh(x):
    return x * (0.5 * (1.0 + jnp.tanh(math.sqrt(2.0 / math.pi) * (x + 0.044715 * (x * x * x)))))


def _ffn_kernel(x_ref, m_ref, gpre_ref, wv_ref, wg_ref, cv_ref, cg_ref, wd_ref, gpost_ref, o_ref,
                h_scr, acc_scr, *, n_ctx_tiles, t_ctx, t_lat):
    j = pl.program_id(1)

    @pl.when(j == 0)
    def _():
        y = _rms(x_ref[...], gpre_ref[...])
        h = y * (1.0 + m_ref[0, :, 4 * D_MODEL:5 * D_MODEL]) + m_ref[0, :, 3 * D_MODEL:4 * D_MODEL]
        h_scr[...] = h.astype(BF16)
        acc_scr[...] = jnp.zeros_like(acc_scr)

    first, last = _seq_edges(pl.program_id(0), n_ctx_tiles, t_ctx, t_lat, x_ref.shape[0])
    h = h_scr[...]
    val = _dwconv3(_dot(h, wv_ref[...]), cv_ref[...], first, last)
    gate = _dwconv3(_dot(h, wg_ref[...]), cg_ref[...], first, last)
    acc_scr[...] += _dot((_gelu_tanh(gate) * val).astype(BF16), wd_ref[...])

    @pl.when(j == pl.num_programs(1) - 1)
    def _():
        gate2 = m_ref[0, :, 5 * D_MODEL:6 * D_MODEL]
        o_ref[...] = x_ref[...] + gate2 * _rms(acc_scr[...], gpost_ref[...])


def _ffn(x, mod, gpre, w_up, conv_w, w_down, gpost, n_ctx_tiles, t_ctx, t_lat):
    n = x.shape[0]
    tm, tf = TOKEN_TILE, 256
    nf = D_FF // tf
    midx = functools.partial(_mod_index, n_ctx_tiles=n_ctx_tiles, tiles_per_lat=1)
    return pl.pallas_call(
        functools.partial(_ffn_kernel, n_ctx_tiles=n_ctx_tiles, t_ctx=t_ctx, t_lat=t_lat),
        grid=(n // tm, nf),
        in_specs=[
            pl.BlockSpec((tm, D_MODEL), lambda i, j: (i, 0)),
            pl.BlockSpec((1, 1, 6 * D_MODEL), lambda i, j: (midx(i), 0, 0)),
            pl.BlockSpec((1, D_MODEL), lambda i, j: (0, 0)),
            pl.BlockSpec((D_MODEL, tf), lambda i, j: (0, j)),
            pl.BlockSpec((D_MODEL, tf), lambda i, j: (0, nf + j)),
            pl.BlockSpec((3, tf), lambda i, j: (0, j)),
            pl.BlockSpec((3, tf), lambda i, j: (0, nf + j)),
            pl.BlockSpec((tf, D_MODEL), lambda i, j: (j, 0)),
            pl.BlockSpec((1, D_MODEL), lambda i, j: (0, 0)),
        ],
        out_specs=pl.BlockSpec((tm, D_MODEL), lambda i, j: (i, 0)),
        out_shape=jax.ShapeDtypeStruct((n, D_MODEL), F32),
        scratch_shapes=[pltpu.VMEM((tm, D_MODEL), BF16), pltpu.VMEM((tm, D_MODEL), F32)],
        compiler_params=_cparams(("parallel", "arbitrary")),
    )(x, mod, gpre, w_up, w_up, conv_w, conv_w, w_down, gpost)


class _ScanGeom:
    def __init__(self, b):
        self.b = b
        self.pairs = b * N_HEAD
        self.fold = max(1, HALF // self.pairs)
        self.ks = HEAD_DIM // self.fold
        self.pt = HALF // self.fold
        assert self.pairs % self.pt == 0 and self.ks % 8 == 0
        self.tiles = self.pairs // self.pt

    def lanes(self, x, lead):
        return jnp.broadcast_to(x, lead + (self.tiles, 2, self.fold, self.pt)).reshape(lead + (self.tiles * LANES,))

    def key_rows(self, x, t):
        x = x.reshape(self.b, t, N_HEAD, self.fold, self.ks).transpose(1, 4, 3, 0, 2)
        return x.reshape(t, self.ks, self.fold, self.tiles, self.pt).transpose(0, 1, 3, 2, 4)

    def key_operand(self, xs, t):
        if xs.ndim == 2:
            x = self.key_rows(xs, t)[:, :, :, None]
        else:
            x = jnp.stack([self.key_rows(xs[0], t), self.key_rows(xs[1], t)], axis=3)
        return self.lanes(x, (t, self.ks))

    def value_operand(self, x, t):
        x = x.reshape(self.b, t, N_HEAD, HEAD_DIM).transpose(1, 3, 0, 2)
        return self.lanes(x.reshape(t, HEAD_DIM, self.tiles, 1, 1, self.pt), (t, HEAD_DIM))

    def key_param(self, p):
        x = jnp.broadcast_to(p.reshape(1, N_HEAD, self.fold, self.ks), (self.b, N_HEAD, self.fold, self.ks))
        x = x.reshape(self.tiles, self.pt, self.fold, self.ks).transpose(3, 0, 2, 1)
        return self.lanes(x[:, :, None], (self.ks,))

    def value_param(self, p):
        x = jnp.broadcast_to(p.reshape(1, N_HEAD, HEAD_DIM), (self.b, N_HEAD, HEAD_DIM))
        x = x.reshape(self.tiles, self.pt, HEAD_DIM).transpose(2, 0, 1)
        return self.lanes(x[:, :, None, None], (HEAD_DIM,))

    def state_in(self, s0):
        x = s0.reshape(self.b, 2, N_HEAD, HEAD_DIM, self.fold, self.ks).transpose(3, 5, 1, 4, 0, 2)
        x = x.reshape(HEAD_DIM, self.ks, 2, self.fold, self.tiles, self.pt).transpose(0, 1, 4, 2, 3, 5)
        return x.reshape(HEAD_DIM, self.ks, self.tiles * LANES)

    def state_out(self, s):
        x = s.reshape(HEAD_DIM, self.ks, self.tiles, 2, self.fold, self.pt).transpose(0, 1, 3, 4, 2, 5)
        x = x.reshape(HEAD_DIM, self.ks, 2, self.fold, self.b, N_HEAD).transpose(4, 2, 5, 0, 3, 1)
        return x.reshape(self.b, 2, N_HEAD, HEAD_DIM, HEAD_DIM)

    def tokens_out(self, o):
        t = o.shape[0]
        x = o.reshape(t, HEAD_DIM, self.tiles, 2 * self.fold, self.pt)[:, :, :, 0]
        return x.reshape(t, HEAD_DIM, self.b, N_HEAD).transpose(2, 0, 3, 1).reshape(self.b * t, D_BRANCH)


def _rwkv_branch(rkv, dec, alpha, lp, s0, b, t):
    geo = _ScanGeom(b)
    r, k, v = (rkv[:, i * D_BRANCH:(i + 1) * D_BRANCH] for i in range(3))
    xk = jnp.stack([geo.key_operand(r, t), geo.key_operand(k, t), geo.key_operand(dec, t),
                    geo.key_operand(alpha, t)], axis=1)
    xv = geo.value_operand(v, t)
    scan_p = jnp.stack([geo.key_param(lp['k_k']), geo.key_param(lp['k_a'])])
    if s0 is None:
        s0_l = jnp.zeros((HEAD_DIM, geo.ks, geo.tiles * LANES), F32)
    else:
        s0_l = geo.state_in(s0)
    yf, yb, s_fin = _wkv_scan(xk, xv, scan_p, s0_l, geo.fold)
    post_v = jnp.stack([geo.value_param(lp['ln_g']), geo.value_param(lp['ln_b'])])
    o = _rwkv_post(yf, yb, xk, xv, geo.key_param(lp['r_k']), post_v, geo.fold)
    return geo.tokens_out(o), geo.state_out(s_fin)


def kernel(x_prompt, x_sample, cache_na_k, cache_na_v, state_rwkv, state_ret, c, c_ctx, ada_w, ada_b, norm_mix_pre, norm_mix_post, norm_ffn_pre, norm_ffn_post, w_in, rwkv_conv, rwkv_w0, rwkv_w_up, rwkv_a0, rwkv_a_up, rwkv_g_up, rwkv_k_k, rwkv_k_a, rwkv_r_k, rwkv_ln_g, rwkv_ln_b, na_rpb, ret_decay, ret_gn, w_o_rwkv, w_o_na, w_o_ret, w_out, ffn_up, ffn_conv, ffn_down):
    bc, tc, _ = x_prompt.shape
    bl, tl, _ = x_sample.shape
    nc, nl = bc * tc, bl * tl
    assert tl == TOKEN_TILE and TOKEN_TILE % tc == 0 and nc % TOKEN_TILE == 0 and 1 + bl <= MOD_ROWS
    assert tc & (tc - 1) == 0 and tl % GRID_W == 0
    n_ctx_tiles = nc // TOKEN_TILE

    x = jnp.concatenate([x_prompt.reshape(nc, D_MODEL), x_sample.reshape(nl, D_MODEL)], axis=0)
    cvec = jnp.concatenate([c_ctx[None, :], c, jnp.zeros((MOD_ROWS - 1 - bl, D_MODEL), F32)], axis=0)
    mods = _ada_modulation(cvec, ada_w, ada_b).reshape(DEPTH, MOD_ROWS, 1, 6 * D_MODEL)
    rope_cos, rope_sin = _rope_tables(tl)
    row = lambda p: p.reshape(1, -1)

    new_k, new_v, new_rw, new_rt = [], [], [], []
    for l in range(DEPTH):
        mod = mods[l]
        zpad = jnp.zeros((D_MODEL, RWKV_PAD - RWKV_COLS), F32)
        o_rw, o_na_, o_rt = RWKV_COLS, RWKV_COLS + 3 * D_BRANCH, RWKV_COLS + 7 * D_BRANCH
        w_in_p = jnp.concatenate([w_in[l][:, o_rt:], w_in[l][:, :o_rw], zpad, w_in[l][:, o_rw:o_rt]], axis=1).astype(BF16)
        z = _in_proj(x, mod, row(norm_mix_pre[l]), w_in_p, n_ctx_tiles)

        conv_w = jnp.pad(rwkv_conv[l], ((0, 0), (0, RWKV_PAD - RWKV_COLS)))
        rkv = _rwkv_conv(z, conv_w, n_ctx_tiles, tc, tl)
        wup_pad = jnp.pad(rwkv_w_up[l], ((0, 0), (0, LANES - W_LORA), (0, 0))).astype(BF16)
        aup_pad = jnp.pad(rwkv_a_up[l], ((0, 0), (W_LORA, LANES - W_LORA - A_LORA), (0, 0))).astype(BF16)
        dec, alpha, g_rwkv = _rwkv_lora(
            z, conv_w[:, 3 * D_BRANCH:3 * D_BRANCH + 2 * LANES], wup_pad, aup_pad, rwkv_g_up[l].astype(BF16),
            rwkv_w0[l], rwkv_a0[l], n_ctx_tiles, tc, tl)
        lp = {'k_k': rwkv_k_k[l], 'k_a': rwkv_k_a[l], 'ln_g': rwkv_ln_g[l], 'ln_b': rwkv_ln_b[l],
              'r_k': rwkv_r_k[l].reshape(-1)}
        o_a_ctx, st_rw = _rwkv_branch(rkv[:nc], dec[:, :nc], alpha[:, :nc], lp, None, bc, tc)
        o_a_lat, _ = _rwkv_branch(rkv[nc:], dec[:, nc:], alpha[:, nc:], lp, state_rwkv[:, l], bl, tl)
        o_rwkv = jnp.concatenate([o_a_ctx, o_a_lat], axis=0)

        o_na, k_ctx, v_ctx = _ctx_attention(z, nc + nl, bc, tc)
        o_na = _na_attention(z, o_na, cache_na_k, cache_na_v, _na_bias_table(na_rpb[l]), l, n_ctx_tiles, bl, tl)

        dec_ret = jnp.broadcast_to(ret_decay[l][:, :, None, None], (2, N_HEAD, 8, tl))
        gn = row(ret_gn[l])
        o_ret, st_rt = _retention_ctx(z, dec_ret[..., :tc], gn, nc + nl, bc, tc)
        o_ret = _retention_lat(z, o_ret, dec_ret, gn, rope_cos, rope_sin, state_ret, l, n_ctx_tiles, bl, tl)

        x = _mix_out(x, mod, o_rwkv, g_rwkv, o_na, o_ret, z, w_o_rwkv[l].astype(BF16), w_o_na[l].astype(BF16),
                     w_o_ret[l].astype(BF16), w_out[l].astype(BF16), row(norm_mix_post[l]), n_ctx_tiles)
        x = _ffn(x, mod, row(norm_ffn_pre[l]), ffn_up[l].astype(BF16), ffn_conv[l], ffn_down[l].astype(BF16),
                 row(norm_ffn_post[l]), n_ctx_tiles, tc, tl)

        new_k.append(k_ctx)
        new_v.append(v_ctx)
        new_rw.append(st_rw)
        new_rt.append(st_rt)

    return (x[:nc].reshape(bc, tc, D_MODEL), x[nc:].reshape(bl, tl, D_MODEL), jnp.stack(new_k, axis=1),
            jnp.stack(new_v, axis=1), jnp.stack(new_rw, axis=1), jnp.stack(new_rt, axis=1))
```

```python
import functools
import math

import jax
import jax.numpy as jnp
from jax import lax
from jax.experimental import pallas as pl
from jax.experimental.pallas import tpu as pltpu

F32 = jnp.float32
BF16 = jnp.bfloat16

D_MODEL = 1024
DEPTH = 2
N_HEAD = 8
HEAD_DIM = 64
D_BRANCH = N_HEAD * HEAD_DIM
GRID_W = 64
NA_KH = 8
NA_KW = 16
W_LORA = 64
A_LORA = 64
G_LORA = 128
D_FF = 2816
ROPE_BASE = 10000.0
NEG_BIG = -1e9
RWKV_LN_EPS = 64e-5
RET_GN_EPS = 1e-5
RMS_EPS = 1e-6

RWKV_COLS = 3 * D_BRANCH + W_LORA + A_LORA + G_LORA
RWKV_PAD = 2048
OFF_GATE = 0
OFF_RWKV = 3 * D_MODEL
OFF_NA = OFF_RWKV + RWKV_PAD
OFF_RET = OFF_NA + 3 * D_BRANCH
Z_COLS = OFF_RET + 4 * D_BRANCH

LANES = 128
TOKEN_TILE = 1024
MOD_ROWS = 8
VMEM_LIMIT = 56 * 1024 * 1024


def _cparams(sem):
    return pltpu.CompilerParams(dimension_semantics=sem, vmem_limit_bytes=VMEM_LIMIT)


def _rms(x, g):
    return x * lax.rsqrt(jnp.mean(x * x, axis=-1, keepdims=True) + RMS_EPS) * g


def _softplus(x):
    return jnp.maximum(x, 0.0) + jnp.log1p(jnp.exp(-jnp.abs(x)))


def _dot(a, b):
    return jnp.dot(a, b, preferred_element_type=F32)


def _dot_nt(a, b):
    return lax.dot_general(a, b, (((1,), (1,)), ((), ())), preferred_element_type=F32)


def _dot_tn(a, b):
    return lax.dot_general(a, b, (((0,), (0,)), ((), ())), preferred_element_type=F32)


def _mod_index(i, n_ctx_tiles, tiles_per_lat):
    return jnp.where(i < n_ctx_tiles, 0, 1 + jnp.maximum(i - n_ctx_tiles, 0) // tiles_per_lat)


def _seq_edges(i, n_ctx_tiles, t_ctx, t_lat, tm):
    seqlen = jnp.where(i < n_ctx_tiles, t_ctx, t_lat)
    pos = lax.broadcasted_iota(jnp.int32, (tm, 1), 0) & (seqlen - 1)
    return pos == 0, pos == seqlen - 1


def _dwconv3(u, w, first, last):
    tm = u.shape[0]
    prev = jnp.where(first, 0.0, pltpu.roll(u, 1, 0))
    nxt = jnp.where(last, 0.0, pltpu.roll(u, tm - 1, 0))
    return prev * w[0:1] + u * w[1:2] + nxt * w[2:3]


def _ada_kernel(c_ref, w_ref, b_ref, o_ref):
    c = c_ref[...]
    s = (c * jax.nn.sigmoid(c)).astype(BF16)
    o_ref[0] = _dot(s, w_ref[0].astype(BF16)) + b_ref[0]


def _ada_modulation(cvec, ada_w, ada_b):
    tn = 512
    n_out = 6 * D_MODEL
    return pl.pallas_call(
        _ada_kernel,
        grid=(DEPTH, n_out // tn),
        in_specs=[
            pl.BlockSpec((MOD_ROWS, D_MODEL), lambda l, j: (0, 0)),
            pl.BlockSpec((1, D_MODEL, tn), lambda l, j: (l, 0, j)),
            pl.BlockSpec((1, 1, tn), lambda l, j: (l, 0, j)),
        ],
        out_specs=pl.BlockSpec((1, MOD_ROWS, tn), lambda l, j: (l, 0, j)),
        out_shape=jax.ShapeDtypeStruct((DEPTH, MOD_ROWS, n_out), F32),
        compiler_params=_cparams(("parallel", "parallel")),
    )(cvec, ada_w, ada_b.reshape(DEPTH, 1, n_out))


def _in_proj_kernel(x_ref, m_ref, g_ref, w_ref, o_ref, h_scr):
    @pl.when(pl.program_id(1) == 0)
    def _():
        y = _rms(x_ref[...], g_ref[...])
        h = y * (1.0 + m_ref[0, :, D_MODEL:2 * D_MODEL]) + m_ref[0, :, 0:D_MODEL]
        h_scr[...] = h.astype(BF16)

    o_ref[...] = _dot(h_scr[...], w_ref[...]).astype(o_ref.dtype)


def _in_proj(x, mod, gain, w_bf16, n_ctx_tiles):
    n = x.shape[0]
    tm, tn = TOKEN_TILE, 512
    midx = functools.partial(_mod_index, n_ctx_tiles=n_ctx_tiles, tiles_per_lat=1)
    return pl.pallas_call(
        _in_proj_kernel,
        grid=(n // tm, Z_COLS // tn),
        in_specs=[
            pl.BlockSpec((tm, D_MODEL), lambda i, j: (i, 0)),
            pl.BlockSpec((1, 1, 6 * D_MODEL), lambda i, j: (midx(i), 0, 0)),
            pl.BlockSpec((1, D_MODEL), lambda i, j: (0, 0)),
            pl.BlockSpec((D_MODEL, tn), lambda i, j: (0, j)),
        ],
        out_specs=pl.BlockSpec((tm, tn), lambda i, j: (i, j)),
        out_shape=jax.ShapeDtypeStruct((n, Z_COLS), BF16),
        scratch_shapes=[pltpu.VMEM((tm, D_MODEL), BF16)],
        compiler_params=_cparams(("parallel", "arbitrary")),
    )(x, mod, gain, w_bf16)


def _rwkv_conv_kernel(z_ref, w_ref, o_ref, *, n_ctx_tiles, t_ctx, t_lat):
    first, last = _seq_edges(pl.program_id(0), n_ctx_tiles, t_ctx, t_lat, z_ref.shape[0])
    o_ref[...] = _dwconv3(z_ref[...].astype(F32), w_ref[...], first, last)


def _rwkv_conv(z, conv_w, n_ctx_tiles, t_ctx, t_lat):
    n = z.shape[0]
    tm, tn = TOKEN_TILE, D_BRANCH
    cb = OFF_RWKV // tn
    return pl.pallas_call(
        functools.partial(_rwkv_conv_kernel, n_ctx_tiles=n_ctx_tiles, t_ctx=t_ctx, t_lat=t_lat),
        grid=(n // tm, 3),
        in_specs=[
            pl.BlockSpec((tm, tn), lambda i, j: (i, cb + j)),
            pl.BlockSpec((3, tn), lambda i, j: (0, j)),
        ],
        out_specs=pl.BlockSpec((tm, tn), lambda i, j: (i, j)),
        out_shape=jax.ShapeDtypeStruct((n, 3 * D_BRANCH), F32),
        compiler_params=_cparams(("parallel", "parallel")),
    )(z, conv_w)


def _rwkv_lora_kernel(z_ref, cw_ref, wup_ref, aup_ref, gup_ref, w0_ref, a0_ref,
                      dec_ref, al_ref, g_ref, *, n_ctx_tiles, t_ctx, t_lat):
    first, last = _seq_edges(pl.program_id(0), n_ctx_tiles, t_ctx, t_lat, z_ref.shape[0])
    u = _dwconv3(z_ref[...].astype(F32), cw_ref[...], first, last)
    wa = u[:, 0:LANES]
    tw = jnp.tanh(wa).astype(BF16)
    ab = wa.astype(BF16)
    sg = jax.nn.sigmoid(u[:, LANES:2 * LANES]).astype(BF16)
    for d in range(2):
        x = w0_ref[d:d + 1, :] + _dot(tw, wup_ref[d])
        w_log = -_softplus(-x) - 0.5
        dec_ref[d] = jnp.exp(-jnp.exp(w_log))
        al_ref[d] = jax.nn.sigmoid(a0_ref[d:d + 1, :] + _dot(ab, aup_ref[d]))
    g_ref[...] = _dot(sg, gup_ref[...])


def _rwkv_lora(z, conv_w_lora, wup_pad, aup_pad, gup, w0, a0, n_ctx_tiles, t_ctx, t_lat):
    n = z.shape[0]
    tm = TOKEN_TILE
    wl = 2 * LANES
    cb = (OFF_RWKV + 3 * D_BRANCH) // wl
    out2 = jax.ShapeDtypeStruct((2, n, D_BRANCH), F32)
    ospec2 = pl.BlockSpec((2, tm, D_BRANCH), lambda i: (0, i, 0))
    full2 = lambda shape: pl.BlockSpec(shape, lambda i: (0, 0))
    full3 = lambda shape: pl.BlockSpec(shape, lambda i: (0, 0, 0))
    return pl.pallas_call(
        functools.partial(_rwkv_lora_kernel, n_ctx_tiles=n_ctx_tiles, t_ctx=t_ctx, t_lat=t_lat),
        grid=(n // tm,),
        in_specs=[
            pl.BlockSpec((tm, wl), lambda i: (i, cb)),
            full2((3, wl)),
            full3((2, LANES, D_BRANCH)),
            full3((2, LANES, D_BRANCH)),
            full2((G_LORA, D_BRANCH)),
            full2((2, D_BRANCH)),
            full2((2, D_BRANCH)),
        ],
        out_specs=[ospec2, ospec2, pl.BlockSpec((tm, D_BRANCH), lambda i: (i, 0))],
        out_shape=[out2, out2, jax.ShapeDtypeStruct((n, D_BRANCH), F32)],
        compiler_params=_cparams(("parallel",)),
    )(z, conv_w_lora, wup_pad, aup_pad, gup, w0, a0)


HALF = LANES // 2


def _fold_sum(x, lane, fold):
    s = HALF // fold
    while s < HALF:
        x = x + jnp.where((lane & s) != 0, pltpu.roll(x, s, 1), pltpu.roll(x, LANES - s, 1))
        s *= 2
    return x


SCAN_TB = 8


def _wkv_kernel(rf_ref, rb_ref, kf_ref, kb_ref, wf_ref, wb_ref, af_ref, ab_ref, vf_ref, vb_ref, p_ref, s0_ref,
                yf_ref, yb_ref, st_ref, s_scr, op_scr, sa_scr, *, fold, unroll):
    tb = SCAN_TB
    t = pl.program_id(1)

    @pl.when(t == 0)
    def _():
        s_scr[...] = s0_ref[...]

    lane = lax.broadcasted_iota(jnp.int32, (1, LANES), 1)
    is_fwd = lane < HALF
    k_k = p_ref[0]
    k_a = p_ref[1]

    for j in range(tb):
        jb = tb - 1 - j
        k = jnp.where(is_fwd, kf_ref[j], kb_ref[jb])
        a = jnp.where(is_fwd, af_ref[j], ab_ref[jb])
        kk = k * k_k
        kk = kk / jnp.maximum(jnp.sqrt(_fold_sum(jnp.sum(kk * kk, axis=0, keepdims=True), lane, fold)), 1e-12)
        op_scr[j, 0] = -kk
        op_scr[j, 1] = jnp.where(is_fwd, wf_ref[j], wb_ref[jb])
        op_scr[j, 2] = kk * a
        op_scr[j, 3] = k * (1.0 + (a - 1.0) * k_a)
        op_scr[j, 4] = jnp.where(is_fwd, rf_ref[j], rb_ref[jb])

    def step(j, carry):
        jb = tb - 1 - j
        if fold > 1:
            def sa_partial(vi, c):
                sa_scr[pl.ds(vi, 1), :] = jnp.sum(s_scr[vi] * op_scr[j, 0], axis=0, keepdims=True)
                return c

            lax.fori_loop(0, HEAD_DIM, sa_partial, 0, unroll=unroll)
            sa_scr[...] = _fold_sum(sa_scr[...], lane, fold)

        def vstep(vi, c):
            sv = s_scr[vi]
            if fold > 1:
                sa = sa_scr[pl.ds(vi, 1), :]
            else:
                sa = jnp.sum(sv * op_scr[j, 0], axis=0, keepdims=True)
            vrow = jnp.where(is_fwd, vf_ref[j, pl.ds(vi, 1), :], vb_ref[jb, pl.ds(vi, 1), :])
            sn = sv * op_scr[j, 1] + sa * op_scr[j, 2] + vrow * op_scr[j, 3]
            s_scr[vi] = sn
            yrow = jnp.sum(sn * op_scr[j, 4], axis=0, keepdims=True)
            yf_ref[j, pl.ds(vi, 1), :] = yrow
            yb_ref[jb, pl.ds(vi, 1), :] = yrow
            return c

        lax.fori_loop(0, HEAD_DIM, vstep, 0, unroll=unroll)
        return carry

    lax.fori_loop(0, tb, step, 0)

    @pl.when(t == pl.num_programs(1) - 1)
    def _():
        st_ref[...] = s_scr[...]


def _wkv_scan(r, k, w, a, v, params, s0, fold):
    t_len, ks, n_lane = r.shape
    tb = SCAN_TB
    nt = t_len // tb
    fwd = lambda rows: pl.BlockSpec((tb, rows, LANES), lambda c, t: (t, 0, c))
    bwd = lambda rows: pl.BlockSpec((tb, rows, LANES), lambda c, t: (nt - 1 - t, 0, c))
    state = pl.BlockSpec((HEAD_DIM, ks, LANES), lambda c, t: (0, 0, c))
    y_shape = jax.ShapeDtypeStruct((t_len, HEAD_DIM, n_lane), F32)
    return pl.pallas_call(
        functools.partial(_wkv_kernel, fold=fold, unroll=8),
        grid=(n_lane // LANES, nt),
        in_specs=[fwd(ks), bwd(ks)] * 4 + [fwd(HEAD_DIM), bwd(HEAD_DIM),
                                          pl.BlockSpec((2, ks, LANES), lambda c, t: (0, 0, c)), state],
        out_specs=[fwd(HEAD_DIM), bwd(HEAD_DIM), state],
        out_shape=[y_shape, y_shape, jax.ShapeDtypeStruct((HEAD_DIM, ks, n_lane), F32)],
        scratch_shapes=[pltpu.VMEM((HEAD_DIM, ks, LANES), F32), pltpu.VMEM((tb, 5, ks, LANES), F32),
                        pltpu.VMEM((HEAD_DIM, LANES), F32)],
        compiler_params=_cparams(("parallel", "arbitrary")),
    )(r, r, k, k, w, w, a, a, v, v, params, s0)


def _rwkv_post_kernel(yf_ref, yb_ref, r_ref, k_ref, v_ref, pk_ref, pv_ref, o_ref, *, fold):
    lane = lax.broadcasted_iota(jnp.int32, (1, LANES), 1)
    ln_g = pv_ref[0]
    ln_b = pv_ref[1]
    r_k = pk_ref[...]
    for j in range(SCAN_TB):
        y = _fold_sum(yf_ref[j] + pltpu.roll(yb_ref[j], HALF, 1), lane, fold)
        mu = jnp.mean(y, axis=0, keepdims=True)
        dlt = y - mu
        var = jnp.mean(dlt * dlt, axis=0, keepdims=True)
        o = dlt * lax.rsqrt(var + RWKV_LN_EPS) * ln_g + ln_b
        rk = _fold_sum(jnp.sum(r_ref[j] * k_ref[j] * r_k, axis=0, keepdims=True), lane, fold)
        o_ref[j] = o + rk * v_ref[j]


def _rwkv_post(yf, yb, r, k, v, pk, pv, fold):
    t_len, _, n_lane = yf.shape
    ks = r.shape[1]
    spec = lambda rows: pl.BlockSpec((SCAN_TB, rows, LANES), lambda c, i: (i, 0, c))
    return pl.pallas_call(
        functools.partial(_rwkv_post_kernel, fold=fold),
        grid=(n_lane // LANES, t_len // SCAN_TB),
        in_specs=[
            spec(HEAD_DIM), spec(HEAD_DIM), spec(ks), spec(ks), spec(HEAD_DIM),
            pl.BlockSpec((ks, LANES), lambda c, i: (0, c)),
            pl.BlockSpec((2, HEAD_DIM, LANES), lambda c, i: (0, 0, c)),
        ],
        out_specs=spec(HEAD_DIM),
        out_shape=jax.ShapeDtypeStruct((t_len, HEAD_DIM, n_lane), F32),
        compiler_params=_cparams(("parallel", "parallel")),
    )(yf, yb, r, k, v, pk, pv)


def _head(h):
    return slice(h * HEAD_DIM, (h + 1) * HEAD_DIM)


def _ctx_attn_kernel(q_ref, k_ref, v_ref, o_ref, ko_ref, vo_ref):
    scale = HEAD_DIM ** -0.5
    for h in range(N_HEAD):
        k = k_ref[:, _head(h)]
        v = v_ref[:, _head(h)]
        ko_ref[0, h] = k.astype(F32)
        vo_ref[0, h] = v.astype(F32)
        s = _dot_nt(q_ref[:, _head(h)], k) * scale
        p = jnp.exp(s - jnp.max(s, axis=-1, keepdims=True))
        p = p / jnp.sum(p, axis=-1, keepdims=True)
        o_ref[:, _head(h)] = _dot(p.astype(BF16), v)


def _ctx_attention(z, n_rows, b, t):
    cb = OFF_NA // D_BRANCH
    zspec = lambda c: pl.BlockSpec((t, D_BRANCH), lambda i: (i, cb + c))
    kv_shape = jax.ShapeDtypeStruct((b, N_HEAD, t, HEAD_DIM), F32)
    kv_spec = pl.BlockSpec((1, N_HEAD, t, HEAD_DIM), lambda i: (i, 0, 0, 0))
    return pl.pallas_call(
        _ctx_attn_kernel,
        grid=(b,),
        in_specs=[zspec(0), zspec(1), zspec(2)],
        out_specs=[pl.BlockSpec((t, D_BRANCH), lambda i: (i, 0)), kv_spec, kv_spec],
        out_shape=[jax.ShapeDtypeStruct((n_rows, D_BRANCH), F32), kv_shape, kv_shape],
        compiler_params=_cparams(("parallel",)),
    )(z, z, z)


def _na_kernel(q_ref, k_ref, v_ref, kc_ref, vc_ref, tab_ref, prev_ref, o_ref, *, rows):
    del prev_ref
    scale = HEAD_DIM ** -0.5
    win = NA_KH * GRID_W
    qcol = lax.broadcasted_iota(jnp.int32, (GRID_W, win), 0)
    kcol = lax.broadcasted_iota(jnp.int32, (GRID_W, win), 1) & (GRID_W - 1)
    c_start = jnp.clip(qcol - NA_KW // 2, 0, GRID_W - NA_KW)
    col_valid = (kcol >= c_start) & (kcol < c_start + NA_KW)
    for h in range(N_HEAD):
        kc = kc_ref[0, 0, h].astype(BF16)
        vc = vc_ref[0, 0, h].astype(BF16)

        def row_block(r, carry):
            rs = jnp.clip(r - NA_KH // 2, 0, rows - NA_KH)
            q_rows = pl.ds(pl.multiple_of(r * GRID_W, GRID_W), GRID_W)
            w_rows = pl.ds(pl.multiple_of(rs * GRID_W, GRID_W), win)
            q = q_ref[q_rows, _head(h)].astype(BF16)
            kw = k_ref[w_rows, _head(h)].astype(BF16)
            vw = v_ref[w_rows, _head(h)].astype(BF16)
            first = rs - r + NA_KH - 1
            bias = jnp.concatenate([tab_ref[h, first + 2 * i] for i in range(NA_KH // 2)], axis=1)
            s_loc = jnp.where(col_valid, _dot_nt(q, kw) * scale + bias, NEG_BIG)
            s_ctx = _dot_nt(q, kc) * scale
            m = jnp.maximum(jnp.max(s_loc, axis=-1, keepdims=True), jnp.max(s_ctx, axis=-1, keepdims=True))
            p_loc = jnp.exp(s_loc - m)
            p_ctx = jnp.exp(s_ctx - m)
            den = jnp.sum(p_loc, axis=-1, keepdims=True) + jnp.sum(p_ctx, axis=-1, keepdims=True)
            o_ref[q_rows, _head(h)] = _dot((p_loc / den).astype(BF16), vw) + _dot((p_ctx / den).astype(BF16), vc)
            return carry

        lax.fori_loop(0, rows, row_block, 0, unroll=2)


def _na_attention(z, o_prev, cache_k, cache_v, bias_tab, layer, row_tile0, b, t):
    past = cache_k.shape[3]
    rows = t // GRID_W
    assert rows >= NA_KH
    cb = OFF_NA // D_BRANCH
    zspec = lambda c: pl.BlockSpec((t, D_BRANCH), lambda i: (row_tile0 + i, cb + c))
    cspec = pl.BlockSpec((1, 1, N_HEAD, past, HEAD_DIM), lambda i: (i, layer, 0, 0, 0))
    return pl.pallas_call(
        functools.partial(_na_kernel, rows=rows),
        grid=(b,),
        in_specs=[zspec(0), zspec(1), zspec(2), cspec, cspec,
                  pl.BlockSpec(bias_tab.shape, lambda i: (0, 0, 0, 0)),
                  pl.BlockSpec(memory_space=pl.ANY)],
        out_specs=pl.BlockSpec((t, D_BRANCH), lambda i: (row_tile0 + i, 0)),
        out_shape=jax.ShapeDtypeStruct(o_prev.shape, F32),
        input_output_aliases={6: 0},
        compiler_params=_cparams(("parallel",)),
    )(z, z, z, cache_k, cache_v, bias_tab, o_prev)


def _na_bias_table(rpb):
    h, ndr, _ = rpb.shape
    edge = GRID_W - NA_KW
    ext = jnp.concatenate([jnp.broadcast_to(rpb[..., :1], (h, ndr, edge)), rpb,
                           jnp.broadcast_to(rpb[..., -1:], (h, ndr, edge + 1))], axis=-1)
    skew = jnp.tile(ext, (1, 1, GRID_W))[..., :GRID_W * (2 * GRID_W - 1)].reshape(h, ndr, GRID_W, 2 * GRID_W - 1)
    toep = skew[..., GRID_W - 1:]
    return jnp.concatenate([toep[:, :-1], toep[:, 1:]], axis=-1)


def _rope_tables(t_len):
    half = HEAD_DIM // 2
    nf = half // 2
    inv = ROPE_BASE ** (-jnp.arange(nf, dtype=F32) / nf)
    t = jnp.arange(t_len)

    def tab(pos):
        ang = pos.astype(F32)[:, None] * inv[None, :]
        cos, sin = jnp.cos(ang), jnp.sin(ang)
        return jnp.concatenate([cos, cos], -1), jnp.concatenate([-sin, sin], -1)

    c_row, s_row = tab(t // GRID_W)
    c_col, s_col = tab(t % GRID_W)
    cos = jnp.tile(jnp.concatenate([c_row, c_col], -1), (1, N_HEAD))
    sin = jnp.tile(jnp.concatenate([s_row, s_col], -1), (1, N_HEAD))
    return cos, sin


def _log_sigmoid(x):
    return -_softplus(-x)


def _ret_heads(q_ref, k_ref, v_ref, g_ref, dec_ref, gn_ref, o_ref, *, t_len, qb, rope_refs=None, s0_ref=None,
               st_ref=None):
    lo = (lax.broadcasted_iota(jnp.int32, (1, LANES), 1) & 31) < 16
    for hp in range(N_HEAD // 2):
        pair = slice(hp * LANES, (hp + 1) * LANES)
        q2 = q_ref[:, pair].astype(F32)
        k2 = k_ref[:, pair].astype(F32)
        if rope_refs is not None:
            cos = rope_refs[0][:, pair]
            sin = rope_refs[1][:, pair]
            rot = lambda x: x * cos + jnp.where(lo, pltpu.roll(x, LANES - 16, 1), pltpu.roll(x, 16, 1)) * sin
            q2, k2 = rot(q2), rot(k2)
        k2 = k2 * (HEAD_DIM ** -0.5)
        for hh in range(2):
            h = 2 * hp + hh
            half = slice(hh * HEAD_DIM, (hh + 1) * HEAD_DIM)
            qh = q2[:, half].astype(BF16)
            k = k2[:, half]
            kb = k.astype(BF16)
            vb = v_ref[:, _head(h)].astype(BF16)
            lgf = _log_sigmoid(dec_ref[0, h])[0:1, :]
            lgb = _log_sigmoid(dec_ref[1, h])[0:1, :]
            lgf_h = lgf[:, 0:HEAD_DIM]
            lgb_h = lgb[:, 0:HEAD_DIM]
            for qi in range(t_len // qb):
                rows = slice(qi * qb, (qi + 1) * qb)
                q = qh[rows]
                s = _dot_nt(q, kb)
                diff = (lax.broadcasted_iota(jnp.int32, (qb, t_len), 0) + qi * qb
                        - lax.broadcasted_iota(jnp.int32, (qb, t_len), 1)).astype(F32)
                dmat = (jnp.where(diff >= 0, jnp.exp(lgf * jnp.maximum(diff, 0.0)), 0.0)
                        + jnp.where(diff <= 0, jnp.exp(lgb * jnp.maximum(-diff, 0.0)), 0.0))
                y = _dot((s * dmat).astype(BF16), vb)
                if s0_ref is not None:
                    pos = (lax.broadcasted_iota(jnp.int32, (qb, HEAD_DIM), 0) + qi * qb).astype(F32)
                    y = y + _dot(q, s0_ref[0, 0, h].astype(BF16)) * jnp.exp(lgf_h * (pos + 1.0))
                    y = y + _dot(q, s0_ref[0, 1, h].astype(BF16)) * jnp.exp(lgb_h * (t_len - pos))
                mu = jnp.mean(y, axis=-1, keepdims=True)
                dlt = y - mu
                var = jnp.mean(dlt * dlt, axis=-1, keepdims=True)
                yn = dlt * lax.rsqrt(var + RET_GN_EPS)
                g = g_ref[rows, _head(h)].astype(F32)
                o_ref[rows, _head(h)] = (yn * gn_ref[:, _head(h)]) * (g * jax.nn.sigmoid(g))
            if st_ref is not None:
                pos = lax.broadcasted_iota(jnp.int32, (t_len, HEAD_DIM), 0).astype(F32)
                kzf = (k * jnp.exp(lgf_h * (t_len - 1.0 - pos))).astype(BF16)
                kzb = (k * jnp.exp(lgb_h * pos)).astype(BF16)
                sf = _dot_tn(kzf, vb)
                sb = _dot_tn(kzb, vb)
                if s0_ref is not None:
                    sf = sf + s0_ref[0, 0, h] * jnp.exp(lgf_h * float(t_len))
                    sb = sb + s0_ref[0, 1, h] * jnp.exp(lgb_h * float(t_len))
                st_ref[0, 0, h] = sf
                st_ref[0, 1, h] = sb


def _ret_ctx_kernel(q_ref, k_ref, v_ref, g_ref, dec_ref, gn_ref, o_ref, st_ref, *, t_len, qb):
    _ret_heads(q_ref, k_ref, v_ref, g_ref, dec_ref, gn_ref, o_ref, t_len=t_len, qb=qb, st_ref=st_ref)


def _ret_lat_kernel(q_ref, k_ref, v_ref, g_ref, dec_ref, gn_ref, cos_ref, sin_ref, s0_ref, prev_ref, o_ref,
                    *, t_len, qb):
    del prev_ref
    _ret_heads(q_ref, k_ref, v_ref, g_ref, dec_ref, gn_ref, o_ref, t_len=t_len, qb=qb,
               rope_refs=(cos_ref, sin_ref), s0_ref=s0_ref.at[0])


def _ret_specs(t, row_tile0):
    cb = OFF_RET // D_BRANCH
    zspec = lambda c: pl.BlockSpec((t, D_BRANCH), lambda i: (row_tile0 + i, cb + c))
    return [zspec(0), zspec(1), zspec(2), zspec(3),
            pl.BlockSpec((2, N_HEAD, 8, t), lambda i: (0, 0, 0, 0)),
            pl.BlockSpec((1, D_BRANCH), lambda i: (0, 0))]


def _retention_ctx(z, dec, gn, n_rows, b, t):
    st_shape = (b, 2, N_HEAD, HEAD_DIM, HEAD_DIM)
    return pl.pallas_call(
        functools.partial(_ret_ctx_kernel, t_len=t, qb=min(t, 256)),
        grid=(b,),
        in_specs=_ret_specs(t, 0),
        out_specs=[pl.BlockSpec((t, D_BRANCH), lambda i: (i, 0)),
                   pl.BlockSpec((1,) + st_shape[1:], lambda i: (i, 0, 0, 0, 0))],
        out_shape=[jax.ShapeDtypeStruct((n_rows, D_BRANCH), F32), jax.ShapeDtypeStruct(st_shape, F32)],
        compiler_params=_cparams(("parallel",)),
    )(z, z, z, z, dec, gn)


def _retention_lat(z, o_prev, dec, gn, cos, sin, s0, layer, row_tile0, b, t):
    tab = pl.BlockSpec((t, D_BRANCH), lambda i: (0, 0))
    return pl.pallas_call(
        functools.partial(_ret_lat_kernel, t_len=t, qb=min(t, 256)),
        grid=(b,),
        in_specs=_ret_specs(t, row_tile0) + [
            tab, tab,
            pl.BlockSpec((1, 1, 2, N_HEAD, HEAD_DIM, HEAD_DIM), lambda i: (i, layer, 0, 0, 0, 0)),
            pl.BlockSpec(memory_space=pl.ANY)],
        out_specs=pl.BlockSpec((t, D_BRANCH), lambda i: (row_tile0 + i, 0)),
        out_shape=jax.ShapeDtypeStruct(o_prev.shape, F32),
        input_output_aliases={9: 0},
        compiler_params=_cparams(("parallel",)),
    )(z, z, z, z, dec, gn, cos, sin, s0, o_prev)


def _mix_out_kernel(x_ref, m_ref, oa_ref, ga_ref, ob_ref, oc_ref, g0_ref, g1_ref, g2_ref,
                    wa_ref, wb_ref, wc_ref, wo_ref, gain_ref, o_ref):
    out_a = _dot((oa_ref[...] * ga_ref[...]).astype(BF16), wa_ref[...])
    out_b = _dot(ob_ref[...].astype(BF16), wb_ref[...])
    out_c = _dot(oc_ref[...].astype(BF16), wc_ref[...])
    sig = lambda ref: jax.nn.sigmoid(ref[...].astype(F32))
    merged = sig(g0_ref) * out_a + sig(g1_ref) * out_b + sig(g2_ref) * out_c
    y = _dot(merged.astype(BF16), wo_ref[...])
    gate = m_ref[0, :, 2 * D_MODEL:3 * D_MODEL]
    o_ref[...] = x_ref[...] + gate * _rms(y, gain_ref[...])


def _mix_out(x, mod, o_rwkv, g_rwkv, o_na, o_ret, z, wa, wb, wc, wo, gain, n_ctx_tiles):
    n = x.shape[0]
    tm = 256
    per = TOKEN_TILE // tm
    midx = functools.partial(_mod_index, n_ctx_tiles=n_ctx_tiles * per, tiles_per_lat=per)
    row = lambda w: pl.BlockSpec((tm, w), lambda i: (i, 0))
    gate_spec = lambda g: pl.BlockSpec((tm, D_MODEL), lambda i: (i, OFF_GATE // D_MODEL + g))
    wspec = lambda a, b: pl.BlockSpec((a, b), lambda i: (0, 0))
    return pl.pallas_call(
        _mix_out_kernel,
        grid=(n // tm,),
        in_specs=[
            row(D_MODEL),
            pl.BlockSpec((1, 1, 6 * D_MODEL), lambda i: (midx(i), 0, 0)),
            row(D_BRANCH), row(D_BRANCH), row(D_BRANCH), row(D_BRANCH),
            gate_spec(0), gate_spec(1), gate_spec(2),
            wspec(D_BRANCH, D_MODEL), wspec(D_BRANCH, D_MODEL), wspec(D_BRANCH, D_MODEL),
            wspec(D_MODEL, D_MODEL), wspec(1, D_MODEL),
        ],
        out_specs=row(D_MODEL),
        out_shape=jax.ShapeDtypeStruct((n, D_MODEL), F32),
        compiler_params=_cparams(("parallel",)),
    )(x, mod, o_rwkv, g_rwkv, o_na, o_ret, z, z, z, wa, wb, wc, wo, gain)


def _gelu_tanh(x):
    return x * (0.5 * (1.0 + jnp.tanh(math.sqrt(2.0 / math.pi) * (x + 0.044715 * (x * x * x)))))


def _ffn_kernel(x_ref, m_ref, gpre_ref, wv_ref, wg_ref, cv_ref, cg_ref, wd_ref, gpost_ref, o_ref,
                h_scr, acc_scr, *, n_ctx_tiles, t_ctx, t_lat):
    j = pl.program_id(1)

    @pl.when(j == 0)
    def _():
        y = _rms(x_ref[...], gpre_ref[...])
        h = y * (1.0 + m_ref[0, :, 4 * D_MODEL:5 * D_MODEL]) + m_ref[0, :, 3 * D_MODEL:4 * D_MODEL]
        h_scr[...] = h.astype(BF16)
        acc_scr[...] = jnp.zeros_like(acc_scr)

    first, last = _seq_edges(pl.program_id(0), n_ctx_tiles, t_ctx, t_lat, x_ref.shape[0])
    h = h_scr[...]
    val = _dwconv3(_dot(h, wv_ref[...]), cv_ref[...], first, last)
    gate = _dwconv3(_dot(h, wg_ref[...]), cg_ref[...], first, last)
    acc_scr[...] += _dot((_gelu_tanh(gate) * val).astype(BF16), wd_ref[...])

    @pl.when(j == pl.num_programs(1) - 1)
    def _():
        gate2 = m_ref[0, :, 5 * D_MODEL:6 * D_MODEL]
        o_ref[...] = x_ref[...] + gate2 * _rms(acc_scr[...], gpost_ref[...])


def _ffn(x, mod, gpre, w_up, conv_w, w_down, gpost, n_ctx_tiles, t_ctx, t_lat):
    n = x.shape[0]
    tm, tf = TOKEN_TILE, 256
    nf = D_FF // tf
    midx = functools.partial(_mod_index, n_ctx_tiles=n_ctx_tiles, tiles_per_lat=1)
    return pl.pallas_call(
        functools.partial(_ffn_kernel, n_ctx_tiles=n_ctx_tiles, t_ctx=t_ctx, t_lat=t_lat),
        grid=(n // tm, nf),
        in_specs=[
            pl.BlockSpec((tm, D_MODEL), lambda i, j: (i, 0)),
            pl.BlockSpec((1, 1, 6 * D_MODEL), lambda i, j: (midx(i), 0, 0)),
            pl.BlockSpec((1, D_MODEL), lambda i, j: (0, 0)),
            pl.BlockSpec((D_MODEL, tf), lambda i, j: (0, j)),
            pl.BlockSpec((D_MODEL, tf), lambda i, j: (0, nf + j)),
            pl.BlockSpec((3, tf), lambda i, j: (0, j)),
            pl.BlockSpec((3, tf), lambda i, j: (0, nf + j)),
            pl.BlockSpec((tf, D_MODEL), lambda i, j: (j, 0)),
            pl.BlockSpec((1, D_MODEL), lambda i, j: (0, 0)),
        ],
        out_specs=pl.BlockSpec((tm, D_MODEL), lambda i, j: (i, 0)),
        out_shape=jax.ShapeDtypeStruct((n, D_MODEL), F32),
        scratch_shapes=[pltpu.VMEM((tm, D_MODEL), BF16), pltpu.VMEM((tm, D_MODEL), F32)],
        compiler_params=_cparams(("parallel", "arbitrary")),
    )(x, mod, gpre, w_up, w_up, conv_w, conv_w, w_down, gpost)


class _ScanGeom:
    def __init__(self, b):
        self.b = b
        self.pairs = b * N_HEAD
        self.fold = max(1, HALF // self.pairs)
        self.ks = HEAD_DIM // self.fold
        self.pt = HALF // self.fold
        assert self.pairs % self.pt == 0 and self.ks % 8 == 0
        self.tiles = self.pairs // self.pt

    def lanes(self, x, lead):
        return jnp.broadcast_to(x, lead + (self.tiles, 2, self.fold, self.pt)).reshape(lead + (self.tiles * LANES,))

    def key_rows(self, x, t):
        x = x.reshape(self.b, t, N_HEAD, self.fold, self.ks).transpose(1, 4, 3, 0, 2)
        return x.reshape(t, self.ks, self.fold, self.tiles, self.pt).transpose(0, 1, 3, 2, 4)

    def key_operand(self, xs, t):
        if xs.ndim == 2:
            x = self.key_rows(xs, t)[:, :, :, None]
        else:
            x = xs.reshape(2, self.b, t, N_HEAD, self.fold, self.ks).transpose(2, 5, 0, 4, 1, 3)
            x = x.reshape(t, self.ks, 2, self.fold, self.tiles, self.pt).transpose(0, 1, 4, 2, 3, 5)
        return self.lanes(x, (t, self.ks))

    def value_operand(self, x, t):
        x = x.reshape(self.b, t, N_HEAD, HEAD_DIM).transpose(1, 3, 0, 2)
        return self.lanes(x.reshape(t, HEAD_DIM, self.tiles, 1, 1, self.pt), (t, HEAD_DIM))

    def key_param(self, p):
        x = jnp.broadcast_to(p.reshape(1, N_HEAD, self.fold, self.ks), (self.b, N_HEAD, self.fold, self.ks))
        x = x.reshape(self.tiles, self.pt, self.fold, self.ks).transpose(3, 0, 2, 1)
        return self.lanes(x[:, :, None], (self.ks,))

    def value_param(self, p):
        x = jnp.broadcast_to(p.reshape(1, N_HEAD, HEAD_DIM), (self.b, N_HEAD, HEAD_DIM))
        x = x.reshape(self.tiles, self.pt, HEAD_DIM).transpose(2, 0, 1)
        return self.lanes(x[:, :, None, None], (HEAD_DIM,))

    def state_in(self, s0):
        x = s0.reshape(self.b, 2, N_HEAD, HEAD_DIM, self.fold, self.ks).transpose(3, 5, 1, 4, 0, 2)
        x = x.reshape(HEAD_DIM, self.ks, 2, self.fold, self.tiles, self.pt).transpose(0, 1, 4, 2, 3, 5)
        return x.reshape(HEAD_DIM, self.ks, self.tiles * LANES)

    def state_out(self, s):
        x = s.reshape(HEAD_DIM, self.ks, self.tiles, 2, self.fold, self.pt).transpose(0, 1, 3, 4, 2, 5)
        x = x.reshape(HEAD_DIM, self.ks, 2, self.fold, self.b, N_HEAD).transpose(4, 2, 5, 0, 3, 1)
        return x.reshape(self.b, 2, N_HEAD, HEAD_DIM, HEAD_DIM)

    def tokens_out(self, o):
        t = o.shape[0]
        x = o.reshape(t, HEAD_DIM, self.tiles, 2 * self.fold, self.pt)[:, :, :, 0]
        return x.reshape(t, HEAD_DIM, self.b, N_HEAD).transpose(2, 0, 3, 1).reshape(self.b * t, D_BRANCH)


def _rwkv_branch(rkv, dec, alpha, lp, s0, b, t):
    geo = _ScanGeom(b)
    r, k, v = (rkv[:, i * D_BRANCH:(i + 1) * D_BRANCH] for i in range(3))
    r, k, w, a = (geo.key_operand(x, t) for x in (r, k, dec, alpha))
    v = geo.value_operand(v, t)
    scan_p = jnp.stack([geo.key_param(lp['k_k']), geo.key_param(lp['k_a'])])
    if s0 is None:
        s0_l = jnp.zeros((HEAD_DIM, geo.ks, geo.tiles * LANES), F32)
    else:
        s0_l = geo.state_in(s0)
    yf, yb, s_fin = _wkv_scan(r, k, w, a, v, scan_p, s0_l, geo.fold)
    post_v = jnp.stack([geo.value_param(lp['ln_g']), geo.value_param(lp['ln_b'])])
    o = _rwkv_post(yf, yb, r, k, v, geo.key_param(lp['r_k']), post_v, geo.fold)
    return geo.tokens_out(o), geo.state_out(s_fin)


def kernel(x_prompt, x_sample, cache_na_k, cache_na_v, state_rwkv, state_ret, c, c_ctx, ada_w, ada_b, norm_mix_pre, norm_mix_post, norm_ffn_pre, norm_ffn_post, w_in, rwkv_conv, rwkv_w0, rwkv_w_up, rwkv_a0, rwkv_a_up, rwkv_g_up, rwkv_k_k, rwkv_k_a, rwkv_r_k, rwkv_ln_g, rwkv_ln_b, na_rpb, ret_decay, ret_gn, w_o_rwkv, w_o_na, w_o_ret, w_out, ffn_up, ffn_conv, ffn_down):
    bc, tc, _ = x_prompt.shape
    bl, tl, _ = x_sample.shape
    nc, nl = bc * tc, bl * tl
    assert tl == TOKEN_TILE and TOKEN_TILE % tc == 0 and nc % TOKEN_TILE == 0 and 1 + bl <= MOD_ROWS
    assert tc & (tc - 1) == 0 and tl % GRID_W == 0
    n_ctx_tiles = nc // TOKEN_TILE

    x = jnp.concatenate([x_prompt.reshape(nc, D_MODEL), x_sample.reshape(nl, D_MODEL)], axis=0)
    cvec = jnp.concatenate([c_ctx[None, :], c, jnp.zeros((MOD_ROWS - 1 - bl, D_MODEL), F32)], axis=0)
    mods = _ada_modulation(cvec, ada_w, ada_b).reshape(DEPTH, MOD_ROWS, 1, 6 * D_MODEL)
    rope_cos, rope_sin = _rope_tables(tl)
    row = lambda p: p.reshape(1, -1)

    new_k, new_v, new_rw, new_rt = [], [], [], []
    for l in range(DEPTH):
        mod = mods[l]
        zpad = jnp.zeros((D_MODEL, RWKV_PAD - RWKV_COLS), F32)
        o_rw, o_na_, o_rt = RWKV_COLS, RWKV_COLS + 3 * D_BRANCH, RWKV_COLS + 7 * D_BRANCH
        w_in_p = jnp.concatenate([w_in[l][:, o_rt:], w_in[l][:, :o_rw], zpad, w_in[l][:, o_rw:o_rt]], axis=1).astype(BF16)
        z = _in_proj(x, mod, row(norm_mix_pre[l]), w_in_p, n_ctx_tiles)

        conv_w = jnp.pad(rwkv_conv[l], ((0, 0), (0, RWKV_PAD - RWKV_COLS)))
        rkv = _rwkv_conv(z, conv_w, n_ctx_tiles, tc, tl)
        wup_pad = jnp.pad(rwkv_w_up[l], ((0, 0), (0, LANES - W_LORA), (0, 0))).astype(BF16)
        aup_pad = jnp.pad(rwkv_a_up[l], ((0, 0), (W_LORA, LANES - W_LORA - A_LORA), (0, 0))).astype(BF16)
        dec, alpha, g_rwkv = _rwkv_lora(
            z, conv_w[:, 3 * D_BRANCH:3 * D_BRANCH + 2 * LANES], wup_pad, aup_pad, rwkv_g_up[l].astype(BF16),
            rwkv_w0[l], rwkv_a0[l], n_ctx_tiles, tc, tl)
        lp = {'k_k': rwkv_k_k[l], 'k_a': rwkv_k_a[l], 'ln_g': rwkv_ln_g[l], 'ln_b': rwkv_ln_b[l],
              'r_k': rwkv_r_k[l].reshape(-1)}
        o_a_ctx, st_rw = _rwkv_branch(rkv[:nc], dec[:, :nc], alpha[:, :nc], lp, None, bc, tc)
        o_a_lat, _ = _rwkv_branch(rkv[nc:], dec[:, nc:], alpha[:, nc:], lp, state_rwkv[:, l], bl, tl)
        o_rwkv = jnp.concatenate([o_a_ctx, o_a_lat], axis=0)

        o_na, k_ctx, v_ctx = _ctx_attention(z, nc + nl, bc, tc)
        o_na = _na_attention(z, o_na, cache_na_k, cache_na_v, _na_bias_table(na_rpb[l]), l, n_ctx_tiles, bl, tl)

        dec_ret = jnp.broadcast_to(ret_decay[l][:, :, None, None], (2, N_HEAD, 8, tl))
        gn = row(ret_gn[l])
        o_ret, st_rt = _retention_ctx(z, dec_ret[..., :tc], gn, nc + nl, bc, tc)
        o_ret = _retention_lat(z, o_ret, dec_ret, gn, rope_cos, rope_sin, state_ret, l, n_ctx_tiles, bl, tl)

        x = _mix_out(x, mod, o_rwkv, g_rwkv, o_na, o_ret, z, w_o_rwkv[l].astype(BF16), w_o_na[l].astype(BF16),
                     w_o_ret[l].astype(BF16), w_out[l].astype(BF16), row(norm_mix_post[l]), n_ctx_tiles)
        x = _ffn(x, mod, row(norm_ffn_pre[l]), ffn_up[l].astype(BF16), ffn_conv[l], ffn_down[l].astype(BF16),
                 row(norm_ffn_post[l]), n_ctx_tiles, tc, tl)

        new_k.append(k_ctx)
        new_v.append(v_ctx)
        new_rw.append(st_rw)
        new_rt.append(st_rt)

    return (x[:nc].reshape(bc, tc, D_MODEL), x[nc:].reshape(bl, tl, D_MODEL), jnp.stack(new_k, axis=1),
            jnp.stack(new_v, axis=1), jnp.stack(new_rw, axis=1), jnp.stack(new_rt, axis=1))
```

```python
import functools
import math

import jax
import jax.numpy as jnp
from jax import lax
from jax.experimental import pallas as pl
from jax.experimental.pallas import tpu as pltpu

F32 = jnp.float32
BF16 = jnp.bfloat16

D_MODEL = 1024
DEPTH = 2
N_HEAD = 8
HEAD_DIM = 64
D_BRANCH = N_HEAD * HEAD_DIM
GRID_W = 64
NA_KH = 8
NA_KW = 16
W_LORA = 64
A_LORA = 64
G_LORA = 128
D_FF = 2816
ROPE_BASE = 10000.0
NEG_BIG = -1e9
RWKV_LN_EPS = 64e-5
RET_GN_EPS = 1e-5
RMS_EPS = 1e-6

RWKV_COLS = 3 * D_BRANCH + W_LORA + A_LORA + G_LORA
RWKV_PAD = 2048
OFF_GATE = 0
OFF_RWKV = 3 * D_MODEL
OFF_NA = OFF_RWKV + RWKV_PAD
OFF_RET = OFF_NA + 3 * D_BRANCH
Z_COLS = OFF_RET + 4 * D_BRANCH

LANES = 128
TOKEN_TILE = 1024
MOD_ROWS = 8
VMEM_LIMIT = 56 * 1024 * 1024


def _cparams(sem):
    return pltpu.CompilerParams(dimension_semantics=sem, vmem_limit_bytes=VMEM_LIMIT)


def _rms(x, g):
    return x * lax.rsqrt(jnp.mean(x * x, axis=-1, keepdims=True) + RMS_EPS) * g


def _softplus(x):
    return jnp.maximum(x, 0.0) + jnp.log1p(jnp.exp(-jnp.abs(x)))


def _dot(a, b):
    return jnp.dot(a, b, preferred_element_type=F32)


def _dot_nt(a, b):
    return lax.dot_general(a, b, (((1,), (1,)), ((), ())), preferred_element_type=F32)


def _dot_tn(a, b):
    return lax.dot_general(a, b, (((0,), (0,)), ((), ())), preferred_element_type=F32)


def _mod_index(i, n_ctx_tiles, tiles_per_lat):
    return jnp.where(i < n_ctx_tiles, 0, 1 + jnp.maximum(i - n_ctx_tiles, 0) // tiles_per_lat)


def _seq_edges(i, n_ctx_tiles, t_ctx, t_lat, tm):
    seqlen = jnp.where(i < n_ctx_tiles, t_ctx, t_lat)
    pos = lax.broadcasted_iota(jnp.int32, (tm, 1), 0) & (seqlen - 1)
    return pos == 0, pos == seqlen - 1


def _dwconv3(u, w, first, last):
    tm = u.shape[0]
    prev = jnp.where(first, 0.0, pltpu.roll(u, 1, 0))
    nxt = jnp.where(last, 0.0, pltpu.roll(u, tm - 1, 0))
    return prev * w[0:1] + u * w[1:2] + nxt * w[2:3]


def _ada_kernel(c_ref, w_ref, b_ref, o_ref):
    c = c_ref[...]
    s = (c * jax.nn.sigmoid(c)).astype(BF16)
    o_ref[0] = _dot(s, w_ref[0].astype(BF16)) + b_ref[0]


def _ada_modulation(cvec, ada_w, ada_b):
    tn = 512
    n_out = 6 * D_MODEL
    return pl.pallas_call(
        _ada_kernel,
        grid=(DEPTH, n_out // tn),
        in_specs=[
            pl.BlockSpec((MOD_ROWS, D_MODEL), lambda l, j: (0, 0)),
            pl.BlockSpec((1, D_MODEL, tn), lambda l, j: (l, 0, j)),
            pl.BlockSpec((1, 1, tn), lambda l, j: (l, 0, j)),
        ],
        out_specs=pl.BlockSpec((1, MOD_ROWS, tn), lambda l, j: (l, 0, j)),
        out_shape=jax.ShapeDtypeStruct((DEPTH, MOD_ROWS, n_out), F32),
        compiler_params=_cparams(("parallel", "parallel")),
    )(cvec, ada_w, ada_b.reshape(DEPTH, 1, n_out))


def _resident(shape):
    return pl.BlockSpec(shape, lambda *_: (0,) * len(shape), pipeline_mode=pl.Buffered(1))


def _in_proj_kernel(x_ref, m_ref, g_ref, w_ref, o_ref, *, tn):
    y = _rms(x_ref[...], g_ref[...])
    h = (y * (1.0 + m_ref[0, :, D_MODEL:2 * D_MODEL]) + m_ref[0, :, 0:D_MODEL]).astype(BF16)
    for c in range(Z_COLS // tn):
        cols = slice(c * tn, (c + 1) * tn)
        o_ref[:, cols] = _dot(h, w_ref[:, cols]).astype(o_ref.dtype)


def _in_proj(x, mod, gain, w_bf16, n_ctx_tiles):
    n = x.shape[0]
    tm, tn = 512, 512
    per = TOKEN_TILE // tm
    midx = functools.partial(_mod_index, n_ctx_tiles=n_ctx_tiles * per, tiles_per_lat=per)
    return pl.pallas_call(
        functools.partial(_in_proj_kernel, tn=tn),
        grid=(n // tm,),
        in_specs=[
            pl.BlockSpec((tm, D_MODEL), lambda i: (i, 0)),
            pl.BlockSpec((1, 1, 6 * D_MODEL), lambda i: (midx(i), 0, 0)),
            pl.BlockSpec((1, D_MODEL), lambda i: (0, 0)),
            _resident((D_MODEL, Z_COLS)),
        ],
        out_specs=pl.BlockSpec((tm, Z_COLS), lambda i: (i, 0)),
        out_shape=jax.ShapeDtypeStruct((n, Z_COLS), BF16),
        compiler_params=_cparams(("parallel",)),
    )(x, mod, gain, w_bf16)


def _stream_edges(t_seq, tm):
    pos = lax.broadcasted_iota(jnp.int32, (tm, 1), 0) & (t_seq - 1)
    return pos == 0, pos == t_seq - 1


def _rwkv_conv_kernel(z_ref, w_ref, o_ref, *, t_seq):
    first, last = _stream_edges(t_seq, z_ref.shape[0])
    o_ref[0] = _dwconv3(z_ref[...].astype(F32), w_ref[...], first, last)


def _rwkv_conv(z, conv_w, row_tile0, n_tiles, t_seq):
    tm, tn = TOKEN_TILE, D_BRANCH
    cb = OFF_RWKV // tn
    return pl.pallas_call(
        functools.partial(_rwkv_conv_kernel, t_seq=t_seq),
        grid=(n_tiles, 3),
        in_specs=[
            pl.BlockSpec((tm, tn), lambda i, j: (row_tile0 + i, cb + j)),
            pl.BlockSpec((3, tn), lambda i, j: (0, j)),
        ],
        out_specs=pl.BlockSpec((1, tm, tn), lambda i, j: (j, i, 0)),
        out_shape=jax.ShapeDtypeStruct((3, n_tiles * tm, tn), F32),
        compiler_params=_cparams(("parallel", "parallel")),
    )(z, conv_w)


def _rwkv_lora_kernel(z_ref, cw_ref, wup_ref, aup_ref, gup_ref, w0_ref, a0_ref, *rest, t_seq):
    dec_ref, al_ref, g_ref = rest[-3:]
    first, last = _stream_edges(t_seq, z_ref.shape[0])
    u = _dwconv3(z_ref[...].astype(F32), cw_ref[...], first, last)
    wa = u[:, 0:LANES]
    tw = jnp.tanh(wa).astype(BF16)
    ab = wa.astype(BF16)
    sg = jax.nn.sigmoid(u[:, LANES:2 * LANES]).astype(BF16)
    for d in range(2):
        x = w0_ref[d:d + 1, :] + _dot(tw, wup_ref[d])
        w_log = -_softplus(-x) - 0.5
        dec_ref[d] = jnp.exp(-jnp.exp(w_log))
        al_ref[d] = jax.nn.sigmoid(a0_ref[d:d + 1, :] + _dot(ab, aup_ref[d]))
    g_ref[...] = _dot(sg, gup_ref[...])


def _rwkv_lora(z, lora_w, g_prev, row_tile0, n_tiles, t_seq):
    n = z.shape[0]
    tm = TOKEN_TILE
    wl = 2 * LANES
    cb = (OFF_RWKV + 3 * D_BRANCH) // wl
    out2 = jax.ShapeDtypeStruct((2, n_tiles * tm, D_BRANCH), F32)
    ospec2 = pl.BlockSpec((2, tm, D_BRANCH), lambda i: (0, i, 0))
    full2 = lambda shape: pl.BlockSpec(shape, lambda i: (0, 0))
    full3 = lambda shape: pl.BlockSpec(shape, lambda i: (0, 0, 0))
    in_specs = [
        pl.BlockSpec((tm, wl), lambda i: (row_tile0 + i, cb)),
        full2((3, wl)),
        full3((2, LANES, D_BRANCH)),
        full3((2, LANES, D_BRANCH)),
        full2((G_LORA, D_BRANCH)),
        full2((2, D_BRANCH)),
        full2((2, D_BRANCH)),
    ]
    args = (z,) + tuple(lora_w)
    aliases = {}
    if g_prev is not None:
        in_specs.append(pl.BlockSpec(memory_space=pl.ANY))
        args += (g_prev,)
        aliases = {len(args) - 1: 2}
    return pl.pallas_call(
        functools.partial(_rwkv_lora_kernel, t_seq=t_seq),
        grid=(n_tiles,),
        in_specs=in_specs,
        out_specs=[ospec2, ospec2, pl.BlockSpec((tm, D_BRANCH), lambda i: (row_tile0 + i, 0))],
        out_shape=[out2, out2, jax.ShapeDtypeStruct((n, D_BRANCH), F32)],
        input_output_aliases=aliases,
        compiler_params=_cparams(("parallel",)),
    )(*args)


HALF = LANES // 2


def _fold_sum(x, lane, fold):
    s = HALF // fold
    while s < HALF:
        x = x + jnp.where((lane & s) != 0, pltpu.roll(x, s, 1), pltpu.roll(x, LANES - s, 1))
        s *= 2
    return x


SCAN_TB = 8


def _wkv_kernel(rf_ref, rb_ref, kf_ref, kb_ref, wf_ref, wb_ref, af_ref, ab_ref, vf_ref, vb_ref, p_ref, s0_ref,
                yf_ref, yb_ref, st_ref, s_scr, op_scr, sa_scr, *, fold, unroll):
    tb = SCAN_TB
    t = pl.program_id(1)

    @pl.when(t == 0)
    def _():
        s_scr[...] = s0_ref[...]

    lane = lax.broadcasted_iota(jnp.int32, (1, LANES), 1)
    is_fwd = lane < HALF
    k_k = p_ref[0]
    k_a = p_ref[1]

    for j in range(tb):
        jb = tb - 1 - j
        k = jnp.where(is_fwd, kf_ref[j], kb_ref[jb])
        a = jnp.where(is_fwd, af_ref[j], ab_ref[jb])
        kk = k * k_k
        kk = kk / jnp.maximum(jnp.sqrt(_fold_sum(jnp.sum(kk * kk, axis=0, keepdims=True), lane, fold)), 1e-12)
        op_scr[j, 0] = -kk
        op_scr[j, 1] = jnp.where(is_fwd, wf_ref[j], wb_ref[jb])
        op_scr[j, 2] = kk * a
        op_scr[j, 3] = k * (1.0 + (a - 1.0) * k_a)
        op_scr[j, 4] = jnp.where(is_fwd, rf_ref[j], rb_ref[jb])

    def step(j, carry):
        jb = tb - 1 - j
        if fold > 1:
            def sa_partial(vi, c):
                sa_scr[pl.ds(vi, 1), :] = jnp.sum(s_scr[vi] * op_scr[j, 0], axis=0, keepdims=True)
                return c

            lax.fori_loop(0, HEAD_DIM, sa_partial, 0, unroll=unroll)
            sa_scr[...] = _fold_sum(sa_scr[...], lane, fold)

        def vstep(vi, c):
            sv = s_scr[vi]
            if fold > 1:
                sa = sa_scr[pl.ds(vi, 1), :]
            else:
                sa = jnp.sum(sv * op_scr[j, 0], axis=0, keepdims=True)
            vrow = jnp.where(is_fwd, vf_ref[j, pl.ds(vi, 1), :], vb_ref[jb, pl.ds(vi, 1), :])
            sn = sv * op_scr[j, 1] + sa * op_scr[j, 2] + vrow * op_scr[j, 3]
            s_scr[vi] = sn
            yrow = jnp.sum(sn * op_scr[j, 4], axis=0, keepdims=True)
            yf_ref[j, pl.ds(vi, 1), :] = yrow
            yb_ref[jb, pl.ds(vi, 1), :] = yrow
            return c

        lax.fori_loop(0, HEAD_DIM, vstep, 0, unroll=unroll)
        return carry

    lax.fori_loop(0, tb, step, 0)

    @pl.when(t == pl.num_programs(1) - 1)
    def _():
        st_ref[...] = s_scr[...]


def _wkv_scan(r, k, w, a, v, params, s0, fold):
    t_len, ks, n_lane = r.shape
    tb = SCAN_TB
    nt = t_len // tb
    fwd = lambda rows: pl.BlockSpec((tb, rows, LANES), lambda c, t: (t, 0, c))
    bwd = lambda rows: pl.BlockSpec((tb, rows, LANES), lambda c, t: (nt - 1 - t, 0, c))
    state = pl.BlockSpec((HEAD_DIM, ks, LANES), lambda c, t: (0, 0, c))
    y_shape = jax.ShapeDtypeStruct((t_len, HEAD_DIM, n_lane), F32)
    return pl.pallas_call(
        functools.partial(_wkv_kernel, fold=fold, unroll=8),
        grid=(n_lane // LANES, nt),
        in_specs=[fwd(ks), bwd(ks)] * 4 + [fwd(HEAD_DIM), bwd(HEAD_DIM),
                                          pl.BlockSpec((2, ks, LANES), lambda c, t: (0, 0, c)), state],
        out_specs=[fwd(HEAD_DIM), bwd(HEAD_DIM), state],
        out_shape=[y_shape, y_shape, jax.ShapeDtypeStruct((HEAD_DIM, ks, n_lane), F32)],
        scratch_shapes=[pltpu.VMEM((HEAD_DIM, ks, LANES), F32), pltpu.VMEM((tb, 5, ks, LANES), F32),
                        pltpu.VMEM((HEAD_DIM, LANES), F32)],
        compiler_params=_cparams(("parallel", "arbitrary")),
    )(r, r, k, k, w, w, a, a, v, v, params, s0)


def _rwkv_post_kernel(yf_ref, yb_ref, r_ref, k_ref, v_ref, pk_ref, pv_ref, o_ref, *, fold):
    lane = lax.broadcasted_iota(jnp.int32, (1, LANES), 1)
    ln_g = pv_ref[0]
    ln_b = pv_ref[1]
    r_k = pk_ref[...]
    for j in range(SCAN_TB):
        y = _fold_sum(yf_ref[j] + pltpu.roll(yb_ref[j], HALF, 1), lane, fold)
        mu = jnp.mean(y, axis=0, keepdims=True)
        dlt = y - mu
        var = jnp.mean(dlt * dlt, axis=0, keepdims=True)
        o = dlt * lax.rsqrt(var + RWKV_LN_EPS) * ln_g + ln_b
        rk = _fold_sum(jnp.sum(r_ref[j] * k_ref[j] * r_k, axis=0, keepdims=True), lane, fold)
        o_ref[j] = o + rk * v_ref[j]


def _rwkv_post(yf, yb, r, k, v, pk, pv, fold):
    t_len, _, n_lane = yf.shape
    ks = r.shape[1]
    spec = lambda rows: pl.BlockSpec((SCAN_TB, rows, LANES), lambda c, i: (i, 0, c))
    return pl.pallas_call(
        functools.partial(_rwkv_post_kernel, fold=fold),
        grid=(n_lane // LANES, t_len // SCAN_TB),
        in_specs=[
            spec(HEAD_DIM), spec(HEAD_DIM), spec(ks), spec(ks), spec(HEAD_DIM),
            pl.BlockSpec((ks, LANES), lambda c, i: (0, c)),
            pl.BlockSpec((2, HEAD_DIM, LANES), lambda c, i: (0, 0, c)),
        ],
        out_specs=spec(HEAD_DIM),
        out_shape=jax.ShapeDtypeStruct((t_len, HEAD_DIM, n_lane), F32),
        compiler_params=_cparams(("parallel", "parallel")),
    )(yf, yb, r, k, v, pk, pv)


def _head(h):
    return slice(h * HEAD_DIM, (h + 1) * HEAD_DIM)


def _ctx_attn_kernel(q_ref, k_ref, v_ref, o_ref, ko_ref, vo_ref):
    scale = HEAD_DIM ** -0.5
    for h in range(N_HEAD):
        k = k_ref[:, _head(h)]
        v = v_ref[:, _head(h)]
        ko_ref[0, h] = k.astype(F32)
        vo_ref[0, h] = v.astype(F32)
        s = _dot_nt(q_ref[:, _head(h)], k) * scale
        p = jnp.exp(s - jnp.max(s, axis=-1, keepdims=True))
        p = p / jnp.sum(p, axis=-1, keepdims=True)
        o_ref[:, _head(h)] = _dot(p.astype(BF16), v)


def _ctx_attention(z, n_rows, b, t):
    cb = OFF_NA // D_BRANCH
    zspec = lambda c: pl.BlockSpec((t, D_BRANCH), lambda i: (i, cb + c))
    kv_shape = jax.ShapeDtypeStruct((b, N_HEAD, t, HEAD_DIM), F32)
    kv_spec = pl.BlockSpec((1, N_HEAD, t, HEAD_DIM), lambda i: (i, 0, 0, 0))
    return pl.pallas_call(
        _ctx_attn_kernel,
        grid=(b,),
        in_specs=[zspec(0), zspec(1), zspec(2)],
        out_specs=[pl.BlockSpec((t, D_BRANCH), lambda i: (i, 0)), kv_spec, kv_spec],
        out_shape=[jax.ShapeDtypeStruct((n_rows, D_BRANCH), F32), kv_shape, kv_shape],
        compiler_params=_cparams(("parallel",)),
    )(z, z, z)


def _na_kernel(q_ref, k_ref, v_ref, kc_ref, vc_ref, tab_ref, prev_ref, o_ref, *, rows):
    del prev_ref
    scale = HEAD_DIM ** -0.5
    win = NA_KH * GRID_W
    qcol = lax.broadcasted_iota(jnp.int32, (GRID_W, win), 0)
    kcol = lax.broadcasted_iota(jnp.int32, (GRID_W, win), 1) & (GRID_W - 1)
    c_start = jnp.clip(qcol - NA_KW // 2, 0, GRID_W - NA_KW)
    col_valid = (kcol >= c_start) & (kcol < c_start + NA_KW)
    for h in range(N_HEAD):
        kc = kc_ref[0, 0, h].astype(BF16)
        vc = vc_ref[0, 0, h].astype(BF16)

        def row_block(r, carry):
            rs = jnp.clip(r - NA_KH // 2, 0, rows - NA_KH)
            q_rows = pl.ds(pl.multiple_of(r * GRID_W, GRID_W), GRID_W)
            w_rows = pl.ds(pl.multiple_of(rs * GRID_W, GRID_W), win)
            q = q_ref[q_rows, _head(h)].astype(BF16)
            kw = k_ref[w_rows, _head(h)].astype(BF16)
            vw = v_ref[w_rows, _head(h)].astype(BF16)
            first = rs - r + NA_KH - 1
            bias = jnp.concatenate([tab_ref[h, first + 2 * i] for i in range(NA_KH // 2)], axis=1)
            s_loc = jnp.where(col_valid, _dot_nt(q, kw) * scale + bias, NEG_BIG)
            s_ctx = _dot_nt(q, kc) * scale
            m = jnp.maximum(jnp.max(s_loc, axis=-1, keepdims=True), jnp.max(s_ctx, axis=-1, keepdims=True))
            p_loc = jnp.exp(s_loc - m)
            p_ctx = jnp.exp(s_ctx - m)
            den = jnp.sum(p_loc, axis=-1, keepdims=True) + jnp.sum(p_ctx, axis=-1, keepdims=True)
            o_ref[q_rows, _head(h)] = _dot((p_loc / den).astype(BF16), vw) + _dot((p_ctx / den).astype(BF16), vc)
            return carry

        lax.fori_loop(0, rows, row_block, 0, unroll=2)


def _na_attention(z, o_prev, cache_k, cache_v, bias_tab, layer, row_tile0, b, t):
    past = cache_k.shape[3]
    rows = t // GRID_W
    assert rows >= NA_KH
    cb = OFF_NA // D_BRANCH
    zspec = lambda c: pl.BlockSpec((t, D_BRANCH), lambda i: (row_tile0 + i, cb + c))
    cspec = pl.BlockSpec((1, 1, N_HEAD, past, HEAD_DIM), lambda i: (i, layer, 0, 0, 0))
    return pl.pallas_call(
        functools.partial(_na_kernel, rows=rows),
        grid=(b,),
        in_specs=[zspec(0), zspec(1), zspec(2), cspec, cspec,
                  pl.BlockSpec(bias_tab.shape, lambda i: (0, 0, 0, 0)),
                  pl.BlockSpec(memory_space=pl.ANY)],
        out_specs=pl.BlockSpec((t, D_BRANCH), lambda i: (row_tile0 + i, 0)),
        out_shape=jax.ShapeDtypeStruct(o_prev.shape, F32),
        input_output_aliases={6: 0},
        compiler_params=_cparams(("parallel",)),
    )(z, z, z, cache_k, cache_v, bias_tab, o_prev)


def _na_bias_table(rpb):
    h, ndr, _ = rpb.shape
    edge = GRID_W - NA_KW
    ext = jnp.concatenate([jnp.broadcast_to(rpb[..., :1], (h, ndr, edge)), rpb,
                           jnp.broadcast_to(rpb[..., -1:], (h, ndr, edge + 1))], axis=-1)
    skew = jnp.tile(ext, (1, 1, GRID_W))[..., :GRID_W * (2 * GRID_W - 1)].reshape(h, ndr, GRID_W, 2 * GRID_W - 1)
    toep = skew[..., GRID_W - 1:]
    return jnp.concatenate([toep[:, :-1], toep[:, 1:]], axis=-1)


def _rope_tables(t_len):
    half = HEAD_DIM // 2
    nf = half // 2
    inv = ROPE_BASE ** (-jnp.arange(nf, dtype=F32) / nf)
    t = jnp.arange(t_len)

    def tab(pos):
        ang = pos.astype(F32)[:, None] * inv[None, :]
        cos, sin = jnp.cos(ang), jnp.sin(ang)
        return jnp.concatenate([cos, cos], -1), jnp.concatenate([-sin, sin], -1)

    c_row, s_row = tab(t // GRID_W)
    c_col, s_col = tab(t % GRID_W)
    cos = jnp.tile(jnp.concatenate([c_row, c_col], -1), (1, N_HEAD))
    sin = jnp.tile(jnp.concatenate([s_row, s_col], -1), (1, N_HEAD))
    return cos, sin


def _log_sigmoid(x):
    return -_softplus(-x)


def _ret_heads(q_ref, k_ref, v_ref, g_ref, dec_ref, gn_ref, o_ref, *, t_len, qb, rope_refs=None, s0_ref=None,
               st_ref=None):
    lo = (lax.broadcasted_iota(jnp.int32, (1, LANES), 1) & 31) < 16
    for hp in range(N_HEAD // 2):
        pair = slice(hp * LANES, (hp + 1) * LANES)
        q2 = q_ref[:, pair].astype(F32)
        k2 = k_ref[:, pair].astype(F32)
        if rope_refs is not None:
            cos = rope_refs[0][:, pair]
            sin = rope_refs[1][:, pair]
            rot = lambda x: x * cos + jnp.where(lo, pltpu.roll(x, LANES - 16, 1), pltpu.roll(x, 16, 1)) * sin
            q2, k2 = rot(q2), rot(k2)
        k2 = k2 * (HEAD_DIM ** -0.5)
        for hh in range(2):
            h = 2 * hp + hh
            half = slice(hh * HEAD_DIM, (hh + 1) * HEAD_DIM)
            qh = q2[:, half].astype(BF16)
            k = k2[:, half]
            kb = k.astype(BF16)
            vb = v_ref[:, _head(h)].astype(BF16)
            lgf = _log_sigmoid(dec_ref[0, h])[0:1, :]
            lgb = _log_sigmoid(dec_ref[1, h])[0:1, :]
            lgf_h = lgf[:, 0:HEAD_DIM]
            lgb_h = lgb[:, 0:HEAD_DIM]
            for qi in range(t_len // qb):
                rows = slice(qi * qb, (qi + 1) * qb)
                q = qh[rows]
                s = _dot_nt(q, kb)
                diff = (lax.broadcasted_iota(jnp.int32, (qb, t_len), 0) + qi * qb
                        - lax.broadcasted_iota(jnp.int32, (qb, t_len), 1)).astype(F32)
                dmat = (jnp.where(diff >= 0, jnp.exp(lgf * jnp.maximum(diff, 0.0)), 0.0)
                        + jnp.where(diff <= 0, jnp.exp(lgb * jnp.maximum(-diff, 0.0)), 0.0))
                y = _dot((s * dmat).astype(BF16), vb)
                if s0_ref is not None:
                    pos = (lax.broadcasted_iota(jnp.int32, (qb, HEAD_DIM), 0) + qi * qb).astype(F32)
                    y = y + _dot(q, s0_ref[0, 0, h].astype(BF16)) * jnp.exp(lgf_h * (pos + 1.0))
                    y = y + _dot(q, s0_ref[0, 1, h].astype(BF16)) * jnp.exp(lgb_h * (t_len - pos))
                mu = jnp.mean(y, axis=-1, keepdims=True)
                dlt = y - mu
                var = jnp.mean(dlt * dlt, axis=-1, keepdims=True)
                yn = dlt * lax.rsqrt(var + RET_GN_EPS)
                g = g_ref[rows, _head(h)].astype(F32)
                o_ref[rows, _head(h)] = (yn * gn_ref[:, _head(h)]) * (g * jax.nn.sigmoid(g))
            if st_ref is not None:
                pos = lax.broadcasted_iota(jnp.int32, (t_len, HEAD_DIM), 0).astype(F32)
                kzf = (k * jnp.exp(lgf_h * (t_len - 1.0 - pos))).astype(BF16)
                kzb = (k * jnp.exp(lgb_h * pos)).astype(BF16)
                sf = _dot_tn(kzf, vb)
                sb = _dot_tn(kzb, vb)
                if s0_ref is not None:
                    sf = sf + s0_ref[0, 0, h] * jnp.exp(lgf_h * float(t_len))
                    sb = sb + s0_ref[0, 1, h] * jnp.exp(lgb_h * float(t_len))
                st_ref[0, 0, h] = sf
                st_ref[0, 1, h] = sb


def _ret_ctx_kernel(q_ref, k_ref, v_ref, g_ref, dec_ref, gn_ref, o_ref, st_ref, *, t_len, qb):
    _ret_heads(q_ref, k_ref, v_ref, g_ref, dec_ref, gn_ref, o_ref, t_len=t_len, qb=qb, st_ref=st_ref)


def _ret_lat_kernel(q_ref, k_ref, v_ref, g_ref, dec_ref, gn_ref, cos_ref, sin_ref, s0_ref, prev_ref, o_ref,
                    *, t_len, qb):
    del prev_ref
    _ret_heads(q_ref, k_ref, v_ref, g_ref, dec_ref, gn_ref, o_ref, t_len=t_len, qb=qb,
               rope_refs=(cos_ref, sin_ref), s0_ref=s0_ref.at[0])


def _ret_specs(t, row_tile0):
    cb = OFF_RET // D_BRANCH
    zspec = lambda c: pl.BlockSpec((t, D_BRANCH), lambda i: (row_tile0 + i, cb + c))
    return [zspec(0), zspec(1), zspec(2), zspec(3),
            pl.BlockSpec((2, N_HEAD, 8, t), lambda i: (0, 0, 0, 0)),
            pl.BlockSpec((1, D_BRANCH), lambda i: (0, 0))]


def _retention_ctx(z, dec, gn, n_rows, b, t):
    st_shape = (b, 2, N_HEAD, HEAD_DIM, HEAD_DIM)
    return pl.pallas_call(
        functools.partial(_ret_ctx_kernel, t_len=t, qb=min(t, 256)),
        grid=(b,),
        in_specs=_ret_specs(t, 0),
        out_specs=[pl.BlockSpec((t, D_BRANCH), lambda i: (i, 0)),
                   pl.BlockSpec((1,) + st_shape[1:], lambda i: (i, 0, 0, 0, 0))],
        out_shape=[jax.ShapeDtypeStruct((n_rows, D_BRANCH), F32), jax.ShapeDtypeStruct(st_shape, F32)],
        compiler_params=_cparams(("parallel",)),
    )(z, z, z, z, dec, gn)


def _retention_lat(z, o_prev, dec, gn, cos, sin, s0, layer, row_tile0, b, t):
    tab = pl.BlockSpec((t, D_BRANCH), lambda i: (0, 0))
    return pl.pallas_call(
        functools.partial(_ret_lat_kernel, t_len=t, qb=min(t, 256)),
        grid=(b,),
        in_specs=_ret_specs(t, row_tile0) + [
            tab, tab,
            pl.BlockSpec((1, 1, 2, N_HEAD, HEAD_DIM, HEAD_DIM), lambda i: (i, layer, 0, 0, 0, 0)),
            pl.BlockSpec(memory_space=pl.ANY)],
        out_specs=pl.BlockSpec((t, D_BRANCH), lambda i: (row_tile0 + i, 0)),
        out_shape=jax.ShapeDtypeStruct(o_prev.shape, F32),
        input_output_aliases={9: 0},
        compiler_params=_cparams(("parallel",)),
    )(z, z, z, z, dec, gn, cos, sin, s0, o_prev)


def _mix_out_kernel(x_ref, m_ref, oa_ref, ga_ref, ob_ref, oc_ref, g0_ref, g1_ref, g2_ref,
                    wa_ref, wb_ref, wc_ref, wo_ref, gain_ref, o_ref):
    out_a = _dot((oa_ref[...] * ga_ref[...]).astype(BF16), wa_ref[...])
    out_b = _dot(ob_ref[...].astype(BF16), wb_ref[...])
    out_c = _dot(oc_ref[...].astype(BF16), wc_ref[...])
    sig = lambda ref: jax.nn.sigmoid(ref[...].astype(F32))
    merged = sig(g0_ref) * out_a + sig(g1_ref) * out_b + sig(g2_ref) * out_c
    y = _dot(merged.astype(BF16), wo_ref[...])
    gate = m_ref[0, :, 2 * D_MODEL:3 * D_MODEL]
    o_ref[...] = x_ref[...] + gate * _rms(y, gain_ref[...])


def _mix_out(x, mod, o_rwkv, g_rwkv, o_na, o_ret, z, wa, wb, wc, wo, gain, n_ctx_tiles):
    n = x.shape[0]
    tm = 256
    per = TOKEN_TILE // tm
    midx = functools.partial(_mod_index, n_ctx_tiles=n_ctx_tiles * per, tiles_per_lat=per)
    row = lambda w: pl.BlockSpec((tm, w), lambda i: (i, 0))
    gate_spec = lambda g: pl.BlockSpec((tm, D_MODEL), lambda i: (i, OFF_GATE // D_MODEL + g))
    wspec = lambda a, b: pl.BlockSpec((a, b), lambda i: (0, 0))
    return pl.pallas_call(
        _mix_out_kernel,
        grid=(n // tm,),
        in_specs=[
            row(D_MODEL),
            pl.BlockSpec((1, 1, 6 * D_MODEL), lambda i: (midx(i), 0, 0)),
            row(D_BRANCH), row(D_BRANCH), row(D_BRANCH), row(D_BRANCH),
            gate_spec(0), gate_spec(1), gate_spec(2),
            wspec(D_BRANCH, D_MODEL), wspec(D_BRANCH, D_MODEL), wspec(D_BRANCH, D_MODEL),
            wspec(D_MODEL, D_MODEL), wspec(1, D_MODEL),
        ],
        out_specs=row(D_MODEL),
        out_shape=jax.ShapeDtypeStruct((n, D_MODEL), F32),
        compiler_params=_cparams(("parallel",)),
    )(x, mod, o_rwkv, g_rwkv, o_na, o_ret, z, z, z, wa, wb, wc, wo, gain)


def _gelu_tanh(x):
    return x * (0.5 * (1.0 + jnp.tanh(math.sqrt(2.0 / math.pi) * (x + 0.044715 * (x * x * x)))))


FFN_CHUNK = 256


def _ffn_kernel(x_ref, m_ref, gpre_ref, wup_ref, cw_ref, wd_ref, gpost_ref, o_ref, act_scr,
                *, n_ctx_tiles, t_ctx, t_lat):
    tm = x_ref.shape[0]
    y = _rms(x_ref[...], gpre_ref[...])
    h = (y * (1.0 + m_ref[0, :, 4 * D_MODEL:5 * D_MODEL]) + m_ref[0, :, 3 * D_MODEL:4 * D_MODEL]).astype(BF16)
    first, last = _seq_edges(pl.program_id(0), n_ctx_tiles, t_ctx, t_lat, tm)
    for c in range(D_FF // FFN_CHUNK):
        vcols = slice(c * FFN_CHUNK, (c + 1) * FFN_CHUNK)
        gcols = slice(D_FF + c * FFN_CHUNK, D_FF + (c + 1) * FFN_CHUNK)
        val = _dwconv3(_dot(h, wup_ref[:, vcols]), cw_ref[:, vcols], first, last)
        gate = _dwconv3(_dot(h, wup_ref[:, gcols]), cw_ref[:, gcols], first, last)
        act_scr[:, vcols] = (_gelu_tanh(gate) * val).astype(BF16)
    gate2 = m_ref[0, :, 5 * D_MODEL:6 * D_MODEL]
    for half in range(2):
        rows = slice(half * tm // 2, (half + 1) * tm // 2)
        down = _dot(act_scr[rows, :], wd_ref[...])
        o_ref[rows, :] = x_ref[rows, :] + gate2 * _rms(down, gpost_ref[...])


def _ffn(x, mod, gpre, w_up, conv_w, w_down, gpost, n_ctx_tiles, t_ctx, t_lat):
    n = x.shape[0]
    tm = TOKEN_TILE
    midx = functools.partial(_mod_index, n_ctx_tiles=n_ctx_tiles, tiles_per_lat=1)
    return pl.pallas_call(
        functools.partial(_ffn_kernel, n_ctx_tiles=n_ctx_tiles, t_ctx=t_ctx, t_lat=t_lat),
        grid=(n // tm,),
        in_specs=[
            pl.BlockSpec((tm, D_MODEL), lambda i: (i, 0)),
            pl.BlockSpec((1, 1, 6 * D_MODEL), lambda i: (midx(i), 0, 0)),
            pl.BlockSpec((1, D_MODEL), lambda i: (0, 0)),
            _resident((D_MODEL, 2 * D_FF)),
            _resident((3, 2 * D_FF)),
            _resident((D_FF, D_MODEL)),
            pl.BlockSpec((1, D_MODEL), lambda i: (0, 0)),
        ],
        out_specs=pl.BlockSpec((tm, D_MODEL), lambda i: (i, 0)),
        out_shape=jax.ShapeDtypeStruct((n, D_MODEL), F32),
        scratch_shapes=[pltpu.VMEM((tm, D_FF), BF16)],
        compiler_params=_cparams(("parallel",)),
    )(x, mod, gpre, w_up, conv_w, w_down, gpost)


class _ScanGeom:
    def __init__(self, b):
        self.b = b
        self.pairs = b * N_HEAD
        self.fold = max(1, HALF // self.pairs)
        self.ks = HEAD_DIM // self.fold
        self.pt = HALF // self.fold
        assert self.pairs % self.pt == 0 and self.ks % 8 == 0
        self.tiles = self.pairs // self.pt

    def lanes(self, x, lead):
        return jnp.broadcast_to(x, lead + (self.tiles, 2, self.fold, self.pt)).reshape(lead + (self.tiles * LANES,))

    def key_rows(self, x, t):
        x = x.reshape(self.b, t, N_HEAD, self.fold, self.ks).transpose(1, 4, 3, 0, 2)
        return x.reshape(t, self.ks, self.fold, self.tiles, self.pt).transpose(0, 1, 3, 2, 4)

    def key_operand(self, xs, t):
        if xs.ndim == 2:
            x = self.key_rows(xs, t)[:, :, :, None]
        else:
            x = xs.reshape(2, self.b, t, N_HEAD, self.fold, self.ks).transpose(2, 5, 0, 4, 1, 3)
            x = x.reshape(t, self.ks, 2, self.fold, self.tiles, self.pt).transpose(0, 1, 4, 2, 3, 5)
        return self.lanes(x, (t, self.ks))

    def value_operand(self, x, t):
        x = x.reshape(self.b, t, N_HEAD, HEAD_DIM).transpose(1, 3, 0, 2)
        return self.lanes(x.reshape(t, HEAD_DIM, self.tiles, 1, 1, self.pt), (t, HEAD_DIM))

    def key_param(self, p):
        x = jnp.broadcast_to(p.reshape(1, N_HEAD, self.fold, self.ks), (self.b, N_HEAD, self.fold, self.ks))
        x = x.reshape(self.tiles, self.pt, self.fold, self.ks).transpose(3, 0, 2, 1)
        return self.lanes(x[:, :, None], (self.ks,))

    def value_param(self, p):
        x = jnp.broadcast_to(p.reshape(1, N_HEAD, HEAD_DIM), (self.b, N_HEAD, HEAD_DIM))
        x = x.reshape(self.tiles, self.pt, HEAD_DIM).transpose(2, 0, 1)
        return self.lanes(x[:, :, None, None], (HEAD_DIM,))

    def state_in(self, s0):
        x = s0.reshape(self.b, 2, N_HEAD, HEAD_DIM, self.fold, self.ks).transpose(3, 5, 1, 4, 0, 2)
        x = x.reshape(HEAD_DIM, self.ks, 2, self.fold, self.tiles, self.pt).transpose(0, 1, 4, 2, 3, 5)
        return x.reshape(HEAD_DIM, self.ks, self.tiles * LANES)

    def state_out(self, s):
        x = s.reshape(HEAD_DIM, self.ks, self.tiles, 2, self.fold, self.pt).transpose(0, 1, 3, 4, 2, 5)
        x = x.reshape(HEAD_DIM, self.ks, 2, self.fold, self.b, N_HEAD).transpose(4, 2, 5, 0, 3, 1)
        return x.reshape(self.b, 2, N_HEAD, HEAD_DIM, HEAD_DIM)

    def tokens_out(self, o):
        t = o.shape[0]
        x = o.reshape(t, HEAD_DIM, self.tiles, 2 * self.fold, self.pt)[:, :, :, 0]
        return x.reshape(t, HEAD_DIM, self.b, N_HEAD).transpose(2, 0, 3, 1).reshape(self.b * t, D_BRANCH)


def _rwkv_branch(rkv, dec, alpha, lp, s0, b, t):
    geo = _ScanGeom(b)
    r, k, w, a = (geo.key_operand(x, t) for x in (rkv[0], rkv[1], dec, alpha))
    v = geo.value_operand(rkv[2], t)
    scan_p = jnp.stack([geo.key_param(lp['k_k']), geo.key_param(lp['k_a'])])
    if s0 is None:
        s0_l = jnp.zeros((HEAD_DIM, geo.ks, geo.tiles * LANES), F32)
    else:
        s0_l = geo.state_in(s0)
    yf, yb, s_fin = _wkv_scan(r, k, w, a, v, scan_p, s0_l, geo.fold)
    post_v = jnp.stack([geo.value_param(lp['ln_g']), geo.value_param(lp['ln_b'])])
    o = _rwkv_post(yf, yb, r, k, v, geo.key_param(lp['r_k']), post_v, geo.fold)
    return geo.tokens_out(o), geo.state_out(s_fin)


def kernel(x_prompt, x_sample, cache_na_k, cache_na_v, state_rwkv, state_ret, c, c_ctx, ada_w, ada_b, norm_mix_pre, norm_mix_post, norm_ffn_pre, norm_ffn_post, w_in, rwkv_conv, rwkv_w0, rwkv_w_up, rwkv_a0, rwkv_a_up, rwkv_g_up, rwkv_k_k, rwkv_k_a, rwkv_r_k, rwkv_ln_g, rwkv_ln_b, na_rpb, ret_decay, ret_gn, w_o_rwkv, w_o_na, w_o_ret, w_out, ffn_up, ffn_conv, ffn_down):
    bc, tc, _ = x_prompt.shape
    bl, tl, _ = x_sample.shape
    nc, nl = bc * tc, bl * tl
    assert tl == TOKEN_TILE and TOKEN_TILE % tc == 0 and nc % TOKEN_TILE == 0 and 1 + bl <= MOD_ROWS
    assert tc & (tc - 1) == 0 and tl % GRID_W == 0
    n_ctx_tiles = nc // TOKEN_TILE

    x = jnp.concatenate([x_prompt.reshape(nc, D_MODEL), x_sample.reshape(nl, D_MODEL)], axis=0)
    cvec = jnp.concatenate([c_ctx[None, :], c, jnp.zeros((MOD_ROWS - 1 - bl, D_MODEL), F32)], axis=0)
    mods = _ada_modulation(cvec, ada_w, ada_b).reshape(DEPTH, MOD_ROWS, 1, 6 * D_MODEL)
    rope_cos, rope_sin = _rope_tables(tl)
    row = lambda p: p.reshape(1, -1)

    new_k, new_v, new_rw, new_rt = [], [], [], []
    for l in range(DEPTH):
        mod = mods[l]
        zpad = jnp.zeros((D_MODEL, RWKV_PAD - RWKV_COLS), F32)
        o_rw, o_na_, o_rt = RWKV_COLS, RWKV_COLS + 3 * D_BRANCH, RWKV_COLS + 7 * D_BRANCH
        w_in_p = jnp.concatenate([w_in[l][:, o_rt:], w_in[l][:, :o_rw], zpad, w_in[l][:, o_rw:o_rt]], axis=1).astype(BF16)
        z = _in_proj(x, mod, row(norm_mix_pre[l]), w_in_p, n_ctx_tiles)

        conv_w = jnp.pad(rwkv_conv[l], ((0, 0), (0, RWKV_PAD - RWKV_COLS)))
        wup_pad = jnp.pad(rwkv_w_up[l], ((0, 0), (0, LANES - W_LORA), (0, 0))).astype(BF16)
        aup_pad = jnp.pad(rwkv_a_up[l], ((0, 0), (W_LORA, LANES - W_LORA - A_LORA), (0, 0))).astype(BF16)
        lora_w = (conv_w[:, 3 * D_BRANCH:3 * D_BRANCH + 2 * LANES], wup_pad, aup_pad, rwkv_g_up[l].astype(BF16),
                  rwkv_w0[l], rwkv_a0[l])
        lp = {'k_k': rwkv_k_k[l], 'k_a': rwkv_k_a[l], 'ln_g': rwkv_ln_g[l], 'ln_b': rwkv_ln_b[l],
              'r_k': rwkv_r_k[l].reshape(-1)}
        n_lat_tiles = nl // TOKEN_TILE
        rkv_c = _rwkv_conv(z, conv_w, 0, n_ctx_tiles, tc)
        dec_c, alpha_c, g_rwkv = _rwkv_lora(z, lora_w, None, 0, n_ctx_tiles, tc)
        rkv_l = _rwkv_conv(z, conv_w, n_ctx_tiles, n_lat_tiles, tl)
        dec_l, alpha_l, g_rwkv = _rwkv_lora(z, lora_w, g_rwkv, n_ctx_tiles, n_lat_tiles, tl)
        o_a_ctx, st_rw = _rwkv_branch(rkv_c, dec_c, alpha_c, lp, None, bc, tc)
        o_a_lat, _ = _rwkv_branch(rkv_l, dec_l, alpha_l, lp, state_rwkv[:, l], bl, tl)
        o_rwkv = jnp.concatenate([o_a_ctx, o_a_lat], axis=0)

        o_na, k_ctx, v_ctx = _ctx_attention(z, nc + nl, bc, tc)
        o_na = _na_attention(z, o_na, cache_na_k, cache_na_v, _na_bias_table(na_rpb[l]), l, n_ctx_tiles, bl, tl)

        dec_ret = jnp.broadcast_to(ret_decay[l][:, :, None, None], (2, N_HEAD, 8, tl))
        gn = row(ret_gn[l])
        o_ret, st_rt = _retention_ctx(z, dec_ret[..., :tc], gn, nc + nl, bc, tc)
        o_ret = _retention_lat(z, o_ret, dec_ret, gn, rope_cos, rope_sin, state_ret, l, n_ctx_tiles, bl, tl)

        x = _mix_out(x, mod, o_rwkv, g_rwkv, o_na, o_ret, z, w_o_rwkv[l].astype(BF16), w_o_na[l].astype(BF16),
                     w_o_ret[l].astype(BF16), w_out[l].astype(BF16), row(norm_mix_post[l]), n_ctx_tiles)
        x = _ffn(x, mod, row(norm_ffn_pre[l]), ffn_up[l].astype(BF16), ffn_conv[l], ffn_down[l].astype(BF16),
                 row(norm_ffn_post[l]), n_ctx_tiles, tc, tl)

        new_k.append(k_ctx)
        new_v.append(v_ctx)
        new_rw.append(st_rw)
        new_rt.append(st_rt)

    return (x[:nc].reshape(bc, tc, D_MODEL), x[nc:].reshape(bl, tl, D_MODEL), jnp.stack(new_k, axis=1),
            jnp.stack(new_v, axis=1), jnp.stack(new_rw, axis=1), jnp.stack(new_rt, axis=1))
```

```python
import functools
import math

import jax
import jax.numpy as jnp
from jax import lax
from jax.experimental import pallas as pl
from jax.experimental.pallas import tpu as pltpu

F32 = jnp.float32
BF16 = jnp.bfloat16

D_MODEL = 1024
DEPTH = 2
N_HEAD = 8
HEAD_DIM = 64
D_BRANCH = N_HEAD * HEAD_DIM
GRID_W = 64
NA_KH = 8
NA_KW = 16
W_LORA = 64
A_LORA = 64
G_LORA = 128
D_FF = 2816
ROPE_BASE = 10000.0
NEG_BIG = -1e9
RWKV_LN_EPS = 64e-5
RET_GN_EPS = 1e-5
RMS_EPS = 1e-6

RWKV_COLS = 3 * D_BRANCH + W_LORA + A_LORA + G_LORA
RWKV_PAD = 2048
OFF_GATE = 0
OFF_RWKV = 3 * D_MODEL
OFF_NA = OFF_RWKV + RWKV_PAD
OFF_RET = OFF_NA + 3 * D_BRANCH
Z_COLS = OFF_RET + 4 * D_BRANCH

LANES = 128
TOKEN_TILE = 1024
MOD_ROWS = 8
VMEM_LIMIT = 56 * 1024 * 1024


def _cparams(sem):
    return pltpu.CompilerParams(dimension_semantics=sem, vmem_limit_bytes=VMEM_LIMIT)


def _rms(x, g):
    return x * lax.rsqrt(jnp.mean(x * x, axis=-1, keepdims=True) + RMS_EPS) * g


def _softplus(x):
    return jnp.maximum(x, 0.0) + jnp.log1p(jnp.exp(-jnp.abs(x)))


def _dot(a, b):
    return jnp.dot(a, b, preferred_element_type=F32)


def _dot_nt(a, b):
    return lax.dot_general(a, b, (((1,), (1,)), ((), ())), preferred_element_type=F32)


def _dot_tn(a, b):
    return lax.dot_general(a, b, (((0,), (0,)), ((), ())), preferred_element_type=F32)


def _mod_index(i, n_ctx_tiles, tiles_per_lat):
    return jnp.where(i < n_ctx_tiles, 0, 1 + jnp.maximum(i - n_ctx_tiles, 0) // tiles_per_lat)


def _seq_edges(i, n_ctx_tiles, t_ctx, t_lat, tm):
    seqlen = jnp.where(i < n_ctx_tiles, t_ctx, t_lat)
    pos = lax.broadcasted_iota(jnp.int32, (tm, 1), 0) & (seqlen - 1)
    return pos == 0, pos == seqlen - 1


def _dwconv3(u, w, first, last):
    tm = u.shape[0]
    prev = jnp.where(first, 0.0, pltpu.roll(u, 1, 0))
    nxt = jnp.where(last, 0.0, pltpu.roll(u, tm - 1, 0))
    return prev * w[0:1] + u * w[1:2] + nxt * w[2:3]


def _ada_kernel(c_ref, w_ref, b_ref, o_ref):
    c = c_ref[...]
    s = (c * jax.nn.sigmoid(c)).astype(BF16)
    o_ref[0] = _dot(s, w_ref[0].astype(BF16)) + b_ref[0]


def _ada_modulation(cvec, ada_w, ada_b):
    tn = 512
    n_out = 6 * D_MODEL
    return pl.pallas_call(
        _ada_kernel,
        grid=(DEPTH, n_out // tn),
        in_specs=[
            pl.BlockSpec((MOD_ROWS, D_MODEL), lambda l, j: (0, 0)),
            pl.BlockSpec((1, D_MODEL, tn), lambda l, j: (l, 0, j)),
            pl.BlockSpec((1, 1, tn), lambda l, j: (l, 0, j)),
        ],
        out_specs=pl.BlockSpec((1, MOD_ROWS, tn), lambda l, j: (l, 0, j)),
        out_shape=jax.ShapeDtypeStruct((DEPTH, MOD_ROWS, n_out), F32),
        compiler_params=_cparams(("parallel", "parallel")),
    )(cvec, ada_w, ada_b.reshape(DEPTH, 1, n_out))


def _resident(shape):
    return pl.BlockSpec(shape, lambda *_: (0,) * len(shape), pipeline_mode=pl.Buffered(1))


_W_RWKV, _W_NA, _W_RET, _W_GATE = 0, RWKV_COLS, RWKV_COLS + 3 * D_BRANCH, RWKV_COLS + 7 * D_BRANCH
IN_PROJ_GROUPS = ((_W_GATE, OFF_GATE, 3 * D_MODEL), (_W_RWKV, OFF_RWKV, RWKV_COLS),
                  (_W_NA, OFF_NA, 3 * D_BRANCH), (_W_RET, OFF_RET, 4 * D_BRANCH))
IN_COLS = _W_GATE + 3 * D_MODEL


def _in_proj_kernel(x_ref, m_ref, g_ref, w_ref, o_ref, *, tn):
    y = _rms(x_ref[...], g_ref[...])
    h = (y * (1.0 + m_ref[0, :, D_MODEL:2 * D_MODEL]) + m_ref[0, :, 0:D_MODEL]).astype(BF16)
    for src, dst, width in IN_PROJ_GROUPS:
        for c in range(0, width, tn):
            n = min(tn, width - c)
            o_ref[:, dst + c:dst + c + n] = _dot(h, w_ref[:, src + c:src + c + n]).astype(o_ref.dtype)
    pad = slice(OFF_RWKV + RWKV_COLS, OFF_RWKV + RWKV_PAD)
    o_ref[:, pad] = jnp.zeros((o_ref.shape[0], RWKV_PAD - RWKV_COLS), o_ref.dtype)


def _in_proj(x, mod, gain, w_bf16, n_ctx_tiles):
    n = x.shape[0]
    tm, tn = 512, 512
    per = TOKEN_TILE // tm
    midx = functools.partial(_mod_index, n_ctx_tiles=n_ctx_tiles * per, tiles_per_lat=per)
    return pl.pallas_call(
        functools.partial(_in_proj_kernel, tn=tn),
        grid=(n // tm,),
        in_specs=[
            pl.BlockSpec((tm, D_MODEL), lambda i: (i, 0)),
            pl.BlockSpec((1, 1, 6 * D_MODEL), lambda i: (midx(i), 0, 0)),
            pl.BlockSpec((1, D_MODEL), lambda i: (0, 0)),
            _resident((D_MODEL, IN_COLS)),
        ],
        out_specs=pl.BlockSpec((tm, Z_COLS), lambda i: (i, 0)),
        out_shape=jax.ShapeDtypeStruct((n, Z_COLS), BF16),
        compiler_params=_cparams(("parallel",)),
    )(x, mod, gain, w_bf16)


def _stream_edges(t_seq, tm):
    pos = lax.broadcasted_iota(jnp.int32, (tm, 1), 0) & (t_seq - 1)
    return pos == 0, pos == t_seq - 1


def _rwkv_conv_kernel(z_ref, w_ref, o_ref, *, t_seq):
    first, last = _stream_edges(t_seq, z_ref.shape[0])
    o_ref[0] = _dwconv3(z_ref[...].astype(F32), w_ref[...], first, last)


def _rwkv_conv(z, conv_w, row_tile0, n_tiles, t_seq):
    tm, tn = TOKEN_TILE, D_BRANCH
    cb = OFF_RWKV // tn
    return pl.pallas_call(
        functools.partial(_rwkv_conv_kernel, t_seq=t_seq),
        grid=(n_tiles, 3),
        in_specs=[
            pl.BlockSpec((tm, tn), lambda i, j: (row_tile0 + i, cb + j)),
            pl.BlockSpec((3, tn), lambda i, j: (0, j)),
        ],
        out_specs=pl.BlockSpec((1, tm, tn), lambda i, j: (j, i, 0)),
        out_shape=jax.ShapeDtypeStruct((3, n_tiles * tm, tn), F32),
        compiler_params=_cparams(("parallel", "parallel")),
    )(z, conv_w)


def _rwkv_lora_kernel(z_ref, cw_ref, wup_ref, aup_ref, gup_ref, w0_ref, a0_ref, *rest, t_seq):
    dec_ref, al_ref, g_ref = rest[-3:]
    first, last = _stream_edges(t_seq, z_ref.shape[0])
    u = _dwconv3(z_ref[...].astype(F32), cw_ref[...], first, last)
    wa = u[:, 0:LANES]
    tw = jnp.tanh(wa).astype(BF16)
    ab = wa.astype(BF16)
    sg = jax.nn.sigmoid(u[:, LANES:2 * LANES]).astype(BF16)
    for d in range(2):
        x = w0_ref[d:d + 1, :] + _dot(tw, wup_ref[d])
        w_log = -_softplus(-x) - 0.5
        dec_ref[d] = jnp.exp(-jnp.exp(w_log))
        al_ref[d] = jax.nn.sigmoid(a0_ref[d:d + 1, :] + _dot(ab, aup_ref[d]))
    g_ref[...] = _dot(sg, gup_ref[...])


def _rwkv_lora(z, lora_w, g_prev, row_tile0, n_tiles, t_seq):
    n = z.shape[0]
    tm = TOKEN_TILE
    wl = 2 * LANES
    cb = (OFF_RWKV + 3 * D_BRANCH) // wl
    out2 = jax.ShapeDtypeStruct((2, n_tiles * tm, D_BRANCH), F32)
    ospec2 = pl.BlockSpec((2, tm, D_BRANCH), lambda i: (0, i, 0))
    full2 = lambda shape: pl.BlockSpec(shape, lambda i: (0, 0))
    full3 = lambda shape: pl.BlockSpec(shape, lambda i: (0, 0, 0))
    in_specs = [
        pl.BlockSpec((tm, wl), lambda i: (row_tile0 + i, cb)),
        full2((3, wl)),
        full3((2, LANES, D_BRANCH)),
        full3((2, LANES, D_BRANCH)),
        full2((G_LORA, D_BRANCH)),
        full2((2, D_BRANCH)),
        full2((2, D_BRANCH)),
    ]
    args = (z,) + tuple(lora_w)
    aliases = {}
    if g_prev is not None:
        in_specs.append(pl.BlockSpec(memory_space=pl.ANY))
        args += (g_prev,)
        aliases = {len(args) - 1: 2}
    return pl.pallas_call(
        functools.partial(_rwkv_lora_kernel, t_seq=t_seq),
        grid=(n_tiles,),
        in_specs=in_specs,
        out_specs=[ospec2, ospec2, pl.BlockSpec((tm, D_BRANCH), lambda i: (row_tile0 + i, 0))],
        out_shape=[out2, out2, jax.ShapeDtypeStruct((n, D_BRANCH), F32)],
        input_output_aliases=aliases,
        compiler_params=_cparams(("parallel",)),
    )(*args)


HALF = LANES // 2


def _fold_sum(x, lane, fold):
    s = HALF // fold
    while s < HALF:
        x = x + jnp.where((lane & s) != 0, pltpu.roll(x, s, 1), pltpu.roll(x, LANES - s, 1))
        s *= 2
    return x


SCAN_TB = 8


GROUPS_PER_PASS = 4


def _wkv_kernel(rf_ref, rb_ref, kf_ref, kb_ref, wf_ref, wb_ref, af_ref, ab_ref, vf_ref, vb_ref, p_ref, s0_ref,
                yf_ref, yb_ref, st_ref, s_scr, op_scr, *, fold, unroll):
    tb = SCAN_TB
    ks = s_scr.shape[1]
    t = pl.program_id(1)

    @pl.when(t == 0)
    def _():
        s_scr[...] = s0_ref[...]

    lane = lax.broadcasted_iota(jnp.int32, (1, LANES), 1)
    is_fwd = lane < HALF
    k_k = p_ref[0]
    k_a = p_ref[1]

    for j in range(tb):
        jb = tb - 1 - j
        k = jnp.where(is_fwd, kf_ref[j], kb_ref[jb])
        a = jnp.where(is_fwd, af_ref[j], ab_ref[jb])
        kk = k * k_k
        kk = kk / jnp.maximum(jnp.sqrt(_fold_sum(jnp.sum(kk * kk, axis=0, keepdims=True), lane, fold)), 1e-12)
        op_scr[j, 0] = -kk
        op_scr[j, 1] = jnp.where(is_fwd, wf_ref[j], wb_ref[jb])
        op_scr[j, 2] = kk * a
        op_scr[j, 3] = k * (1.0 + (a - 1.0) * k_a)
        op_scr[j, 4] = jnp.where(is_fwd, rf_ref[j], rb_ref[jb])

    n_grp = s_scr.shape[0]
    zeros = lambda n: tuple(jnp.zeros((8, LANES), F32) for _ in range(n))
    op_row = lambda j, i, k: op_scr[j, i, pl.ds(k, 1), :]

    def step(j, carry):
        jb = tb - 1 - j

        def sa_acc(k, acc):
            a = op_row(j, 0, k)
            return tuple(acc[g] + s_scr[g, k] * a for g in range(n_grp))

        sa = lax.fori_loop(0, ks, sa_acc, zeros(n_grp), unroll=unroll)
        sa = tuple(_fold_sum(x, lane, fold) for x in sa)

        for first in range(0, n_grp, GROUPS_PER_PASS):
            groups = range(first, first + GROUPS_PER_PASS)
            rows = lambda g: slice(g * 8, (g + 1) * 8)
            vs = [jnp.where(is_fwd, vf_ref[j, rows(g), :], vb_ref[jb, rows(g), :]) for g in groups]

            def update(k, acc):
                w, b, kd, r = (op_row(j, i, k) for i in (1, 2, 3, 4))
                out = []
                for i, g in enumerate(groups):
                    sn = s_scr[g, k] * w + sa[g] * b + vs[i] * kd
                    s_scr[g, k] = sn
                    out.append(acc[i] + sn * r)
                return tuple(out)

            ys = lax.fori_loop(0, ks, update, zeros(GROUPS_PER_PASS), unroll=unroll)
            for i, g in enumerate(groups):
                yf_ref[j, rows(g), :] = ys[i]
                yb_ref[jb, rows(g), :] = ys[i]
        return carry

    lax.fori_loop(0, tb, step, 0)

    @pl.when(t == pl.num_programs(1) - 1)
    def _():
        st_ref[...] = s_scr[...]


def _wkv_scan(r, k, w, a, v, params, s0, fold):
    t_len, ks, n_lane = r.shape
    tb = SCAN_TB
    nt = t_len // tb
    n_grp = HEAD_DIM // 8
    fwd = lambda rows: pl.BlockSpec((tb, rows, LANES), lambda c, t: (t, 0, c))
    bwd = lambda rows: pl.BlockSpec((tb, rows, LANES), lambda c, t: (nt - 1 - t, 0, c))
    state = pl.BlockSpec((n_grp, ks, 8, LANES), lambda c, t: (0, 0, 0, c))
    y_shape = jax.ShapeDtypeStruct((t_len, HEAD_DIM, n_lane), F32)
    return pl.pallas_call(
        functools.partial(_wkv_kernel, fold=fold, unroll=8),
        grid=(n_lane // LANES, nt),
        in_specs=[fwd(ks), bwd(ks)] * 4 + [fwd(HEAD_DIM), bwd(HEAD_DIM),
                                          pl.BlockSpec((2, ks, LANES), lambda c, t: (0, 0, c)), state],
        out_specs=[fwd(HEAD_DIM), bwd(HEAD_DIM), state],
        out_shape=[y_shape, y_shape, jax.ShapeDtypeStruct((n_grp, ks, 8, n_lane), F32)],
        scratch_shapes=[pltpu.VMEM((n_grp, ks, 8, LANES), F32), pltpu.VMEM((tb, 5, ks, LANES), F32)],
        compiler_params=_cparams(("parallel", "arbitrary")),
    )(r, r, k, k, w, w, a, a, v, v, params, s0)


def _rwkv_post_kernel(yf_ref, yb_ref, r_ref, k_ref, v_ref, pk_ref, pv_ref, o_ref, *, fold):
    lane = lax.broadcasted_iota(jnp.int32, (1, LANES), 1)
    ln_g = pv_ref[0]
    ln_b = pv_ref[1]
    r_k = pk_ref[...]
    for j in range(SCAN_TB):
        y = _fold_sum(yf_ref[j] + pltpu.roll(yb_ref[j], HALF, 1), lane, fold)
        mu = jnp.mean(y, axis=0, keepdims=True)
        dlt = y - mu
        var = jnp.mean(dlt * dlt, axis=0, keepdims=True)
        o = dlt * lax.rsqrt(var + RWKV_LN_EPS) * ln_g + ln_b
        rk = _fold_sum(jnp.sum(r_ref[j] * k_ref[j] * r_k, axis=0, keepdims=True), lane, fold)
        o_ref[j] = o + rk * v_ref[j]


def _rwkv_post(yf, yb, r, k, v, pk, pv, fold):
    t_len, _, n_lane = yf.shape
    ks = r.shape[1]
    spec = lambda rows: pl.BlockSpec((SCAN_TB, rows, LANES), lambda c, i: (i, 0, c))
    return pl.pallas_call(
        functools.partial(_rwkv_post_kernel, fold=fold),
        grid=(n_lane // LANES, t_len // SCAN_TB),
        in_specs=[
            spec(HEAD_DIM), spec(HEAD_DIM), spec(ks), spec(ks), spec(HEAD_DIM),
            pl.BlockSpec((ks, LANES), lambda c, i: (0, c)),
            pl.BlockSpec((2, HEAD_DIM, LANES), lambda c, i: (0, 0, c)),
        ],
        out_specs=spec(HEAD_DIM),
        out_shape=jax.ShapeDtypeStruct((t_len, HEAD_DIM, n_lane), F32),
        compiler_params=_cparams(("parallel", "parallel")),
    )(yf, yb, r, k, v, pk, pv)


def _head(h):
    return slice(h * HEAD_DIM, (h + 1) * HEAD_DIM)


def _ctx_attn_kernel(q_ref, k_ref, v_ref, o_ref, ko_ref, vo_ref):
    scale = HEAD_DIM ** -0.5
    for h in range(N_HEAD):
        k = k_ref[:, _head(h)]
        v = v_ref[:, _head(h)]
        ko_ref[0, h] = k.astype(F32)
        vo_ref[0, h] = v.astype(F32)
        s = _dot_nt(q_ref[:, _head(h)], k) * scale
        p = jnp.exp(s - jnp.max(s, axis=-1, keepdims=True))
        p = p / jnp.sum(p, axis=-1, keepdims=True)
        o_ref[:, _head(h)] = _dot(p.astype(BF16), v)


def _ctx_attention(z, n_rows, b, t):
    cb = OFF_NA // D_BRANCH
    zspec = lambda c: pl.BlockSpec((t, D_BRANCH), lambda i: (i, cb + c))
    kv_shape = jax.ShapeDtypeStruct((b, N_HEAD, t, HEAD_DIM), F32)
    kv_spec = pl.BlockSpec((1, N_HEAD, t, HEAD_DIM), lambda i: (i, 0, 0, 0))
    return pl.pallas_call(
        _ctx_attn_kernel,
        grid=(b,),
        in_specs=[zspec(0), zspec(1), zspec(2)],
        out_specs=[pl.BlockSpec((t, D_BRANCH), lambda i: (i, 0)), kv_spec, kv_spec],
        out_shape=[jax.ShapeDtypeStruct((n_rows, D_BRANCH), F32), kv_shape, kv_shape],
        compiler_params=_cparams(("parallel",)),
    )(z, z, z)


def _na_kernel(q_ref, k_ref, v_ref, kc_ref, vc_ref, tab_ref, prev_ref, o_ref, *, rows):
    del prev_ref
    scale = HEAD_DIM ** -0.5
    win = NA_KH * GRID_W
    qcol = lax.broadcasted_iota(jnp.int32, (GRID_W, win), 0)
    kcol = lax.broadcasted_iota(jnp.int32, (GRID_W, win), 1) & (GRID_W - 1)
    c_start = jnp.clip(qcol - NA_KW // 2, 0, GRID_W - NA_KW)
    col_valid = (kcol >= c_start) & (kcol < c_start + NA_KW)
    for h in range(N_HEAD):
        kc = kc_ref[0, 0, h].astype(BF16)
        vc = vc_ref[0, 0, h].astype(BF16)

        def row_block(r, carry):
            rs = jnp.clip(r - NA_KH // 2, 0, rows - NA_KH)
            q_rows = pl.ds(pl.multiple_of(r * GRID_W, GRID_W), GRID_W)
            w_rows = pl.ds(pl.multiple_of(rs * GRID_W, GRID_W), win)
            q = q_ref[q_rows, _head(h)].astype(BF16)
            kw = k_ref[w_rows, _head(h)].astype(BF16)
            vw = v_ref[w_rows, _head(h)].astype(BF16)
            first = rs - r + NA_KH - 1
            bias = jnp.concatenate([tab_ref[h, first + 2 * i] for i in range(NA_KH // 2)], axis=1)
            s_loc = jnp.where(col_valid, _dot_nt(q, kw) * scale + bias, NEG_BIG)
            s_ctx = _dot_nt(q, kc) * scale
            m = jnp.maximum(jnp.max(s_loc, axis=-1, keepdims=True), jnp.max(s_ctx, axis=-1, keepdims=True))
            p_loc = jnp.exp(s_loc - m)
            p_ctx = jnp.exp(s_ctx - m)
            den = jnp.sum(p_loc, axis=-1, keepdims=True) + jnp.sum(p_ctx, axis=-1, keepdims=True)
            o_ref[q_rows, _head(h)] = _dot((p_loc / den).astype(BF16), vw) + _dot((p_ctx / den).astype(BF16), vc)
            return carry

        lax.fori_loop(0, rows, row_block, 0, unroll=2)


def _na_attention(z, o_prev, cache_k, cache_v, bias_tab, layer, row_tile0, b, t):
    past = cache_k.shape[3]
    rows = t // GRID_W
    assert rows >= NA_KH
    cb = OFF_NA // D_BRANCH
    zspec = lambda c: pl.BlockSpec((t, D_BRANCH), lambda i: (row_tile0 + i, cb + c))
    cspec = pl.BlockSpec((1, 1, N_HEAD, past, HEAD_DIM), lambda i: (i, layer, 0, 0, 0))
    return pl.pallas_call(
        functools.partial(_na_kernel, rows=rows),
        grid=(b,),
        in_specs=[zspec(0), zspec(1), zspec(2), cspec, cspec,
                  pl.BlockSpec(bias_tab.shape, lambda i: (0, 0, 0, 0)),
                  pl.BlockSpec(memory_space=pl.ANY)],
        out_specs=pl.BlockSpec((t, D_BRANCH), lambda i: (row_tile0 + i, 0)),
        out_shape=jax.ShapeDtypeStruct(o_prev.shape, F32),
        input_output_aliases={6: 0},
        compiler_params=_cparams(("parallel",)),
    )(z, z, z, cache_k, cache_v, bias_tab, o_prev)


def _na_bias_table(rpb):
    h, ndr, _ = rpb.shape
    edge = GRID_W - NA_KW
    ext = jnp.concatenate([jnp.broadcast_to(rpb[..., :1], (h, ndr, edge)), rpb,
                           jnp.broadcast_to(rpb[..., -1:], (h, ndr, edge + 1))], axis=-1)
    skew = jnp.tile(ext, (1, 1, GRID_W))[..., :GRID_W * (2 * GRID_W - 1)].reshape(h, ndr, GRID_W, 2 * GRID_W - 1)
    toep = skew[..., GRID_W - 1:]
    return jnp.concatenate([toep[:, :-1], toep[:, 1:]], axis=-1)


def _rope_tables(t_len):
    half = HEAD_DIM // 2
    nf = half // 2
    inv = ROPE_BASE ** (-jnp.arange(nf, dtype=F32) / nf)
    t = jnp.arange(t_len)

    def tab(pos):
        ang = pos.astype(F32)[:, None] * inv[None, :]
        cos, sin = jnp.cos(ang), jnp.sin(ang)
        return jnp.concatenate([cos, cos], -1), jnp.concatenate([-sin, sin], -1)

    c_row, s_row = tab(t // GRID_W)
    c_col, s_col = tab(t % GRID_W)
    cos = jnp.tile(jnp.concatenate([c_row, c_col], -1), (1, N_HEAD))
    sin = jnp.tile(jnp.concatenate([s_row, s_col], -1), (1, N_HEAD))
    return cos, sin


def _log_sigmoid(x):
    return -_softplus(-x)


def _ret_heads(q_ref, k_ref, v_ref, g_ref, dec_ref, gn_ref, o_ref, *, t_len, qb, rope_refs=None, s0_ref=None,
               st_ref=None):
    lo = (lax.broadcasted_iota(jnp.int32, (1, LANES), 1) & 31) < 16
    for hp in range(N_HEAD // 2):
        pair = slice(hp * LANES, (hp + 1) * LANES)
        q2 = q_ref[:, pair].astype(F32)
        k2 = k_ref[:, pair].astype(F32)
        if rope_refs is not None:
            cos = rope_refs[0][:, pair]
            sin = rope_refs[1][:, pair]
            rot = lambda x: x * cos + jnp.where(lo, pltpu.roll(x, LANES - 16, 1), pltpu.roll(x, 16, 1)) * sin
            q2, k2 = rot(q2), rot(k2)
        k2 = k2 * (HEAD_DIM ** -0.5)
        for hh in range(2):
            h = 2 * hp + hh
            half = slice(hh * HEAD_DIM, (hh + 1) * HEAD_DIM)
            qh = q2[:, half].astype(BF16)
            k = k2[:, half]
            kb = k.astype(BF16)
            vb = v_ref[:, _head(h)].astype(BF16)
            lgf = _log_sigmoid(dec_ref[0, h])[0:1, :]
            lgb = _log_sigmoid(dec_ref[1, h])[0:1, :]
            lgf_h = lgf[:, 0:HEAD_DIM]
            lgb_h = lgb[:, 0:HEAD_DIM]
            for qi in range(t_len // qb):
                rows = slice(qi * qb, (qi + 1) * qb)
                q = qh[rows]
                s = _dot_nt(q, kb)
                diff = (lax.broadcasted_iota(jnp.int32, (qb, t_len), 0) + qi * qb
                        - lax.broadcasted_iota(jnp.int32, (qb, t_len), 1)).astype(F32)
                dmat = (jnp.where(diff >= 0, jnp.exp(lgf * jnp.maximum(diff, 0.0)), 0.0)
                        + jnp.where(diff <= 0, jnp.exp(lgb * jnp.maximum(-diff, 0.0)), 0.0))
                y = _dot((s * dmat).astype(BF16), vb)
                if s0_ref is not None:
                    pos = (lax.broadcasted_iota(jnp.int32, (qb, HEAD_DIM), 0) + qi * qb).astype(F32)
                    y = y + _dot(q, s0_ref[0, 0, h].astype(BF16)) * jnp.exp(lgf_h * (pos + 1.0))
                    y = y + _dot(q, s0_ref[0, 1, h].astype(BF16)) * jnp.exp(lgb_h * (t_len - pos))
                mu = jnp.mean(y, axis=-1, keepdims=True)
                dlt = y - mu
                var = jnp.mean(dlt * dlt, axis=-1, keepdims=True)
                yn = dlt * lax.rsqrt(var + RET_GN_EPS)
                g = g_ref[rows, _head(h)].astype(F32)
                o_ref[rows, _head(h)] = (yn * gn_ref[:, _head(h)]) * (g * jax.nn.sigmoid(g))
            if st_ref is not None:
                pos = lax.broadcasted_iota(jnp.int32, (t_len, HEAD_DIM), 0).astype(F32)
                kzf = (k * jnp.exp(lgf_h * (t_len - 1.0 - pos))).astype(BF16)
                kzb = (k * jnp.exp(lgb_h * pos)).astype(BF16)
                sf = _dot_tn(kzf, vb)
                sb = _dot_tn(kzb, vb)
                if s0_ref is not None:
                    sf = sf + s0_ref[0, 0, h] * jnp.exp(lgf_h * float(t_len))
                    sb = sb + s0_ref[0, 1, h] * jnp.exp(lgb_h * float(t_len))
                st_ref[0, 0, h] = sf
                st_ref[0, 1, h] = sb


def _ret_ctx_kernel(q_ref, k_ref, v_ref, g_ref, dec_ref, gn_ref, o_ref, st_ref, *, t_len, qb):
    _ret_heads(q_ref, k_ref, v_ref, g_ref, dec_ref, gn_ref, o_ref, t_len=t_len, qb=qb, st_ref=st_ref)


def _ret_lat_kernel(q_ref, k_ref, v_ref, g_ref, dec_ref, gn_ref, cos_ref, sin_ref, s0_ref, prev_ref, o_ref,
                    *, t_len, qb):
    del prev_ref
    _ret_heads(q_ref, k_ref, v_ref, g_ref, dec_ref, gn_ref, o_ref, t_len=t_len, qb=qb,
               rope_refs=(cos_ref, sin_ref), s0_ref=s0_ref.at[0])


def _ret_specs(t, row_tile0):
    cb = OFF_RET // D_BRANCH
    zspec = lambda c: pl.BlockSpec((t, D_BRANCH), lambda i: (row_tile0 + i, cb + c))
    return [zspec(0), zspec(1), zspec(2), zspec(3),
            pl.BlockSpec((2, N_HEAD, 8, t), lambda i: (0, 0, 0, 0)),
            pl.BlockSpec((1, D_BRANCH), lambda i: (0, 0))]


def _retention_ctx(z, dec, gn, n_rows, b, t):
    st_shape = (b, 2, N_HEAD, HEAD_DIM, HEAD_DIM)
    return pl.pallas_call(
        functools.partial(_ret_ctx_kernel, t_len=t, qb=min(t, 256)),
        grid=(b,),
        in_specs=_ret_specs(t, 0),
        out_specs=[pl.BlockSpec((t, D_BRANCH), lambda i: (i, 0)),
                   pl.BlockSpec((1,) + st_shape[1:], lambda i: (i, 0, 0, 0, 0))],
        out_shape=[jax.ShapeDtypeStruct((n_rows, D_BRANCH), F32), jax.ShapeDtypeStruct(st_shape, F32)],
        compiler_params=_cparams(("parallel",)),
    )(z, z, z, z, dec, gn)


def _retention_lat(z, o_prev, dec, gn, cos, sin, s0, layer, row_tile0, b, t):
    tab = pl.BlockSpec((t, D_BRANCH), lambda i: (0, 0))
    return pl.pallas_call(
        functools.partial(_ret_lat_kernel, t_len=t, qb=min(t, 256)),
        grid=(b,),
        in_specs=_ret_specs(t, row_tile0) + [
            tab, tab,
            pl.BlockSpec((1, 1, 2, N_HEAD, HEAD_DIM, HEAD_DIM), lambda i: (i, layer, 0, 0, 0, 0)),
            pl.BlockSpec(memory_space=pl.ANY)],
        out_specs=pl.BlockSpec((t, D_BRANCH), lambda i: (row_tile0 + i, 0)),
        out_shape=jax.ShapeDtypeStruct(o_prev.shape, F32),
        input_output_aliases={9: 0},
        compiler_params=_cparams(("parallel",)),
    )(z, z, z, z, dec, gn, cos, sin, s0, o_prev)


def _mix_out_kernel(x_ref, m_ref, oa_ref, ga_ref, ob_ref, oc_ref, g0_ref, g1_ref, g2_ref,
                    wa_ref, wb_ref, wc_ref, wo_ref, gain_ref, o_ref):
    out_a = _dot((oa_ref[...] * ga_ref[...]).astype(BF16), wa_ref[...])
    out_b = _dot(ob_ref[...].astype(BF16), wb_ref[...])
    out_c = _dot(oc_ref[...].astype(BF16), wc_ref[...])
    sig = lambda ref: jax.nn.sigmoid(ref[...].astype(F32))
    merged = sig(g0_ref) * out_a + sig(g1_ref) * out_b + sig(g2_ref) * out_c
    y = _dot(merged.astype(BF16), wo_ref[...])
    gate = m_ref[0, :, 2 * D_MODEL:3 * D_MODEL]
    o_ref[...] = x_ref[...] + gate * _rms(y, gain_ref[...])


def _mix_out(x, mod, o_rwkv, g_rwkv, o_na, o_ret, z, wa, wb, wc, wo, gain, n_ctx_tiles):
    n = x.shape[0]
    tm = 512
    per = TOKEN_TILE // tm
    midx = functools.partial(_mod_index, n_ctx_tiles=n_ctx_tiles * per, tiles_per_lat=per)
    row = lambda w: pl.BlockSpec((tm, w), lambda i: (i, 0))
    gate_spec = lambda g: pl.BlockSpec((tm, D_MODEL), lambda i: (i, OFF_GATE // D_MODEL + g))
    wspec = lambda a, b: pl.BlockSpec((a, b), lambda i: (0, 0))
    return pl.pallas_call(
        _mix_out_kernel,
        grid=(n // tm,),
        in_specs=[
            row(D_MODEL),
            pl.BlockSpec((1, 1, 6 * D_MODEL), lambda i: (midx(i), 0, 0)),
            row(D_BRANCH), row(D_BRANCH), row(D_BRANCH), row(D_BRANCH),
            gate_spec(0), gate_spec(1), gate_spec(2),
            wspec(D_BRANCH, D_MODEL), wspec(D_BRANCH, D_MODEL), wspec(D_BRANCH, D_MODEL),
            wspec(D_MODEL, D_MODEL), wspec(1, D_MODEL),
        ],
        out_specs=row(D_MODEL),
        out_shape=jax.ShapeDtypeStruct((n, D_MODEL), F32),
        compiler_params=_cparams(("parallel",)),
    )(x, mod, o_rwkv, g_rwkv, o_na, o_ret, z, z, z, wa, wb, wc, wo, gain)


def _gelu_tanh(x):
    return x * (0.5 * (1.0 + jnp.tanh(math.sqrt(2.0 / math.pi) * (x + 0.044715 * (x * x * x)))))


FFN_CHUNK = 256


def _ffn_kernel(x_ref, m_ref, gpre_ref, wup_ref, cw_ref, wd_ref, gpost_ref, o_ref, act_scr,
                *, n_ctx_tiles, t_ctx, t_lat):
    tm = x_ref.shape[0]
    y = _rms(x_ref[...], gpre_ref[...])
    h = (y * (1.0 + m_ref[0, :, 4 * D_MODEL:5 * D_MODEL]) + m_ref[0, :, 3 * D_MODEL:4 * D_MODEL]).astype(BF16)
    first, last = _seq_edges(pl.program_id(0), n_ctx_tiles, t_ctx, t_lat, tm)
    for c in range(D_FF // FFN_CHUNK):
        vcols = slice(c * FFN_CHUNK, (c + 1) * FFN_CHUNK)
        gcols = slice(D_FF + c * FFN_CHUNK, D_FF + (c + 1) * FFN_CHUNK)
        val = _dwconv3(_dot(h, wup_ref[:, vcols]), cw_ref[:, vcols], first, last)
        gate = _dwconv3(_dot(h, wup_ref[:, gcols]), cw_ref[:, gcols], first, last)
        act_scr[:, vcols] = (_gelu_tanh(gate) * val).astype(BF16)
    gate2 = m_ref[0, :, 5 * D_MODEL:6 * D_MODEL]
    for half in range(2):
        rows = slice(half * tm // 2, (half + 1) * tm // 2)
        down = _dot(act_scr[rows, :], wd_ref[...])
        o_ref[rows, :] = x_ref[rows, :] + gate2 * _rms(down, gpost_ref[...])


def _ffn(x, mod, gpre, w_up, conv_w, w_down, gpost, n_ctx_tiles, t_ctx, t_lat):
    n = x.shape[0]
    tm = TOKEN_TILE
    midx = functools.partial(_mod_index, n_ctx_tiles=n_ctx_tiles, tiles_per_lat=1)
    return pl.pallas_call(
        functools.partial(_ffn_kernel, n_ctx_tiles=n_ctx_tiles, t_ctx=t_ctx, t_lat=t_lat),
        grid=(n // tm,),
        in_specs=[
            pl.BlockSpec((tm, D_MODEL), lambda i: (i, 0)),
            pl.BlockSpec((1, 1, 6 * D_MODEL), lambda i: (midx(i), 0, 0)),
            pl.BlockSpec((1, D_MODEL), lambda i: (0, 0)),
            _resident((D_MODEL, 2 * D_FF)),
            _resident((3, 2 * D_FF)),
            _resident((D_FF, D_MODEL)),
            pl.BlockSpec((1, D_MODEL), lambda i: (0, 0)),
        ],
        out_specs=pl.BlockSpec((tm, D_MODEL), lambda i: (i, 0)),
        out_shape=jax.ShapeDtypeStruct((n, D_MODEL), F32),
        scratch_shapes=[pltpu.VMEM((tm, D_FF), BF16)],
        compiler_params=_cparams(("parallel",)),
    )(x, mod, gpre, w_up, conv_w, w_down, gpost)


class _ScanGeom:
    def __init__(self, b):
        self.b = b
        self.pairs = b * N_HEAD
        self.fold = max(1, HALF // self.pairs)
        self.ks = HEAD_DIM // self.fold
        self.pt = HALF // self.fold
        assert self.pairs % self.pt == 0 and self.ks % 8 == 0
        self.tiles = self.pairs // self.pt

    def lanes(self, x, lead):
        return jnp.broadcast_to(x, lead + (self.tiles, 2, self.fold, self.pt)).reshape(lead + (self.tiles * LANES,))

    def key_rows(self, x, t):
        x = x.reshape(self.b, t, N_HEAD, self.fold, self.ks).transpose(1, 4, 3, 0, 2)
        return x.reshape(t, self.ks, self.fold, self.tiles, self.pt).transpose(0, 1, 3, 2, 4)

    def key_operand(self, xs, t):
        if xs.ndim == 2:
            x = self.key_rows(xs, t)[:, :, :, None]
        else:
            x = xs.reshape(2, self.b, t, N_HEAD, self.fold, self.ks).transpose(2, 5, 0, 4, 1, 3)
            x = x.reshape(t, self.ks, 2, self.fold, self.tiles, self.pt).transpose(0, 1, 4, 2, 3, 5)
        return self.lanes(x, (t, self.ks))

    def value_operand(self, x, t):
        x = x.reshape(self.b, t, N_HEAD, HEAD_DIM).transpose(1, 3, 0, 2)
        return self.lanes(x.reshape(t, HEAD_DIM, self.tiles, 1, 1, self.pt), (t, HEAD_DIM))

    def key_param(self, p):
        x = jnp.broadcast_to(p.reshape(1, N_HEAD, self.fold, self.ks), (self.b, N_HEAD, self.fold, self.ks))
        x = x.reshape(self.tiles, self.pt, self.fold, self.ks).transpose(3, 0, 2, 1)
        return self.lanes(x[:, :, None], (self.ks,))

    def value_param(self, p):
        x = jnp.broadcast_to(p.reshape(1, N_HEAD, HEAD_DIM), (self.b, N_HEAD, HEAD_DIM))
        x = x.reshape(self.tiles, self.pt, HEAD_DIM).transpose(2, 0, 1)
        return self.lanes(x[:, :, None, None], (HEAD_DIM,))

    def state_in(self, s0):
        x = s0.reshape(self.b, 2, N_HEAD, HEAD_DIM, self.fold, self.ks).transpose(3, 5, 1, 4, 0, 2)
        x = x.reshape(HEAD_DIM, self.ks, 2, self.fold, self.tiles, self.pt).transpose(0, 1, 4, 2, 3, 5)
        x = x.reshape(HEAD_DIM // 8, 8, self.ks, self.tiles * LANES)
        return x.transpose(0, 2, 1, 3)

    def state_out(self, s):
        x = s.transpose(0, 2, 1, 3)
        x = x.reshape(HEAD_DIM, self.ks, self.tiles, 2, self.fold, self.pt).transpose(0, 1, 3, 4, 2, 5)
        x = x.reshape(HEAD_DIM, self.ks, 2, self.fold, self.b, N_HEAD).transpose(4, 2, 5, 0, 3, 1)
        return x.reshape(self.b, 2, N_HEAD, HEAD_DIM, HEAD_DIM)

    def tokens_out(self, o):
        t = o.shape[0]
        x = o.reshape(t, HEAD_DIM, self.tiles, 2 * self.fold, self.pt)[:, :, :, 0]
        return x.reshape(t, HEAD_DIM, self.b, N_HEAD).transpose(2, 0, 3, 1).reshape(self.b * t, D_BRANCH)


def _rwkv_branch(rkv, dec, alpha, lp, s0, b, t):
    geo = _ScanGeom(b)
    r, k, w, a = (geo.key_operand(x, t) for x in (rkv[0], rkv[1], dec, alpha))
    v = geo.value_operand(rkv[2], t)
    scan_p = jnp.stack([geo.key_param(lp['k_k']), geo.key_param(lp['k_a'])])
    if s0 is None:
        s0_l = jnp.zeros((HEAD_DIM // 8, geo.ks, 8, geo.tiles * LANES), F32)
    else:
        s0_l = geo.state_in(s0)
    yf, yb, s_fin = _wkv_scan(r, k, w, a, v, scan_p, s0_l, geo.fold)
    post_v = jnp.stack([geo.value_param(lp['ln_g']), geo.value_param(lp['ln_b'])])
    o = _rwkv_post(yf, yb, r, k, v, geo.key_param(lp['r_k']), post_v, geo.fold)
    return geo.tokens_out(o), geo.state_out(s_fin)


def kernel(x_prompt, x_sample, cache_na_k, cache_na_v, state_rwkv, state_ret, c, c_ctx, ada_w, ada_b, norm_mix_pre, norm_mix_post, norm_ffn_pre, norm_ffn_post, w_in, rwkv_conv, rwkv_w0, rwkv_w_up, rwkv_a0, rwkv_a_up, rwkv_g_up, rwkv_k_k, rwkv_k_a, rwkv_r_k, rwkv_ln_g, rwkv_ln_b, na_rpb, ret_decay, ret_gn, w_o_rwkv, w_o_na, w_o_ret, w_out, ffn_up, ffn_conv, ffn_down):
    bc, tc, _ = x_prompt.shape
    bl, tl, _ = x_sample.shape
    nc, nl = bc * tc, bl * tl
    assert tl == TOKEN_TILE and TOKEN_TILE % tc == 0 and nc % TOKEN_TILE == 0 and 1 + bl <= MOD_ROWS
    assert tc & (tc - 1) == 0 and tl % GRID_W == 0
    n_ctx_tiles = nc // TOKEN_TILE

    x = jnp.concatenate([x_prompt.reshape(nc, D_MODEL), x_sample.reshape(nl, D_MODEL)], axis=0)
    cvec = jnp.concatenate([c_ctx[None, :], c, jnp.zeros((MOD_ROWS - 1 - bl, D_MODEL), F32)], axis=0)
    mods = _ada_modulation(cvec, ada_w, ada_b).reshape(DEPTH, MOD_ROWS, 1, 6 * D_MODEL)
    rope_cos, rope_sin = _rope_tables(tl)
    row = lambda p: p.reshape(1, -1)

    new_k, new_v, new_rw, new_rt = [], [], [], []
    for l in range(DEPTH):
        mod = mods[l]
        z = _in_proj(x, mod, row(norm_mix_pre[l]), w_in[l].astype(BF16), n_ctx_tiles)

        conv_w = jnp.pad(rwkv_conv[l], ((0, 0), (0, RWKV_PAD - RWKV_COLS)))
        wup_pad = jnp.pad(rwkv_w_up[l], ((0, 0), (0, LANES - W_LORA), (0, 0))).astype(BF16)
        aup_pad = jnp.pad(rwkv_a_up[l], ((0, 0), (W_LORA, LANES - W_LORA - A_LORA), (0, 0))).astype(BF16)
        lora_w = (conv_w[:, 3 * D_BRANCH:3 * D_BRANCH + 2 * LANES], wup_pad, aup_pad, rwkv_g_up[l].astype(BF16),
                  rwkv_w0[l], rwkv_a0[l])
        lp = {'k_k': rwkv_k_k[l], 'k_a': rwkv_k_a[l], 'ln_g': rwkv_ln_g[l], 'ln_b': rwkv_ln_b[l],
              'r_k': rwkv_r_k[l].reshape(-1)}
        n_lat_tiles = nl // TOKEN_TILE
        rkv_c = _rwkv_conv(z, conv_w, 0, n_ctx_tiles, tc)
        dec_c, alpha_c, g_rwkv = _rwkv_lora(z, lora_w, None, 0, n_ctx_tiles, tc)
        rkv_l = _rwkv_conv(z, conv_w, n_ctx_tiles, n_lat_tiles, tl)
        dec_l, alpha_l, g_rwkv = _rwkv_lora(z, lora_w, g_rwkv, n_ctx_tiles, n_lat_tiles, tl)
        o_a_ctx, st_rw = _rwkv_branch(rkv_c, dec_c, alpha_c, lp, None, bc, tc)
        o_a_lat, _ = _rwkv_branch(rkv_l, dec_l, alpha_l, lp, state_rwkv[:, l], bl, tl)
        o_rwkv = jnp.concatenate([o_a_ctx, o_a_lat], axis=0)

        o_na, k_ctx, v_ctx = _ctx_attention(z, nc + nl, bc, tc)
        o_na = _na_attention(z, o_na, cache_na_k, cache_na_v, _na_bias_table(na_rpb[l]), l, n_ctx_tiles, bl, tl)

        dec_ret = jnp.broadcast_to(ret_decay[l][:, :, None, None], (2, N_HEAD, 8, tl))
        gn = row(ret_gn[l])
        o_ret, st_rt = _retention_ctx(z, dec_ret[..., :tc], gn, nc + nl, bc, tc)
        o_ret = _retention_lat(z, o_ret, dec_ret, gn, rope_cos, rope_sin, state_ret, l, n_ctx_tiles, bl, tl)

        x = _mix_out(x, mod, o_rwkv, g_rwkv, o_na, o_ret, z, w_o_rwkv[l].astype(BF16), w_o_na[l].astype(BF16),
                     w_o_ret[l].astype(BF16), w_out[l].astype(BF16), row(norm_mix_post[l]), n_ctx_tiles)
        x = _ffn(x, mod, row(norm_ffn_pre[l]), ffn_up[l].astype(BF16), ffn_conv[l], ffn_down[l].astype(BF16),
                 row(norm_ffn_post[l]), n_ctx_tiles, tc, tl)

        new_k.append(k_ctx)
        new_v.append(v_ctx)
        new_rw.append(st_rw)
        new_rt.append(st_rt)

    return (x[:nc].reshape(bc, tc, D_MODEL), x[nc:].reshape(bl, tl, D_MODEL), jnp.stack(new_k, axis=1),
            jnp.stack(new_v, axis=1), jnp.stack(new_rw, axis=1), jnp.stack(new_rt, axis=1))
```

```python
import functools
import math

import jax
import jax.numpy as jnp
from jax import lax
from jax.experimental import pallas as pl
from jax.experimental.pallas import tpu as pltpu

F32 = jnp.float32
BF16 = jnp.bfloat16

D_MODEL = 1024
DEPTH = 2
N_HEAD = 8
HEAD_DIM = 64
D_BRANCH = N_HEAD * HEAD_DIM
GRID_W = 64
NA_KH = 8
NA_KW = 16
W_LORA = 64
A_LORA = 64
G_LORA = 128
D_FF = 2816
ROPE_BASE = 10000.0
NEG_BIG = -1e9
RWKV_LN_EPS = 64e-5
RET_GN_EPS = 1e-5
RMS_EPS = 1e-6

RWKV_COLS = 3 * D_BRANCH + W_LORA + A_LORA + G_LORA
RWKV_PAD = 2048
OFF_GATE = 0
OFF_RWKV = 3 * D_MODEL
OFF_NA = OFF_RWKV + RWKV_PAD
OFF_RET = OFF_NA + 3 * D_BRANCH
Z_COLS = OFF_RET + 4 * D_BRANCH

LANES = 128
TOKEN_TILE = 1024
MOD_ROWS = 8
VMEM_LIMIT = 56 * 1024 * 1024


def _cparams(sem):
    return pltpu.CompilerParams(dimension_semantics=sem, vmem_limit_bytes=VMEM_LIMIT)


def _rms(x, g):
    return x * lax.rsqrt(jnp.mean(x * x, axis=-1, keepdims=True) + RMS_EPS) * g


def _softplus(x):
    return jnp.maximum(x, 0.0) + jnp.log1p(jnp.exp(-jnp.abs(x)))


def _dot(a, b):
    return jnp.dot(a, b, preferred_element_type=F32)


def _dot_nt(a, b):
    return lax.dot_general(a, b, (((1,), (1,)), ((), ())), preferred_element_type=F32)


def _dot_tn(a, b):
    return lax.dot_general(a, b, (((0,), (0,)), ((), ())), preferred_element_type=F32)


def _mod_index(i, n_ctx_tiles, tiles_per_lat):
    return jnp.where(i < n_ctx_tiles, 0, 1 + jnp.maximum(i - n_ctx_tiles, 0) // tiles_per_lat)


def _seq_edges(i, n_ctx_tiles, t_ctx, t_lat, tm):
    seqlen = jnp.where(i < n_ctx_tiles, t_ctx, t_lat)
    pos = lax.broadcasted_iota(jnp.int32, (tm, 1), 0) & (seqlen - 1)
    return pos == 0, pos == seqlen - 1


def _dwconv3(u, w, first, last):
    tm = u.shape[0]
    prev = jnp.where(first, 0.0, pltpu.roll(u, 1, 0))
    nxt = jnp.where(last, 0.0, pltpu.roll(u, tm - 1, 0))
    return prev * w[0:1] + u * w[1:2] + nxt * w[2:3]


def _ada_kernel(c_ref, w_ref, b_ref, o_ref):
    c = c_ref[...]
    s = (c * jax.nn.sigmoid(c)).astype(BF16)
    o_ref[0] = _dot(s, w_ref[0].astype(BF16)) + b_ref[0]


def _ada_modulation(cvec, ada_w, ada_b):
    tn = 512
    n_out = 6 * D_MODEL
    return pl.pallas_call(
        _ada_kernel,
        grid=(DEPTH, n_out // tn),
        in_specs=[
            pl.BlockSpec((MOD_ROWS, D_MODEL), lambda l, j: (0, 0)),
            pl.BlockSpec((1, D_MODEL, tn), lambda l, j: (l, 0, j)),
            pl.BlockSpec((1, 1, tn), lambda l, j: (l, 0, j)),
        ],
        out_specs=pl.BlockSpec((1, MOD_ROWS, tn), lambda l, j: (l, 0, j)),
        out_shape=jax.ShapeDtypeStruct((DEPTH, MOD_ROWS, n_out), F32),
        compiler_params=_cparams(("parallel", "parallel")),
    )(cvec, ada_w, ada_b.reshape(DEPTH, 1, n_out))


def _resident(shape):
    return pl.BlockSpec(shape, lambda *_: (0,) * len(shape), pipeline_mode=pl.Buffered(1))


_W_RWKV, _W_NA, _W_RET, _W_GATE = 0, RWKV_COLS, RWKV_COLS + 3 * D_BRANCH, RWKV_COLS + 7 * D_BRANCH
IN_PROJ_GROUPS = ((_W_GATE, OFF_GATE, 3 * D_MODEL), (_W_RWKV, OFF_RWKV, RWKV_COLS),
                  (_W_NA, OFF_NA, 3 * D_BRANCH), (_W_RET, OFF_RET, 4 * D_BRANCH))
IN_COLS = _W_GATE + 3 * D_MODEL


def _in_proj_kernel(x_ref, m_ref, g_ref, w_ref, o_ref, *, tn):
    y = _rms(x_ref[...], g_ref[...])
    h = (y * (1.0 + m_ref[0, :, D_MODEL:2 * D_MODEL]) + m_ref[0, :, 0:D_MODEL]).astype(BF16)
    for src, dst, width in IN_PROJ_GROUPS:
        for c in range(0, width, tn):
            n = min(tn, width - c)
            o_ref[:, dst + c:dst + c + n] = _dot(h, w_ref[:, src + c:src + c + n]).astype(o_ref.dtype)
    pad = slice(OFF_RWKV + RWKV_COLS, OFF_RWKV + RWKV_PAD)
    o_ref[:, pad] = jnp.zeros((o_ref.shape[0], RWKV_PAD - RWKV_COLS), o_ref.dtype)


def _in_proj(x, mod, gain, w_bf16, n_ctx_tiles):
    n = x.shape[0]
    tm, tn = 512, 512
    per = TOKEN_TILE // tm
    midx = functools.partial(_mod_index, n_ctx_tiles=n_ctx_tiles * per, tiles_per_lat=per)
    return pl.pallas_call(
        functools.partial(_in_proj_kernel, tn=tn),
        grid=(n // tm,),
        in_specs=[
            pl.BlockSpec((tm, D_MODEL), lambda i: (i, 0)),
            pl.BlockSpec((1, 1, 6 * D_MODEL), lambda i: (midx(i), 0, 0)),
            pl.BlockSpec((1, D_MODEL), lambda i: (0, 0)),
            _resident((D_MODEL, IN_COLS)),
        ],
        out_specs=pl.BlockSpec((tm, Z_COLS), lambda i: (i, 0)),
        out_shape=jax.ShapeDtypeStruct((n, Z_COLS), BF16),
        compiler_params=_cparams(("parallel",)),
    )(x, mod, gain, w_bf16)


def _stream_edges(t_seq, tm):
    pos = lax.broadcasted_iota(jnp.int32, (tm, 1), 0) & (t_seq - 1)
    return pos == 0, pos == t_seq - 1


def _rwkv_conv_kernel(z_ref, w_ref, o_ref, *, t_seq):
    first, last = _stream_edges(t_seq, z_ref.shape[0])
    o_ref[0] = _dwconv3(z_ref[...].astype(F32), w_ref[...], first, last)


def _rwkv_conv(z, conv_w, row_tile0, n_tiles, t_seq):
    tm, tn = TOKEN_TILE, D_BRANCH
    cb = OFF_RWKV // tn
    return pl.pallas_call(
        functools.partial(_rwkv_conv_kernel, t_seq=t_seq),
        grid=(n_tiles, 3),
        in_specs=[
            pl.BlockSpec((tm, tn), lambda i, j: (row_tile0 + i, cb + j)),
            pl.BlockSpec((3, tn), lambda i, j: (0, j)),
        ],
        out_specs=pl.BlockSpec((1, tm, tn), lambda i, j: (j, i, 0)),
        out_shape=jax.ShapeDtypeStruct((3, n_tiles * tm, tn), F32),
        compiler_params=_cparams(("parallel", "parallel")),
    )(z, conv_w)


def _rwkv_lora_kernel(z_ref, cw_ref, wup_ref, aup_ref, gup_ref, w0_ref, a0_ref, *rest, t_seq):
    dec_ref, al_ref, g_ref = rest[-3:]
    first, last = _stream_edges(t_seq, z_ref.shape[0])
    u = _dwconv3(z_ref[...].astype(F32), cw_ref[...], first, last)
    wa = u[:, 0:LANES]
    tw = jnp.tanh(wa).astype(BF16)
    ab = wa.astype(BF16)
    sg = jax.nn.sigmoid(u[:, LANES:2 * LANES]).astype(BF16)
    for d in range(2):
        x = w0_ref[d:d + 1, :] + _dot(tw, wup_ref[d])
        w_log = -_softplus(-x) - 0.5
        dec_ref[d] = jnp.exp(-jnp.exp(w_log))
        al_ref[d] = jax.nn.sigmoid(a0_ref[d:d + 1, :] + _dot(ab, aup_ref[d]))
    g_ref[...] = _dot(sg, gup_ref[...])


def _rwkv_lora(z, lora_w, g_prev, row_tile0, n_tiles, t_seq):
    n = z.shape[0]
    tm = TOKEN_TILE
    wl = 2 * LANES
    cb = (OFF_RWKV + 3 * D_BRANCH) // wl
    out2 = jax.ShapeDtypeStruct((2, n_tiles * tm, D_BRANCH), F32)
    ospec2 = pl.BlockSpec((2, tm, D_BRANCH), lambda i: (0, i, 0))
    full2 = lambda shape: pl.BlockSpec(shape, lambda i: (0, 0))
    full3 = lambda shape: pl.BlockSpec(shape, lambda i: (0, 0, 0))
    in_specs = [
        pl.BlockSpec((tm, wl), lambda i: (row_tile0 + i, cb)),
        full2((3, wl)),
        full3((2, LANES, D_BRANCH)),
        full3((2, LANES, D_BRANCH)),
        full2((G_LORA, D_BRANCH)),
        full2((2, D_BRANCH)),
        full2((2, D_BRANCH)),
    ]
    args = (z,) + tuple(lora_w)
    aliases = {}
    if g_prev is not None:
        in_specs.append(pl.BlockSpec(memory_space=pl.ANY))
        args += (g_prev,)
        aliases = {len(args) - 1: 2}
    return pl.pallas_call(
        functools.partial(_rwkv_lora_kernel, t_seq=t_seq),
        grid=(n_tiles,),
        in_specs=in_specs,
        out_specs=[ospec2, ospec2, pl.BlockSpec((tm, D_BRANCH), lambda i: (row_tile0 + i, 0))],
        out_shape=[out2, out2, jax.ShapeDtypeStruct((n, D_BRANCH), F32)],
        input_output_aliases=aliases,
        compiler_params=_cparams(("parallel",)),
    )(*args)


def _fold_sum(x, lane, fold):
    s = LANES // fold
    while s < LANES:
        x = x + jnp.where((lane & s) != 0, pltpu.roll(x, s, 1), pltpu.roll(x, LANES - s, 1))
        s *= 2
    return x


SCAN_TB = 8


GROUPS_PER_PASS = 4


def _wkv_kernel(rf_ref, rb_ref, kf_ref, kb_ref, wf_ref, wb_ref, af_ref, ab_ref, vf_ref, vb_ref, p_ref, s0_ref,
                yf_ref, yb_ref, st_ref, s_scr, op_scr, *, fold, unroll):
    tb = SCAN_TB
    n_grp, ks = s_scr.shape[1], s_scr.shape[2]
    t = pl.program_id(1)

    @pl.when(t == 0)
    def _():
        s_scr[...] = s0_ref[...]

    lane = lax.broadcasted_iota(jnp.int32, (1, LANES), 1)
    k_k = p_ref[0]
    k_a = p_ref[1]
    dirs = ((rf_ref, kf_ref, wf_ref, af_ref, vf_ref, yf_ref), (rb_ref, kb_ref, wb_ref, ab_ref, vb_ref, yb_ref))
    row_of = lambda d, j: j if d == 0 else tb - 1 - j

    for d, (r_ref, k_ref, w_ref, a_ref, _, _) in enumerate(dirs):
        for j in range(tb):
            jj = row_of(d, j)
            k = k_ref[jj]
            a = a_ref[jj]
            kk = k * k_k
            kk = kk / jnp.maximum(jnp.sqrt(_fold_sum(jnp.sum(kk * kk, axis=0, keepdims=True), lane, fold)), 1e-12)
            op_scr[d, j, 0] = -kk
            op_scr[d, j, 1] = w_ref[jj]
            op_scr[d, j, 2] = kk * a
            op_scr[d, j, 3] = k * (1.0 + (a - 1.0) * k_a)
            op_scr[d, j, 4] = r_ref[jj]

    zeros = lambda n: tuple(jnp.zeros((8, LANES), F32) for _ in range(n))
    op_row = lambda d, j, i, k: op_scr[d, j, i, pl.ds(k, 1), :]
    rows = lambda g: slice(g * 8, (g + 1) * 8)

    def step(j, carry):
        def sa_acc(k, acc):
            out = []
            for d in range(2):
                a = op_row(d, j, 0, k)
                out += [acc[d * n_grp + g] + s_scr[d, g, k] * a for g in range(n_grp)]
            return tuple(out)

        sa = lax.fori_loop(0, ks, sa_acc, zeros(2 * n_grp), unroll=unroll)
        sa = tuple(_fold_sum(x, lane, fold) for x in sa)

        for first in range(0, n_grp, GROUPS_PER_PASS):
            groups = range(first, first + GROUPS_PER_PASS)
            vs = [dirs[d][4][row_of(d, j), rows(g), :] for d in range(2) for g in groups]

            def update(k, acc):
                out = []
                for d in range(2):
                    w, b, kd, r = (op_row(d, j, i, k) for i in (1, 2, 3, 4))
                    for i, g in enumerate(groups):
                        c = d * GROUPS_PER_PASS + i
                        sn = s_scr[d, g, k] * w + sa[d * n_grp + g] * b + vs[c] * kd
                        s_scr[d, g, k] = sn
                        out.append(acc[c] + sn * r)
                return tuple(out)

            ys = lax.fori_loop(0, ks, update, zeros(2 * GROUPS_PER_PASS), unroll=unroll)
            for d in range(2):
                for i, g in enumerate(groups):
                    dirs[d][5][row_of(d, j), rows(g), :] = ys[d * GROUPS_PER_PASS + i]
        return carry

    lax.fori_loop(0, tb, step, 0)

    @pl.when(t == pl.num_programs(1) - 1)
    def _():
        st_ref[...] = s_scr[...]


def _wkv_scan(r, k, w, a, v, params, s0, fold):
    t_len, ks, n_lane = r.shape
    tb = SCAN_TB
    nt = t_len // tb
    n_grp = HEAD_DIM // 8
    fwd = lambda rows: pl.BlockSpec((tb, rows, LANES), lambda c, t: (t, 0, c))
    bwd = lambda rows: pl.BlockSpec((tb, rows, LANES), lambda c, t: (nt - 1 - t, 0, c))
    fwd_d = lambda rows: pl.BlockSpec((None, tb, rows, LANES), lambda c, t: (0, t, 0, c))
    bwd_d = lambda rows: pl.BlockSpec((None, tb, rows, LANES), lambda c, t: (1, nt - 1 - t, 0, c))
    state = pl.BlockSpec((2, n_grp, ks, 8, LANES), lambda c, t: (0, 0, 0, 0, c))
    return pl.pallas_call(
        functools.partial(_wkv_kernel, fold=fold, unroll=4),
        grid=(n_lane // LANES, nt),
        in_specs=[fwd(ks), bwd(ks)] * 2 + [fwd_d(ks), bwd_d(ks)] * 2 + [
            fwd(HEAD_DIM), bwd(HEAD_DIM), pl.BlockSpec((2, ks, LANES), lambda c, t: (0, 0, c)), state],
        out_specs=[fwd(HEAD_DIM), bwd(HEAD_DIM), state],
        out_shape=[jax.ShapeDtypeStruct((t_len, HEAD_DIM, n_lane), F32)] * 2
        + [jax.ShapeDtypeStruct((2, n_grp, ks, 8, n_lane), F32)],
        scratch_shapes=[pltpu.VMEM((2, n_grp, ks, 8, LANES), F32), pltpu.VMEM((2, tb, 5, ks, LANES), F32)],
        compiler_params=_cparams(("parallel", "arbitrary")),
    )(r, r, k, k, w, w, a, a, v, v, params, s0)


def _rwkv_post_kernel(yf_ref, yb_ref, r_ref, k_ref, v_ref, pk_ref, pv_ref, o_ref, *, fold):
    lane = lax.broadcasted_iota(jnp.int32, (1, LANES), 1)
    ln_g = pv_ref[0]
    ln_b = pv_ref[1]
    r_k = pk_ref[...]
    for j in range(SCAN_TB):
        y = _fold_sum(yf_ref[j] + yb_ref[j], lane, fold)
        mu = jnp.mean(y, axis=0, keepdims=True)
        dlt = y - mu
        var = jnp.mean(dlt * dlt, axis=0, keepdims=True)
        o = dlt * lax.rsqrt(var + RWKV_LN_EPS) * ln_g + ln_b
        rk = _fold_sum(jnp.sum(r_ref[j] * k_ref[j] * r_k, axis=0, keepdims=True), lane, fold)
        o_ref[j] = o + rk * v_ref[j]


def _rwkv_post(yf, yb, r, k, v, pk, pv, fold):
    t_len, _, n_lane = yf.shape
    ks = r.shape[1]
    spec = lambda rows: pl.BlockSpec((SCAN_TB, rows, LANES), lambda c, i: (i, 0, c))
    return pl.pallas_call(
        functools.partial(_rwkv_post_kernel, fold=fold),
        grid=(n_lane // LANES, t_len // SCAN_TB),
        in_specs=[
            spec(HEAD_DIM), spec(HEAD_DIM), spec(ks), spec(ks), spec(HEAD_DIM),
            pl.BlockSpec((ks, LANES), lambda c, i: (0, c)),
            pl.BlockSpec((2, HEAD_DIM, LANES), lambda c, i: (0, 0, c)),
        ],
        out_specs=spec(HEAD_DIM),
        out_shape=jax.ShapeDtypeStruct((t_len, HEAD_DIM, n_lane), F32),
        compiler_params=_cparams(("parallel", "parallel")),
    )(yf, yb, r, k, v, pk, pv)


def _head(h):
    return slice(h * HEAD_DIM, (h + 1) * HEAD_DIM)


def _ctx_attn_kernel(q_ref, k_ref, v_ref, o_ref, ko_ref, vo_ref):
    scale = HEAD_DIM ** -0.5
    for h in range(N_HEAD):
        k = k_ref[:, _head(h)]
        v = v_ref[:, _head(h)]
        ko_ref[0, h] = k.astype(F32)
        vo_ref[0, h] = v.astype(F32)
        s = _dot_nt(q_ref[:, _head(h)], k) * scale
        p = jnp.exp(s - jnp.max(s, axis=-1, keepdims=True))
        p = p / jnp.sum(p, axis=-1, keepdims=True)
        o_ref[:, _head(h)] = _dot(p.astype(BF16), v)


def _ctx_attention(z, n_rows, b, t):
    cb = OFF_NA // D_BRANCH
    zspec = lambda c: pl.BlockSpec((t, D_BRANCH), lambda i: (i, cb + c))
    kv_shape = jax.ShapeDtypeStruct((b, N_HEAD, t, HEAD_DIM), F32)
    kv_spec = pl.BlockSpec((1, N_HEAD, t, HEAD_DIM), lambda i: (i, 0, 0, 0))
    return pl.pallas_call(
        _ctx_attn_kernel,
        grid=(b,),
        in_specs=[zspec(0), zspec(1), zspec(2)],
        out_specs=[pl.BlockSpec((t, D_BRANCH), lambda i: (i, 0)), kv_spec, kv_spec],
        out_shape=[jax.ShapeDtypeStruct((n_rows, D_BRANCH), F32), kv_shape, kv_shape],
        compiler_params=_cparams(("parallel",)),
    )(z, z, z)


def _na_kernel(q_ref, k_ref, v_ref, kc_ref, vc_ref, tab_ref, prev_ref, o_ref, *, rows):
    del prev_ref
    scale = HEAD_DIM ** -0.5
    win = NA_KH * GRID_W
    qcol = lax.broadcasted_iota(jnp.int32, (GRID_W, win), 0)
    kcol = lax.broadcasted_iota(jnp.int32, (GRID_W, win), 1) & (GRID_W - 1)
    c_start = jnp.clip(qcol - NA_KW // 2, 0, GRID_W - NA_KW)
    col_valid = (kcol >= c_start) & (kcol < c_start + NA_KW)
    def row_block(r, carry):
        rs = jnp.clip(r - NA_KH // 2, 0, rows - NA_KH)
        q_rows = pl.ds(pl.multiple_of(r * GRID_W, GRID_W), GRID_W)
        w_rows = pl.ds(pl.multiple_of(rs * GRID_W, GRID_W), win)
        first = rs - r + NA_KH - 1
        for h in range(N_HEAD):
            q = q_ref[q_rows, _head(h)].astype(BF16)
            kw = k_ref[w_rows, _head(h)].astype(BF16)
            vw = v_ref[w_rows, _head(h)].astype(BF16)
            kc = kc_ref[0, 0, h].astype(BF16)
            vc = vc_ref[0, 0, h].astype(BF16)
            bias = jnp.concatenate([tab_ref[h, first + 2 * i] for i in range(NA_KH // 2)], axis=1)
            s_loc = jnp.where(col_valid, _dot_nt(q, kw) * scale + bias, NEG_BIG)
            s_ctx = _dot_nt(q, kc) * scale
            m = jnp.maximum(jnp.max(s_loc, axis=-1, keepdims=True), jnp.max(s_ctx, axis=-1, keepdims=True))
            p_loc = jnp.exp(s_loc - m)
            p_ctx = jnp.exp(s_ctx - m)
            den = jnp.sum(p_loc, axis=-1, keepdims=True) + jnp.sum(p_ctx, axis=-1, keepdims=True)
            o_ref[q_rows, _head(h)] = _dot((p_loc / den).astype(BF16), vw) + _dot((p_ctx / den).astype(BF16), vc)
        return carry

    lax.fori_loop(0, rows, row_block, 0)


def _na_attention(z, o_prev, cache_k, cache_v, bias_tab, layer, row_tile0, b, t):
    past = cache_k.shape[3]
    rows = t // GRID_W
    assert rows >= NA_KH
    cb = OFF_NA // D_BRANCH
    zspec = lambda c: pl.BlockSpec((t, D_BRANCH), lambda i: (row_tile0 + i, cb + c))
    cspec = pl.BlockSpec((1, 1, N_HEAD, past, HEAD_DIM), lambda i: (i, layer, 0, 0, 0))
    return pl.pallas_call(
        functools.partial(_na_kernel, rows=rows),
        grid=(b,),
        in_specs=[zspec(0), zspec(1), zspec(2), cspec, cspec,
                  pl.BlockSpec(bias_tab.shape, lambda i: (0, 0, 0, 0)),
                  pl.BlockSpec(memory_space=pl.ANY)],
        out_specs=pl.BlockSpec((t, D_BRANCH), lambda i: (row_tile0 + i, 0)),
        out_shape=jax.ShapeDtypeStruct(o_prev.shape, F32),
        input_output_aliases={6: 0},
        compiler_params=_cparams(("parallel",)),
    )(z, z, z, cache_k, cache_v, bias_tab, o_prev)


def _na_bias_table(rpb):
    h, ndr, _ = rpb.shape
    edge = GRID_W - NA_KW
    ext = jnp.concatenate([jnp.broadcast_to(rpb[..., :1], (h, ndr, edge)), rpb,
                           jnp.broadcast_to(rpb[..., -1:], (h, ndr, edge + 1))], axis=-1)
    skew = jnp.tile(ext, (1, 1, GRID_W))[..., :GRID_W * (2 * GRID_W - 1)].reshape(h, ndr, GRID_W, 2 * GRID_W - 1)
    toep = skew[..., GRID_W - 1:]
    return jnp.concatenate([toep[:, :-1], toep[:, 1:]], axis=-1)


def _rope_tables(t_len):
    half = HEAD_DIM // 2
    nf = half // 2
    inv = ROPE_BASE ** (-jnp.arange(nf, dtype=F32) / nf)
    t = jnp.arange(t_len)

    def tab(pos):
        ang = pos.astype(F32)[:, None] * inv[None, :]
        cos, sin = jnp.cos(ang), jnp.sin(ang)
        return jnp.concatenate([cos, cos], -1), jnp.concatenate([-sin, sin], -1)

    c_row, s_row = tab(t // GRID_W)
    c_col, s_col = tab(t % GRID_W)
    cos = jnp.tile(jnp.concatenate([c_row, c_col], -1), (1, N_HEAD))
    sin = jnp.tile(jnp.concatenate([s_row, s_col], -1), (1, N_HEAD))
    return cos, sin


def _log_sigmoid(x):
    return -_softplus(-x)


def _ret_heads(q_ref, k_ref, v_ref, g_ref, dec_ref, gn_ref, o_ref, *, t_len, qb, rope_refs=None, s0_ref=None,
               st_ref=None):
    lo = (lax.broadcasted_iota(jnp.int32, (1, LANES), 1) & 31) < 16
    for hp in range(N_HEAD // 2):
        pair = slice(hp * LANES, (hp + 1) * LANES)
        q2 = q_ref[:, pair].astype(F32)
        k2 = k_ref[:, pair].astype(F32)
        if rope_refs is not None:
            cos = rope_refs[0][:, pair]
            sin = rope_refs[1][:, pair]
            rot = lambda x: x * cos + jnp.where(lo, pltpu.roll(x, LANES - 16, 1), pltpu.roll(x, 16, 1)) * sin
            q2, k2 = rot(q2), rot(k2)
        k2 = k2 * (HEAD_DIM ** -0.5)
        for hh in range(2):
            h = 2 * hp + hh
            half = slice(hh * HEAD_DIM, (hh + 1) * HEAD_DIM)
            qh = q2[:, half].astype(BF16)
            k = k2[:, half]
            kb = k.astype(BF16)
            vb = v_ref[:, _head(h)].astype(BF16)
            lgf = _log_sigmoid(dec_ref[0, h])[0:1, :]
            lgb = _log_sigmoid(dec_ref[1, h])[0:1, :]
            lgf_h = lgf[:, 0:HEAD_DIM]
            lgb_h = lgb[:, 0:HEAD_DIM]
            for qi in range(t_len // qb):
                rows = slice(qi * qb, (qi + 1) * qb)
                q = qh[rows]
                s = _dot_nt(q, kb)
                diff = (lax.broadcasted_iota(jnp.int32, (qb, t_len), 0) + qi * qb
                        - lax.broadcasted_iota(jnp.int32, (qb, t_len), 1)).astype(F32)
                dmat = (jnp.where(diff >= 0, jnp.exp(lgf * jnp.maximum(diff, 0.0)), 0.0)
                        + jnp.where(diff <= 0, jnp.exp(lgb * jnp.maximum(-diff, 0.0)), 0.0))
                y = _dot((s * dmat).astype(BF16), vb)
                if s0_ref is not None:
                    pos = (lax.broadcasted_iota(jnp.int32, (qb, HEAD_DIM), 0) + qi * qb).astype(F32)
                    y = y + _dot(q, s0_ref[0, 0, h].astype(BF16)) * jnp.exp(lgf_h * (pos + 1.0))
                    y = y + _dot(q, s0_ref[0, 1, h].astype(BF16)) * jnp.exp(lgb_h * (t_len - pos))
                mu = jnp.mean(y, axis=-1, keepdims=True)
                dlt = y - mu
                var = jnp.mean(dlt * dlt, axis=-1, keepdims=True)
                yn = dlt * lax.rsqrt(var + RET_GN_EPS)
                g = g_ref[rows, _head(h)].astype(F32)
                o_ref[rows, _head(h)] = (yn * gn_ref[:, _head(h)]) * (g * jax.nn.sigmoid(g))
            if st_ref is not None:
                pos = lax.broadcasted_iota(jnp.int32, (t_len, HEAD_DIM), 0).astype(F32)
                kzf = (k * jnp.exp(lgf_h * (t_len - 1.0 - pos))).astype(BF16)
                kzb = (k * jnp.exp(lgb_h * pos)).astype(BF16)
                sf = _dot_tn(kzf, vb)
                sb = _dot_tn(kzb, vb)
                if s0_ref is not None:
                    sf = sf + s0_ref[0, 0, h] * jnp.exp(lgf_h * float(t_len))
                    sb = sb + s0_ref[0, 1, h] * jnp.exp(lgb_h * float(t_len))
                st_ref[0, 0, h] = sf
                st_ref[0, 1, h] = sb


def _ret_ctx_kernel(q_ref, k_ref, v_ref, g_ref, dec_ref, gn_ref, o_ref, st_ref, *, t_len, qb):
    _ret_heads(q_ref, k_ref, v_ref, g_ref, dec_ref, gn_ref, o_ref, t_len=t_len, qb=qb, st_ref=st_ref)


def _ret_lat_kernel(q_ref, k_ref, v_ref, g_ref, dec_ref, gn_ref, cos_ref, sin_ref, s0_ref, prev_ref, o_ref,
                    *, t_len, qb):
    del prev_ref
    _ret_heads(q_ref, k_ref, v_ref, g_ref, dec_ref, gn_ref, o_ref, t_len=t_len, qb=qb,
               rope_refs=(cos_ref, sin_ref), s0_ref=s0_ref.at[0])


def _ret_specs(t, row_tile0):
    cb = OFF_RET // D_BRANCH
    zspec = lambda c: pl.BlockSpec((t, D_BRANCH), lambda i: (row_tile0 + i, cb + c))
    return [zspec(0), zspec(1), zspec(2), zspec(3),
            pl.BlockSpec((2, N_HEAD, 8, t), lambda i: (0, 0, 0, 0)),
            pl.BlockSpec((1, D_BRANCH), lambda i: (0, 0))]


def _retention_ctx(z, dec, gn, n_rows, b, t):
    st_shape = (b, 2, N_HEAD, HEAD_DIM, HEAD_DIM)
    return pl.pallas_call(
        functools.partial(_ret_ctx_kernel, t_len=t, qb=min(t, 256)),
        grid=(b,),
        in_specs=_ret_specs(t, 0),
        out_specs=[pl.BlockSpec((t, D_BRANCH), lambda i: (i, 0)),
                   pl.BlockSpec((1,) + st_shape[1:], lambda i: (i, 0, 0, 0, 0))],
        out_shape=[jax.ShapeDtypeStruct((n_rows, D_BRANCH), F32), jax.ShapeDtypeStruct(st_shape, F32)],
        compiler_params=_cparams(("parallel",)),
    )(z, z, z, z, dec, gn)


def _retention_lat(z, o_prev, dec, gn, cos, sin, s0, layer, row_tile0, b, t):
    tab = pl.BlockSpec((t, D_BRANCH), lambda i: (0, 0))
    return pl.pallas_call(
        functools.partial(_ret_lat_kernel, t_len=t, qb=min(t, 256)),
        grid=(b,),
        in_specs=_ret_specs(t, row_tile0) + [
            tab, tab,
            pl.BlockSpec((1, 1, 2, N_HEAD, HEAD_DIM, HEAD_DIM), lambda i: (i, layer, 0, 0, 0, 0)),
            pl.BlockSpec(memory_space=pl.ANY)],
        out_specs=pl.BlockSpec((t, D_BRANCH), lambda i: (row_tile0 + i, 0)),
        out_shape=jax.ShapeDtypeStruct(o_prev.shape, F32),
        input_output_aliases={9: 0},
        compiler_params=_cparams(("parallel",)),
    )(z, z, z, z, dec, gn, cos, sin, s0, o_prev)


def _mix_out_kernel(x_ref, m_ref, oa_ref, ga_ref, ob_ref, oc_ref, g0_ref, g1_ref, g2_ref,
                    wa_ref, wb_ref, wc_ref, wo_ref, gain_ref, o_ref):
    out_a = _dot((oa_ref[...] * ga_ref[...]).astype(BF16), wa_ref[...])
    out_b = _dot(ob_ref[...].astype(BF16), wb_ref[...])
    out_c = _dot(oc_ref[...].astype(BF16), wc_ref[...])
    sig = lambda ref: jax.nn.sigmoid(ref[...].astype(F32))
    merged = sig(g0_ref) * out_a + sig(g1_ref) * out_b + sig(g2_ref) * out_c
    y = _dot(merged.astype(BF16), wo_ref[...])
    gate = m_ref[0, :, 2 * D_MODEL:3 * D_MODEL]
    o_ref[...] = x_ref[...] + gate * _rms(y, gain_ref[...])


def _mix_out(x, mod, o_rwkv, g_rwkv, o_na, o_ret, z, wa, wb, wc, wo, gain, n_ctx_tiles):
    n = x.shape[0]
    tm = 512
    per = TOKEN_TILE // tm
    midx = functools.partial(_mod_index, n_ctx_tiles=n_ctx_tiles * per, tiles_per_lat=per)
    row = lambda w: pl.BlockSpec((tm, w), lambda i: (i, 0))
    gate_spec = lambda g: pl.BlockSpec((tm, D_MODEL), lambda i: (i, OFF_GATE // D_MODEL + g))
    wspec = lambda a, b: pl.BlockSpec((a, b), lambda i: (0, 0))
    return pl.pallas_call(
        _mix_out_kernel,
        grid=(n // tm,),
        in_specs=[
            row(D_MODEL),
            pl.BlockSpec((1, 1, 6 * D_MODEL), lambda i: (midx(i), 0, 0)),
            row(D_BRANCH), row(D_BRANCH), row(D_BRANCH), row(D_BRANCH),
            gate_spec(0), gate_spec(1), gate_spec(2),
            wspec(D_BRANCH, D_MODEL), wspec(D_BRANCH, D_MODEL), wspec(D_BRANCH, D_MODEL),
            wspec(D_MODEL, D_MODEL), wspec(1, D_MODEL),
        ],
        out_specs=row(D_MODEL),
        out_shape=jax.ShapeDtypeStruct((n, D_MODEL), F32),
        compiler_params=_cparams(("parallel",)),
    )(x, mod, o_rwkv, g_rwkv, o_na, o_ret, z, z, z, wa, wb, wc, wo, gain)


def _gelu_tanh(x):
    return x * (0.5 * (1.0 + jnp.tanh(math.sqrt(2.0 / math.pi) * (x + 0.044715 * (x * x * x)))))


FFN_CHUNK = 256


def _ffn_kernel(x_ref, m_ref, gpre_ref, wup_ref, cw_ref, wd_ref, gpost_ref, o_ref, act_scr,
                *, n_ctx_tiles, t_ctx, t_lat):
    tm = x_ref.shape[0]
    y = _rms(x_ref[...], gpre_ref[...])
    h = (y * (1.0 + m_ref[0, :, 4 * D_MODEL:5 * D_MODEL]) + m_ref[0, :, 3 * D_MODEL:4 * D_MODEL]).astype(BF16)
    first, last = _seq_edges(pl.program_id(0), n_ctx_tiles, t_ctx, t_lat, tm)
    for c in range(D_FF // FFN_CHUNK):
        vcols = slice(c * FFN_CHUNK, (c + 1) * FFN_CHUNK)
        gcols = slice(D_FF + c * FFN_CHUNK, D_FF + (c + 1) * FFN_CHUNK)
        val = _dwconv3(_dot(h, wup_ref[:, vcols]), cw_ref[:, vcols], first, last)
        gate = _dwconv3(_dot(h, wup_ref[:, gcols]), cw_ref[:, gcols], first, last)
        act_scr[:, vcols] = (_gelu_tanh(gate) * val).astype(BF16)
    gate2 = m_ref[0, :, 5 * D_MODEL:6 * D_MODEL]
    for half in range(2):
        rows = slice(half * tm // 2, (half + 1) * tm // 2)
        down = _dot(act_scr[rows, :], wd_ref[...])
        o_ref[rows, :] = x_ref[rows, :] + gate2 * _rms(down, gpost_ref[...])


def _ffn(x, mod, gpre, w_up, conv_w, w_down, gpost, n_ctx_tiles, t_ctx, t_lat):
    n = x.shape[0]
    tm = TOKEN_TILE
    midx = functools.partial(_mod_index, n_ctx_tiles=n_ctx_tiles, tiles_per_lat=1)
    return pl.pallas_call(
        functools.partial(_ffn_kernel, n_ctx_tiles=n_ctx_tiles, t_ctx=t_ctx, t_lat=t_lat),
        grid=(n // tm,),
        in_specs=[
            pl.BlockSpec((tm, D_MODEL), lambda i: (i, 0)),
            pl.BlockSpec((1, 1, 6 * D_MODEL), lambda i: (midx(i), 0, 0)),
            pl.BlockSpec((1, D_MODEL), lambda i: (0, 0)),
            _resident((D_MODEL, 2 * D_FF)),
            _resident((3, 2 * D_FF)),
            _resident((D_FF, D_MODEL)),
            pl.BlockSpec((1, D_MODEL), lambda i: (0, 0)),
        ],
        out_specs=pl.BlockSpec((tm, D_MODEL), lambda i: (i, 0)),
        out_shape=jax.ShapeDtypeStruct((n, D_MODEL), F32),
        scratch_shapes=[pltpu.VMEM((tm, D_FF), BF16)],
        compiler_params=_cparams(("parallel",)),
    )(x, mod, gpre, w_up, conv_w, w_down, gpost)


class _ScanGeom:
    def __init__(self, b):
        self.b = b
        self.pairs = b * N_HEAD
        self.fold = max(1, LANES // self.pairs)
        self.ks = HEAD_DIM // self.fold
        self.pt = LANES // self.fold
        assert self.pairs % self.pt == 0 and self.ks % 8 == 0
        self.tiles = self.pairs // self.pt
        self.n_lane = self.tiles * LANES

    def key_operand(self, xs, t):
        lead = xs.shape[:-2]
        n = len(lead)
        ax = lambda *p: tuple(range(n)) + tuple(n + i for i in p)
        x = xs.reshape(lead + (self.b, t, N_HEAD, self.fold, self.ks)).transpose(ax(1, 4, 3, 0, 2))
        x = x.reshape(lead + (t, self.ks, self.fold, self.tiles, self.pt)).transpose(ax(0, 1, 3, 2, 4))
        return x.reshape(lead + (t, self.ks, self.n_lane))

    def value_operand(self, x, t):
        x = x.reshape(self.b, t, N_HEAD, HEAD_DIM).transpose(1, 3, 0, 2).reshape(t, HEAD_DIM, self.tiles, 1, self.pt)
        return jnp.broadcast_to(x, (t, HEAD_DIM, self.tiles, self.fold, self.pt)).reshape(t, HEAD_DIM, self.n_lane)

    def key_param(self, p):
        x = jnp.broadcast_to(p.reshape(1, N_HEAD, self.fold, self.ks), (self.b, N_HEAD, self.fold, self.ks))
        return x.reshape(self.tiles, self.pt, self.fold, self.ks).transpose(3, 0, 2, 1).reshape(self.ks, self.n_lane)

    def value_param(self, p):
        x = jnp.broadcast_to(p.reshape(1, N_HEAD, HEAD_DIM), (self.b, N_HEAD, HEAD_DIM))
        x = x.reshape(self.tiles, 1, self.pt, HEAD_DIM).transpose(3, 0, 1, 2)
        return jnp.broadcast_to(x, (HEAD_DIM, self.tiles, self.fold, self.pt)).reshape(HEAD_DIM, self.n_lane)

    def state_in(self, s0):
        x = s0.reshape(self.b, 2, N_HEAD, HEAD_DIM, self.fold, self.ks).transpose(1, 3, 5, 4, 0, 2)
        x = x.reshape(2, HEAD_DIM, self.ks, self.fold, self.tiles, self.pt).transpose(0, 1, 2, 4, 3, 5)
        return x.reshape(2, HEAD_DIM // 8, 8, self.ks, self.n_lane).transpose(0, 1, 3, 2, 4)

    def state_out(self, s):
        x = s.transpose(0, 1, 3, 2, 4).reshape(2, HEAD_DIM, self.ks, self.tiles, self.fold, self.pt)
        x = x.transpose(0, 1, 2, 4, 3, 5).reshape(2, HEAD_DIM, self.ks, self.fold, self.b, N_HEAD)
        return x.transpose(4, 0, 5, 1, 3, 2).reshape(self.b, 2, N_HEAD, HEAD_DIM, HEAD_DIM)

    def tokens_out(self, o):
        t = o.shape[0]
        x = o.reshape(t, HEAD_DIM, self.tiles, self.fold, self.pt)[:, :, :, 0]
        return x.reshape(t, HEAD_DIM, self.b, N_HEAD).transpose(2, 0, 3, 1).reshape(self.b * t, D_BRANCH)


def _rwkv_branch(rkv, dec, alpha, lp, s0, b, t):
    geo = _ScanGeom(b)
    r, k, w, a = (geo.key_operand(x, t) for x in (rkv[0], rkv[1], dec, alpha))
    v = geo.value_operand(rkv[2], t)
    scan_p = jnp.stack([geo.key_param(lp['k_k']), geo.key_param(lp['k_a'])])
    if s0 is None:
        s0_l = jnp.zeros((2, HEAD_DIM // 8, geo.ks, 8, geo.n_lane), F32)
    else:
        s0_l = geo.state_in(s0)
    yf, yb, s_fin = _wkv_scan(r, k, w, a, v, scan_p, s0_l, geo.fold)
    post_v = jnp.stack([geo.value_param(lp['ln_g']), geo.value_param(lp['ln_b'])])
    o = _rwkv_post(yf, yb, r, k, v, geo.key_param(lp['r_k']), post_v, geo.fold)
    return geo.tokens_out(o), geo.state_out(s_fin)


def kernel(x_prompt, x_sample, cache_na_k, cache_na_v, state_rwkv, state_ret, c, c_ctx, ada_w, ada_b, norm_mix_pre, norm_mix_post, norm_ffn_pre, norm_ffn_post, w_in, rwkv_conv, rwkv_w0, rwkv_w_up, rwkv_a0, rwkv_a_up, rwkv_g_up, rwkv_k_k, rwkv_k_a, rwkv_r_k, rwkv_ln_g, rwkv_ln_b, na_rpb, ret_decay, ret_gn, w_o_rwkv, w_o_na, w_o_ret, w_out, ffn_up, ffn_conv, ffn_down):
    bc, tc, _ = x_prompt.shape
    bl, tl, _ = x_sample.shape
    nc, nl = bc * tc, bl * tl
    assert tl == TOKEN_TILE and TOKEN_TILE % tc == 0 and nc % TOKEN_TILE == 0 and 1 + bl <= MOD_ROWS
    assert tc & (tc - 1) == 0 and tl % GRID_W == 0
    n_ctx_tiles = nc // TOKEN_TILE

    x = jnp.concatenate([x_prompt.reshape(nc, D_MODEL), x_sample.reshape(nl, D_MODEL)], axis=0)
    cvec = jnp.concatenate([c_ctx[None, :], c, jnp.zeros((MOD_ROWS - 1 - bl, D_MODEL), F32)], axis=0)
    mods = _ada_modulation(cvec, ada_w, ada_b).reshape(DEPTH, MOD_ROWS, 1, 6 * D_MODEL)
    rope_cos, rope_sin = _rope_tables(tl)
    row = lambda p: p.reshape(1, -1)

    new_k, new_v, new_rw, new_rt = [], [], [], []
    for l in range(DEPTH):
        mod = mods[l]
        z = _in_proj(x, mod, row(norm_mix_pre[l]), w_in[l].astype(BF16), n_ctx_tiles)

        conv_w = jnp.pad(rwkv_conv[l], ((0, 0), (0, RWKV_PAD - RWKV_COLS)))
        wup_pad = jnp.pad(rwkv_w_up[l], ((0, 0), (0, LANES - W_LORA), (0, 0))).astype(BF16)
        aup_pad = jnp.pad(rwkv_a_up[l], ((0, 0), (W_LORA, LANES - W_LORA - A_LORA), (0, 0))).astype(BF16)
        lora_w = (conv_w[:, 3 * D_BRANCH:3 * D_BRANCH + 2 * LANES], wup_pad, aup_pad, rwkv_g_up[l].astype(BF16),
                  rwkv_w0[l], rwkv_a0[l])
        lp = {'k_k': rwkv_k_k[l], 'k_a': rwkv_k_a[l], 'ln_g': rwkv_ln_g[l], 'ln_b': rwkv_ln_b[l],
              'r_k': rwkv_r_k[l].reshape(-1)}
        n_lat_tiles = nl // TOKEN_TILE
        rkv_c = _rwkv_conv(z, conv_w, 0, n_ctx_tiles, tc)
        dec_c, alpha_c, g_rwkv = _rwkv_lora(z, lora_w, None, 0, n_ctx_tiles, tc)
        rkv_l = _rwkv_conv(z, conv_w, n_ctx_tiles, n_lat_tiles, tl)
        dec_l, alpha_l, g_rwkv = _rwkv_lora(z, lora_w, g_rwkv, n_ctx_tiles, n_lat_tiles, tl)
        o_a_ctx, st_rw = _rwkv_branch(rkv_c, dec_c, alpha_c, lp, None, bc, tc)
        o_a_lat, _ = _rwkv_branch(rkv_l, dec_l, alpha_l, lp, state_rwkv[:, l], bl, tl)
        o_rwkv = jnp.concatenate([o_a_ctx, o_a_lat], axis=0)

        o_na, k_ctx, v_ctx = _ctx_attention(z, nc + nl, bc, tc)
        o_na = _na_attention(z, o_na, cache_na_k, cache_na_v, _na_bias_table(na_rpb[l]), l, n_ctx_tiles, bl, tl)

        dec_ret = jnp.broadcast_to(ret_decay[l][:, :, None, None], (2, N_HEAD, 8, tl))
        gn = row(ret_gn[l])
        o_ret, st_rt = _retention_ctx(z, dec_ret[..., :tc], gn, nc + nl, bc, tc)
        o_ret = _retention_lat(z, o_ret, dec_ret, gn, rope_cos, rope_sin, state_ret, l, n_ctx_tiles, bl, tl)

        x = _mix_out(x, mod, o_rwkv, g_rwkv, o_na, o_ret, z, w_o_rwkv[l].astype(BF16), w_o_na[l].astype(BF16),
                     w_o_ret[l].astype(BF16), w_out[l].astype(BF16), row(norm_mix_post[l]), n_ctx_tiles)
        x = _ffn(x, mod, row(norm_ffn_pre[l]), ffn_up[l].astype(BF16), ffn_conv[l], ffn_down[l].astype(BF16),
                 row(norm_ffn_post[l]), n_ctx_tiles, tc, tl)

        new_k.append(k_ctx)
        new_v.append(v_ctx)
        new_rw.append(st_rw)
        new_rt.append(st_rt)

    return (x[:nc].reshape(bc, tc, D_MODEL), x[nc:].reshape(bl, tl, D_MODEL), jnp.stack(new_k, axis=1),
            jnp.stack(new_v, axis=1), jnp.stack(new_rw, axis=1), jnp.stack(new_rt, axis=1))
```

```python
import functools
import math

import jax
import jax.numpy as jnp
from jax import lax
from jax.experimental import pallas as pl
from jax.experimental.pallas import tpu as pltpu

F32 = jnp.float32
BF16 = jnp.bfloat16

D_MODEL = 1024
DEPTH = 2
N_HEAD = 8
HEAD_DIM = 64
D_BRANCH = N_HEAD * HEAD_DIM
GRID_W = 64
NA_KH = 8
NA_KW = 16
W_LORA = 64
A_LORA = 64
G_LORA = 128
D_FF = 2816
ROPE_BASE = 10000.0
NEG_BIG = -1e9
RWKV_LN_EPS = 64e-5
RET_GN_EPS = 1e-5
RMS_EPS = 1e-6

RWKV_COLS = 3 * D_BRANCH + W_LORA + A_LORA + G_LORA
RWKV_PAD = 2048
OFF_GATE = 0
OFF_RWKV = 3 * D_MODEL
OFF_NA = OFF_RWKV + RWKV_PAD
OFF_RET = OFF_NA + 3 * D_BRANCH
Z_COLS = OFF_RET + 4 * D_BRANCH

LANES = 128
TOKEN_TILE = 1024
MOD_ROWS = 8
VMEM_LIMIT = 56 * 1024 * 1024


def _cparams(sem):
    return pltpu.CompilerParams(dimension_semantics=sem, vmem_limit_bytes=VMEM_LIMIT)


def _rms(x, g):
    return x * lax.rsqrt(jnp.mean(x * x, axis=-1, keepdims=True) + RMS_EPS) * g


def _softplus(x):
    return jnp.maximum(x, 0.0) + jnp.log1p(jnp.exp(-jnp.abs(x)))


def _dot(a, b):
    return jnp.dot(a, b, preferred_element_type=F32)


def _dot_nt(a, b):
    return lax.dot_general(a, b, (((1,), (1,)), ((), ())), preferred_element_type=F32)


def _dot_tn(a, b):
    return lax.dot_general(a, b, (((0,), (0,)), ((), ())), preferred_element_type=F32)


def _mod_index(i, n_ctx_tiles, tiles_per_lat):
    return jnp.where(i < n_ctx_tiles, 0, 1 + jnp.maximum(i - n_ctx_tiles, 0) // tiles_per_lat)


def _seq_edges(i, n_ctx_tiles, t_ctx, t_lat, tm):
    seqlen = jnp.where(i < n_ctx_tiles, t_ctx, t_lat)
    pos = lax.broadcasted_iota(jnp.int32, (tm, 1), 0) & (seqlen - 1)
    return pos == 0, pos == seqlen - 1


def _dwconv3(u, w, first, last):
    tm = u.shape[0]
    prev = jnp.where(first, 0.0, pltpu.roll(u, 1, 0))
    nxt = jnp.where(last, 0.0, pltpu.roll(u, tm - 1, 0))
    return prev * w[0:1] + u * w[1:2] + nxt * w[2:3]


def _ada_kernel(c_ref, w_ref, b_ref, o_ref):
    c = c_ref[...]
    s = (c * jax.nn.sigmoid(c)).astype(BF16)
    o_ref[0] = _dot(s, w_ref[0].astype(BF16)) + b_ref[0]


def _ada_modulation(cvec, ada_w, ada_b):
    tn = 512
    n_out = 6 * D_MODEL
    return pl.pallas_call(
        _ada_kernel,
        grid=(DEPTH, n_out // tn),
        in_specs=[
            pl.BlockSpec((MOD_ROWS, D_MODEL), lambda l, j: (0, 0)),
            pl.BlockSpec((1, D_MODEL, tn), lambda l, j: (l, 0, j)),
            pl.BlockSpec((1, 1, tn), lambda l, j: (l, 0, j)),
        ],
        out_specs=pl.BlockSpec((1, MOD_ROWS, tn), lambda l, j: (l, 0, j)),
        out_shape=jax.ShapeDtypeStruct((DEPTH, MOD_ROWS, n_out), F32),
        compiler_params=_cparams(("parallel", "parallel")),
    )(cvec, ada_w, ada_b.reshape(DEPTH, 1, n_out))


def _resident(shape):
    return pl.BlockSpec(shape, lambda *_: (0,) * len(shape), pipeline_mode=pl.Buffered(1))


_W_RWKV, _W_NA, _W_RET, _W_GATE = 0, RWKV_COLS, RWKV_COLS + 3 * D_BRANCH, RWKV_COLS + 7 * D_BRANCH
IN_PROJ_GROUPS = ((_W_GATE, OFF_GATE, 3 * D_MODEL), (_W_RWKV, OFF_RWKV, RWKV_COLS),
                  (_W_NA, OFF_NA, 3 * D_BRANCH), (_W_RET, OFF_RET, 4 * D_BRANCH))
IN_COLS = _W_GATE + 3 * D_MODEL


def _in_proj_kernel(x_ref, m_ref, g_ref, w_ref, o_ref, *, tn):
    y = _rms(x_ref[...], g_ref[...])
    h = (y * (1.0 + m_ref[0, :, D_MODEL:2 * D_MODEL]) + m_ref[0, :, 0:D_MODEL]).astype(BF16)
    for src, dst, width in IN_PROJ_GROUPS:
        for c in range(0, width, tn):
            n = min(tn, width - c)
            o_ref[:, dst + c:dst + c + n] = _dot(h, w_ref[:, src + c:src + c + n]).astype(o_ref.dtype)
    pad = slice(OFF_RWKV + RWKV_COLS, OFF_RWKV + RWKV_PAD)
    o_ref[:, pad] = jnp.zeros((o_ref.shape[0], RWKV_PAD - RWKV_COLS), o_ref.dtype)


def _in_proj(x, mod, gain, w_bf16, n_ctx_tiles):
    n = x.shape[0]
    tm, tn = 512, 512
    per = TOKEN_TILE // tm
    midx = functools.partial(_mod_index, n_ctx_tiles=n_ctx_tiles * per, tiles_per_lat=per)
    return pl.pallas_call(
        functools.partial(_in_proj_kernel, tn=tn),
        grid=(n // tm,),
        in_specs=[
            pl.BlockSpec((tm, D_MODEL), lambda i: (i, 0)),
            pl.BlockSpec((1, 1, 6 * D_MODEL), lambda i: (midx(i), 0, 0)),
            pl.BlockSpec((1, D_MODEL), lambda i: (0, 0)),
            _resident((D_MODEL, IN_COLS)),
        ],
        out_specs=pl.BlockSpec((tm, Z_COLS), lambda i: (i, 0)),
        out_shape=jax.ShapeDtypeStruct((n, Z_COLS), BF16),
        compiler_params=_cparams(("parallel",)),
    )(x, mod, gain, w_bf16)


def _stream_edges(t_seq, tm):
    pos = lax.broadcasted_iota(jnp.int32, (tm, 1), 0) & (t_seq - 1)
    return pos == 0, pos == t_seq - 1


def _rwkv_conv_kernel(z_ref, w_ref, o_ref, *, t_seq):
    first, last = _stream_edges(t_seq, z_ref.shape[0])
    o_ref[0] = _dwconv3(z_ref[...].astype(F32), w_ref[...], first, last)


def _rwkv_conv(z, conv_w, row_tile0, n_tiles, t_seq):
    tm, tn = TOKEN_TILE, D_BRANCH
    cb = OFF_RWKV // tn
    return pl.pallas_call(
        functools.partial(_rwkv_conv_kernel, t_seq=t_seq),
        grid=(n_tiles, 3),
        in_specs=[
            pl.BlockSpec((tm, tn), lambda i, j: (row_tile0 + i, cb + j)),
            pl.BlockSpec((3, tn), lambda i, j: (0, j)),
        ],
        out_specs=pl.BlockSpec((1, tm, tn), lambda i, j: (j, i, 0)),
        out_shape=jax.ShapeDtypeStruct((3, n_tiles * tm, tn), F32),
        compiler_params=_cparams(("parallel", "parallel")),
    )(z, conv_w)


def _rwkv_lora_kernel(z_ref, cw_ref, wup_ref, aup_ref, gup_ref, w0_ref, a0_ref, *rest, t_seq):
    dec_ref, al_ref, g_ref = rest[-3:]
    first, last = _stream_edges(t_seq, z_ref.shape[0])
    u = _dwconv3(z_ref[...].astype(F32), cw_ref[...], first, last)
    wa = u[:, 0:LANES]
    tw = jnp.tanh(wa).astype(BF16)
    ab = wa.astype(BF16)
    sg = jax.nn.sigmoid(u[:, LANES:2 * LANES]).astype(BF16)
    for d in range(2):
        x = w0_ref[d:d + 1, :] + _dot(tw, wup_ref[d])
        w_log = -_softplus(-x) - 0.5
        dec_ref[d] = jnp.exp(-jnp.exp(w_log))
        al_ref[d] = jax.nn.sigmoid(a0_ref[d:d + 1, :] + _dot(ab, aup_ref[d]))
    g_ref[...] = _dot(sg, gup_ref[...])


def _rwkv_lora(z, lora_w, g_prev, row_tile0, n_tiles, t_seq):
    n = z.shape[0]
    tm = TOKEN_TILE
    wl = 2 * LANES
    cb = (OFF_RWKV + 3 * D_BRANCH) // wl
    out2 = jax.ShapeDtypeStruct((2, n_tiles * tm, D_BRANCH), F32)
    ospec2 = pl.BlockSpec((2, tm, D_BRANCH), lambda i: (0, i, 0))
    full2 = lambda shape: pl.BlockSpec(shape, lambda i: (0, 0))
    full3 = lambda shape: pl.BlockSpec(shape, lambda i: (0, 0, 0))
    in_specs = [
        pl.BlockSpec((tm, wl), lambda i: (row_tile0 + i, cb)),
        full2((3, wl)),
        full3((2, LANES, D_BRANCH)),
        full3((2, LANES, D_BRANCH)),
        full2((G_LORA, D_BRANCH)),
        full2((2, D_BRANCH)),
        full2((2, D_BRANCH)),
    ]
    args = (z,) + tuple(lora_w)
    aliases = {}
    if g_prev is not None:
        in_specs.append(pl.BlockSpec(memory_space=pl.ANY))
        args += (g_prev,)
        aliases = {len(args) - 1: 2}
    return pl.pallas_call(
        functools.partial(_rwkv_lora_kernel, t_seq=t_seq),
        grid=(n_tiles,),
        in_specs=in_specs,
        out_specs=[ospec2, ospec2, pl.BlockSpec((tm, D_BRANCH), lambda i: (row_tile0 + i, 0))],
        out_shape=[out2, out2, jax.ShapeDtypeStruct((n, D_BRANCH), F32)],
        input_output_aliases=aliases,
        compiler_params=_cparams(("parallel",)),
    )(*args)


def _fold_sum(x, lane, fold):
    s = LANES // fold
    while s < LANES:
        x = x + jnp.where((lane & s) != 0, pltpu.roll(x, s, 1), pltpu.roll(x, LANES - s, 1))
        s *= 2
    return x


SCAN_TB = 8


GROUPS_PER_PASS = 4


def _wkv_kernel(rf_ref, rb_ref, kf_ref, kb_ref, wf_ref, wb_ref, af_ref, ab_ref, vf_ref, vb_ref, p_ref, s0_ref,
                yf_ref, yb_ref, st_ref, s_scr, op_scr, *, fold, unroll):
    tb = SCAN_TB
    n_grp, ks = s_scr.shape[1], s_scr.shape[2]
    t = pl.program_id(1)

    @pl.when(t == 0)
    def _():
        s_scr[...] = s0_ref[...]

    lane = lax.broadcasted_iota(jnp.int32, (1, LANES), 1)
    k_k = p_ref[0]
    k_a = p_ref[1]
    dirs = ((rf_ref, kf_ref, wf_ref, af_ref, vf_ref, yf_ref), (rb_ref, kb_ref, wb_ref, ab_ref, vb_ref, yb_ref))
    row_of = lambda d, j: j if d == 0 else tb - 1 - j

    for d, (r_ref, k_ref, w_ref, a_ref, _, _) in enumerate(dirs):
        for j in range(tb):
            jj = row_of(d, j)
            k = k_ref[jj]
            a = a_ref[jj]
            kk = k * k_k
            kk = kk / jnp.maximum(jnp.sqrt(_fold_sum(jnp.sum(kk * kk, axis=0, keepdims=True), lane, fold)), 1e-12)
            op_scr[d, j, 0] = -kk
            op_scr[d, j, 1] = w_ref[jj]
            op_scr[d, j, 2] = kk * a
            op_scr[d, j, 3] = k * (1.0 + (a - 1.0) * k_a)
            op_scr[d, j, 4] = r_ref[jj]

    zeros = lambda n: tuple(jnp.zeros((8, LANES), F32) for _ in range(n))
    op_row = lambda d, j, i, k: op_scr[d, j, i, pl.ds(k, 1), :]
    rows = lambda g: slice(g * 8, (g + 1) * 8)

    def step(j, carry):
        sa = []
        for d in range(2):
            def sa_acc(k, acc):
                a = op_row(d, j, 0, k)
                return tuple(acc[g] + s_scr[d, g, k] * a for g in range(n_grp))

            sa.append(lax.fori_loop(0, ks, sa_acc, zeros(n_grp), unroll=unroll))
        sa = [tuple(_fold_sum(x, lane, fold) for x in sa_d) for sa_d in sa]

        for d in range(2):
            v_ref, y_ref = dirs[d][4], dirs[d][5]
            jj = row_of(d, j)
            for first in range(0, n_grp, GROUPS_PER_PASS):
                groups = range(first, first + GROUPS_PER_PASS)
                vs = [v_ref[jj, rows(g), :] for g in groups]

                def update(k, acc):
                    w, b, kd, r = (op_row(d, j, i, k) for i in (1, 2, 3, 4))
                    out = []
                    for i, g in enumerate(groups):
                        sn = s_scr[d, g, k] * w + sa[d][g] * b + vs[i] * kd
                        s_scr[d, g, k] = sn
                        out.append(acc[i] + sn * r)
                    return tuple(out)

                ys = lax.fori_loop(0, ks, update, zeros(GROUPS_PER_PASS), unroll=unroll)
                for i, g in enumerate(groups):
                    y_ref[jj, rows(g), :] = ys[i]
        return carry

    lax.fori_loop(0, tb, step, 0)

    @pl.when(t == pl.num_programs(1) - 1)
    def _():
        st_ref[...] = s_scr[...]


def _wkv_scan(r, k, w, a, v, params, s0, fold):
    t_len, ks, n_lane = r.shape
    tb = SCAN_TB
    nt = t_len // tb
    n_grp = HEAD_DIM // 8
    fwd = lambda rows: pl.BlockSpec((tb, rows, LANES), lambda c, t: (t, 0, c))
    bwd = lambda rows: pl.BlockSpec((tb, rows, LANES), lambda c, t: (nt - 1 - t, 0, c))
    fwd_d = lambda rows: pl.BlockSpec((None, tb, rows, LANES), lambda c, t: (0, t, 0, c))
    bwd_d = lambda rows: pl.BlockSpec((None, tb, rows, LANES), lambda c, t: (1, nt - 1 - t, 0, c))
    state = pl.BlockSpec((2, n_grp, ks, 8, LANES), lambda c, t: (0, 0, 0, 0, c))
    return pl.pallas_call(
        functools.partial(_wkv_kernel, fold=fold, unroll=8),
        grid=(n_lane // LANES, nt),
        in_specs=[fwd(ks), bwd(ks)] * 2 + [fwd_d(ks), bwd_d(ks)] * 2 + [
            fwd(HEAD_DIM), bwd(HEAD_DIM), pl.BlockSpec((2, ks, LANES), lambda c, t: (0, 0, c)), state],
        out_specs=[fwd(HEAD_DIM), bwd(HEAD_DIM), state],
        out_shape=[jax.ShapeDtypeStruct((t_len, HEAD_DIM, n_lane), F32)] * 2
        + [jax.ShapeDtypeStruct((2, n_grp, ks, 8, n_lane), F32)],
        scratch_shapes=[pltpu.VMEM((2, n_grp, ks, 8, LANES), F32), pltpu.VMEM((2, tb, 5, ks, LANES), F32)],
        compiler_params=_cparams(("parallel", "arbitrary")),
    )(r, r, k, k, w, w, a, a, v, v, params, s0)


def _rwkv_post_kernel(yf_ref, yb_ref, r_ref, k_ref, v_ref, pk_ref, pv_ref, o_ref, *, fold):
    lane = lax.broadcasted_iota(jnp.int32, (1, LANES), 1)
    ln_g = pv_ref[0]
    ln_b = pv_ref[1]
    r_k = pk_ref[...]
    for j in range(SCAN_TB):
        y = _fold_sum(yf_ref[j] + yb_ref[j], lane, fold)
        mu = jnp.mean(y, axis=0, keepdims=True)
        dlt = y - mu
        var = jnp.mean(dlt * dlt, axis=0, keepdims=True)
        o = dlt * lax.rsqrt(var + RWKV_LN_EPS) * ln_g + ln_b
        rk = _fold_sum(jnp.sum(r_ref[j] * k_ref[j] * r_k, axis=0, keepdims=True), lane, fold)
        o_ref[j] = o + rk * v_ref[j]


def _rwkv_post(yf, yb, r, k, v, pk, pv, fold):
    t_len, _, n_lane = yf.shape
    ks = r.shape[1]
    spec = lambda rows: pl.BlockSpec((SCAN_TB, rows, LANES), lambda c, i: (i, 0, c))
    return pl.pallas_call(
        functools.partial(_rwkv_post_kernel, fold=fold),
        grid=(n_lane // LANES, t_len // SCAN_TB),
        in_specs=[
            spec(HEAD_DIM), spec(HEAD_DIM), spec(ks), spec(ks), spec(HEAD_DIM),
            pl.BlockSpec((ks, LANES), lambda c, i: (0, c)),
            pl.BlockSpec((2, HEAD_DIM, LANES), lambda c, i: (0, 0, c)),
        ],
        out_specs=spec(HEAD_DIM),
        out_shape=jax.ShapeDtypeStruct((t_len, HEAD_DIM, n_lane), F32),
        compiler_params=_cparams(("parallel", "parallel")),
    )(yf, yb, r, k, v, pk, pv)


def _head(h):
    return slice(h * HEAD_DIM, (h + 1) * HEAD_DIM)


def _ctx_attn_kernel(q_ref, k_ref, v_ref, *rest):
    o_ref, ko_ref, vo_ref = rest[-3:]
    scale = HEAD_DIM ** -0.5
    for h in range(N_HEAD):
        k = k_ref[:, _head(h)]
        v = v_ref[:, _head(h)]
        ko_ref[0, h] = k.astype(F32)
        vo_ref[0, h] = v.astype(F32)
        s = _dot_nt(q_ref[:, _head(h)], k) * scale
        p = jnp.exp(s - jnp.max(s, axis=-1, keepdims=True))
        p = p / jnp.sum(p, axis=-1, keepdims=True)
        o_ref[:, _head(h)] = _dot(p.astype(BF16), v)


def _layer_slot(stacked_prev, n_in):
    if stacked_prev is None:
        return [], (), {}
    specs = [pl.BlockSpec(memory_space=pl.ANY)] * len(stacked_prev)
    return specs, tuple(stacked_prev), {n_in + i: 1 + i for i in range(len(stacked_prev))}


def _ctx_attention(z, n_rows, b, t, layer, kv_prev):
    cb = OFF_NA // D_BRANCH
    zspec = lambda c: pl.BlockSpec((t, D_BRANCH), lambda i: (i, cb + c))
    kv_shape = jax.ShapeDtypeStruct((b, DEPTH, N_HEAD, t, HEAD_DIM), F32)
    kv_spec = pl.BlockSpec((1, None, N_HEAD, t, HEAD_DIM), lambda i: (i, layer, 0, 0, 0))
    extra_specs, extra_args, aliases = _layer_slot(kv_prev, 3)
    return pl.pallas_call(
        _ctx_attn_kernel,
        grid=(b,),
        in_specs=[zspec(0), zspec(1), zspec(2)] + extra_specs,
        out_specs=[pl.BlockSpec((t, D_BRANCH), lambda i: (i, 0)), kv_spec, kv_spec],
        out_shape=[jax.ShapeDtypeStruct((n_rows, D_BRANCH), F32), kv_shape, kv_shape],
        input_output_aliases=aliases,
        compiler_params=_cparams(("parallel",)),
    )(z, z, z, *extra_args)


def _na_kernel(q_ref, k_ref, v_ref, kc_ref, vc_ref, tab_ref, prev_ref, o_ref, *, rows):
    del prev_ref
    scale = HEAD_DIM ** -0.5
    win = NA_KH * GRID_W
    qcol = lax.broadcasted_iota(jnp.int32, (GRID_W, win), 0)
    kcol = lax.broadcasted_iota(jnp.int32, (GRID_W, win), 1) & (GRID_W - 1)
    c_start = jnp.clip(qcol - NA_KW // 2, 0, GRID_W - NA_KW)
    col_valid = (kcol >= c_start) & (kcol < c_start + NA_KW)
    def row_block(r, carry):
        rs = jnp.clip(r - NA_KH // 2, 0, rows - NA_KH)
        q_rows = pl.ds(pl.multiple_of(r * GRID_W, GRID_W), GRID_W)
        w_rows = pl.ds(pl.multiple_of(rs * GRID_W, GRID_W), win)
        first = rs - r + NA_KH - 1
        for h in range(N_HEAD):
            q = q_ref[q_rows, _head(h)].astype(BF16)
            kw = k_ref[w_rows, _head(h)].astype(BF16)
            vw = v_ref[w_rows, _head(h)].astype(BF16)
            kc = kc_ref[0, 0, h].astype(BF16)
            vc = vc_ref[0, 0, h].astype(BF16)
            bias = jnp.concatenate([tab_ref[h, first + 2 * i] for i in range(NA_KH // 2)], axis=1)
            s_loc = jnp.where(col_valid, _dot_nt(q, kw) * scale + bias, NEG_BIG)
            s_ctx = _dot_nt(q, kc) * scale
            m = jnp.maximum(jnp.max(s_loc, axis=-1, keepdims=True), jnp.max(s_ctx, axis=-1, keepdims=True))
            p_loc = jnp.exp(s_loc - m)
            p_ctx = jnp.exp(s_ctx - m)
            den = jnp.sum(p_loc, axis=-1, keepdims=True) + jnp.sum(p_ctx, axis=-1, keepdims=True)
            o_ref[q_rows, _head(h)] = _dot((p_loc / den).astype(BF16), vw) + _dot((p_ctx / den).astype(BF16), vc)
        return carry

    lax.fori_loop(0, rows, row_block, 0)


def _na_attention(z, o_prev, cache_k, cache_v, bias_tab, layer, row_tile0, b, t):
    past = cache_k.shape[3]
    rows = t // GRID_W
    assert rows >= NA_KH
    cb = OFF_NA // D_BRANCH
    zspec = lambda c: pl.BlockSpec((t, D_BRANCH), lambda i: (row_tile0 + i, cb + c))
    cspec = pl.BlockSpec((1, 1, N_HEAD, past, HEAD_DIM), lambda i: (i, layer, 0, 0, 0))
    return pl.pallas_call(
        functools.partial(_na_kernel, rows=rows),
        grid=(b,),
        in_specs=[zspec(0), zspec(1), zspec(2), cspec, cspec,
                  pl.BlockSpec(bias_tab.shape, lambda i: (0, 0, 0, 0)),
                  pl.BlockSpec(memory_space=pl.ANY)],
        out_specs=pl.BlockSpec((t, D_BRANCH), lambda i: (row_tile0 + i, 0)),
        out_shape=jax.ShapeDtypeStruct(o_prev.shape, F32),
        input_output_aliases={6: 0},
        compiler_params=_cparams(("parallel",)),
    )(z, z, z, cache_k, cache_v, bias_tab, o_prev)


def _na_bias_table(rpb):
    h, ndr, _ = rpb.shape
    edge = GRID_W - NA_KW
    ext = jnp.concatenate([jnp.broadcast_to(rpb[..., :1], (h, ndr, edge)), rpb,
                           jnp.broadcast_to(rpb[..., -1:], (h, ndr, edge + 1))], axis=-1)
    skew = jnp.tile(ext, (1, 1, GRID_W))[..., :GRID_W * (2 * GRID_W - 1)].reshape(h, ndr, GRID_W, 2 * GRID_W - 1)
    toep = skew[..., GRID_W - 1:]
    return jnp.concatenate([toep[:, :-1], toep[:, 1:]], axis=-1)


def _rope_tables(t_len):
    half = HEAD_DIM // 2
    nf = half // 2
    inv = ROPE_BASE ** (-jnp.arange(nf, dtype=F32) / nf)
    t = jnp.arange(t_len)

    def tab(pos):
        ang = pos.astype(F32)[:, None] * inv[None, :]
        cos, sin = jnp.cos(ang), jnp.sin(ang)
        return jnp.concatenate([cos, cos], -1), jnp.concatenate([-sin, sin], -1)

    c_row, s_row = tab(t // GRID_W)
    c_col, s_col = tab(t % GRID_W)
    cos = jnp.tile(jnp.concatenate([c_row, c_col], -1), (1, N_HEAD))
    sin = jnp.tile(jnp.concatenate([s_row, s_col], -1), (1, N_HEAD))
    return cos, sin


def _log_sigmoid(x):
    return -_softplus(-x)


def _ret_heads(q_ref, k_ref, v_ref, g_ref, dec_ref, gn_ref, o_ref, *, t_len, qb, rope_refs=None, s0_ref=None,
               st_ref=None):
    lo = (lax.broadcasted_iota(jnp.int32, (1, LANES), 1) & 31) < 16
    for hp in range(N_HEAD // 2):
        pair = slice(hp * LANES, (hp + 1) * LANES)
        q2 = q_ref[:, pair].astype(F32)
        k2 = k_ref[:, pair].astype(F32)
        if rope_refs is not None:
            cos = rope_refs[0][:, pair]
            sin = rope_refs[1][:, pair]
            rot = lambda x: x * cos + jnp.where(lo, pltpu.roll(x, LANES - 16, 1), pltpu.roll(x, 16, 1)) * sin
            q2, k2 = rot(q2), rot(k2)
        k2 = k2 * (HEAD_DIM ** -0.5)
        for hh in range(2):
            h = 2 * hp + hh
            half = slice(hh * HEAD_DIM, (hh + 1) * HEAD_DIM)
            qh = q2[:, half].astype(BF16)
            k = k2[:, half]
            kb = k.astype(BF16)
            vb = v_ref[:, _head(h)].astype(BF16)
            lgf = _log_sigmoid(dec_ref[0, h])[0:1, :]
            lgb = _log_sigmoid(dec_ref[1, h])[0:1, :]
            lgf_h = lgf[:, 0:HEAD_DIM]
            lgb_h = lgb[:, 0:HEAD_DIM]
            for qi in range(t_len // qb):
                rows = slice(qi * qb, (qi + 1) * qb)
                q = qh[rows]
                s = _dot_nt(q, kb)
                diff = (lax.broadcasted_iota(jnp.int32, (qb, t_len), 0) + qi * qb
                        - lax.broadcasted_iota(jnp.int32, (qb, t_len), 1)).astype(F32)
                dmat = (jnp.where(diff >= 0, jnp.exp(lgf * jnp.maximum(diff, 0.0)), 0.0)
                        + jnp.where(diff <= 0, jnp.exp(lgb * jnp.maximum(-diff, 0.0)), 0.0))
                y = _dot((s * dmat).astype(BF16), vb)
                if s0_ref is not None:
                    pos = (lax.broadcasted_iota(jnp.int32, (qb, HEAD_DIM), 0) + qi * qb).astype(F32)
                    y = y + _dot(q, s0_ref[0, 0, h].astype(BF16)) * jnp.exp(lgf_h * (pos + 1.0))
                    y = y + _dot(q, s0_ref[0, 1, h].astype(BF16)) * jnp.exp(lgb_h * (t_len - pos))
                mu = jnp.mean(y, axis=-1, keepdims=True)
                dlt = y - mu
                var = jnp.mean(dlt * dlt, axis=-1, keepdims=True)
                yn = dlt * lax.rsqrt(var + RET_GN_EPS)
                g = g_ref[rows, _head(h)].astype(F32)
                o_ref[rows, _head(h)] = (yn * gn_ref[:, _head(h)]) * (g * jax.nn.sigmoid(g))
            if st_ref is not None:
                pos = lax.broadcasted_iota(jnp.int32, (t_len, HEAD_DIM), 0).astype(F32)
                kzf = (k * jnp.exp(lgf_h * (t_len - 1.0 - pos))).astype(BF16)
                kzb = (k * jnp.exp(lgb_h * pos)).astype(BF16)
                sf = _dot_tn(kzf, vb)
                sb = _dot_tn(kzb, vb)
                if s0_ref is not None:
                    sf = sf + s0_ref[0, 0, h] * jnp.exp(lgf_h * float(t_len))
                    sb = sb + s0_ref[0, 1, h] * jnp.exp(lgb_h * float(t_len))
                st_ref[0, 0, h] = sf
                st_ref[0, 1, h] = sb


def _ret_ctx_kernel(q_ref, k_ref, v_ref, g_ref, dec_ref, gn_ref, *rest, t_len, qb):
    o_ref, st_ref = rest[-2:]
    _ret_heads(q_ref, k_ref, v_ref, g_ref, dec_ref, gn_ref, o_ref, t_len=t_len, qb=qb, st_ref=st_ref)


def _ret_lat_kernel(q_ref, k_ref, v_ref, g_ref, dec_ref, gn_ref, cos_ref, sin_ref, s0_ref, prev_ref, o_ref,
                    *, t_len, qb):
    del prev_ref
    _ret_heads(q_ref, k_ref, v_ref, g_ref, dec_ref, gn_ref, o_ref, t_len=t_len, qb=qb,
               rope_refs=(cos_ref, sin_ref), s0_ref=s0_ref.at[0])


def _ret_specs(t, row_tile0):
    cb = OFF_RET // D_BRANCH
    zspec = lambda c: pl.BlockSpec((t, D_BRANCH), lambda i: (row_tile0 + i, cb + c))
    return [zspec(0), zspec(1), zspec(2), zspec(3),
            pl.BlockSpec((2, N_HEAD, 8, t), lambda i: (0, 0, 0, 0)),
            pl.BlockSpec((1, D_BRANCH), lambda i: (0, 0))]


def _retention_ctx(z, dec, gn, n_rows, b, t, layer, st_prev):
    st_shape = (b, DEPTH, 2, N_HEAD, HEAD_DIM, HEAD_DIM)
    extra_specs, extra_args, aliases = _layer_slot(st_prev, 6)
    return pl.pallas_call(
        functools.partial(_ret_ctx_kernel, t_len=t, qb=min(t, 256)),
        grid=(b,),
        in_specs=_ret_specs(t, 0) + extra_specs,
        out_specs=[pl.BlockSpec((t, D_BRANCH), lambda i: (i, 0)),
                   pl.BlockSpec((1, None) + st_shape[2:], lambda i: (i, layer, 0, 0, 0, 0))],
        out_shape=[jax.ShapeDtypeStruct((n_rows, D_BRANCH), F32), jax.ShapeDtypeStruct(st_shape, F32)],
        input_output_aliases=aliases,
        compiler_params=_cparams(("parallel",)),
    )(z, z, z, z, dec, gn, *extra_args)


def _retention_lat(z, o_prev, dec, gn, cos, sin, s0, layer, row_tile0, b, t):
    tab = pl.BlockSpec((t, D_BRANCH), lambda i: (0, 0))
    return pl.pallas_call(
        functools.partial(_ret_lat_kernel, t_len=t, qb=min(t, 256)),
        grid=(b,),
        in_specs=_ret_specs(t, row_tile0) + [
            tab, tab,
            pl.BlockSpec((1, 1, 2, N_HEAD, HEAD_DIM, HEAD_DIM), lambda i: (i, layer, 0, 0, 0, 0)),
            pl.BlockSpec(memory_space=pl.ANY)],
        out_specs=pl.BlockSpec((t, D_BRANCH), lambda i: (row_tile0 + i, 0)),
        out_shape=jax.ShapeDtypeStruct(o_prev.shape, F32),
        input_output_aliases={9: 0},
        compiler_params=_cparams(("parallel",)),
    )(z, z, z, z, dec, gn, cos, sin, s0, o_prev)


def _mix_out_kernel(x_ref, m_ref, oa_ref, ga_ref, ob_ref, oc_ref, g0_ref, g1_ref, g2_ref,
                    wa_ref, wb_ref, wc_ref, wo_ref, gain_ref, o_ref):
    out_a = _dot((oa_ref[...] * ga_ref[...]).astype(BF16), wa_ref[...])
    out_b = _dot(ob_ref[...].astype(BF16), wb_ref[...])
    out_c = _dot(oc_ref[...].astype(BF16), wc_ref[...])
    sig = lambda ref: jax.nn.sigmoid(ref[...].astype(F32))
    merged = sig(g0_ref) * out_a + sig(g1_ref) * out_b + sig(g2_ref) * out_c
    y = _dot(merged.astype(BF16), wo_ref[...])
    gate = m_ref[0, :, 2 * D_MODEL:3 * D_MODEL]
    o_ref[...] = x_ref[...] + gate * _rms(y, gain_ref[...])


def _mix_out(x, mod, o_rwkv, g_rwkv, o_na, o_ret, z, wa, wb, wc, wo, gain, n_ctx_tiles):
    n = x.shape[0]
    tm = 512
    per = TOKEN_TILE // tm
    midx = functools.partial(_mod_index, n_ctx_tiles=n_ctx_tiles * per, tiles_per_lat=per)
    row = lambda w: pl.BlockSpec((tm, w), lambda i: (i, 0))
    gate_spec = lambda g: pl.BlockSpec((tm, D_MODEL), lambda i: (i, OFF_GATE // D_MODEL + g))
    wspec = lambda a, b: pl.BlockSpec((a, b), lambda i: (0, 0))
    return pl.pallas_call(
        _mix_out_kernel,
        grid=(n // tm,),
        in_specs=[
            row(D_MODEL),
            pl.BlockSpec((1, 1, 6 * D_MODEL), lambda i: (midx(i), 0, 0)),
            row(D_BRANCH), row(D_BRANCH), row(D_BRANCH), row(D_BRANCH),
            gate_spec(0), gate_spec(1), gate_spec(2),
            wspec(D_BRANCH, D_MODEL), wspec(D_BRANCH, D_MODEL), wspec(D_BRANCH, D_MODEL),
            wspec(D_MODEL, D_MODEL), wspec(1, D_MODEL),
        ],
        out_specs=row(D_MODEL),
        out_shape=jax.ShapeDtypeStruct((n, D_MODEL), F32),
        compiler_params=_cparams(("parallel",)),
    )(x, mod, o_rwkv, g_rwkv, o_na, o_ret, z, z, z, wa, wb, wc, wo, gain)


def _gelu_tanh(x):
    return x * (0.5 * (1.0 + jnp.tanh(math.sqrt(2.0 / math.pi) * (x + 0.044715 * (x * x * x)))))


FFN_CHUNK = 256


def _ffn_kernel(x_ref, m_ref, gpre_ref, wup_ref, cw_ref, wd_ref, gpost_ref, o_ref, act_scr,
                *, n_ctx_tiles, t_ctx, t_lat):
    tm = x_ref.shape[0]
    y = _rms(x_ref[...], gpre_ref[...])
    h = (y * (1.0 + m_ref[0, :, 4 * D_MODEL:5 * D_MODEL]) + m_ref[0, :, 3 * D_MODEL:4 * D_MODEL]).astype(BF16)
    first, last = _seq_edges(pl.program_id(0), n_ctx_tiles, t_ctx, t_lat, tm)
    for c in range(D_FF // FFN_CHUNK):
        vcols = slice(c * FFN_CHUNK, (c + 1) * FFN_CHUNK)
        gcols = slice(D_FF + c * FFN_CHUNK, D_FF + (c + 1) * FFN_CHUNK)
        val = _dwconv3(_dot(h, wup_ref[:, vcols]), cw_ref[:, vcols], first, last)
        gate = _dwconv3(_dot(h, wup_ref[:, gcols]), cw_ref[:, gcols], first, last)
        act_scr[:, vcols] = (_gelu_tanh(gate) * val).astype(BF16)
    gate2 = m_ref[0, :, 5 * D_MODEL:6 * D_MODEL]
    for half in range(2):
        rows = slice(half * tm // 2, (half + 1) * tm // 2)
        down = _dot(act_scr[rows, :], wd_ref[...])
        o_ref[rows, :] = x_ref[rows, :] + gate2 * _rms(down, gpost_ref[...])


def _ffn(x, mod, gpre, w_up, conv_w, w_down, gpost, n_ctx_tiles, t_ctx, t_lat):
    n = x.shape[0]
    tm = TOKEN_TILE
    midx = functools.partial(_mod_index, n_ctx_tiles=n_ctx_tiles, tiles_per_lat=1)
    return pl.pallas_call(
        functools.partial(_ffn_kernel, n_ctx_tiles=n_ctx_tiles, t_ctx=t_ctx, t_lat=t_lat),
        grid=(n // tm,),
        in_specs=[
            pl.BlockSpec((tm, D_MODEL), lambda i: (i, 0)),
            pl.BlockSpec((1, 1, 6 * D_MODEL), lambda i: (midx(i), 0, 0)),
            pl.BlockSpec((1, D_MODEL), lambda i: (0, 0)),
            _resident((D_MODEL, 2 * D_FF)),
            _resident((3, 2 * D_FF)),
            _resident((D_FF, D_MODEL)),
            pl.BlockSpec((1, D_MODEL), lambda i: (0, 0)),
        ],
        out_specs=pl.BlockSpec((tm, D_MODEL), lambda i: (i, 0)),
        out_shape=jax.ShapeDtypeStruct((n, D_MODEL), F32),
        scratch_shapes=[pltpu.VMEM((tm, D_FF), BF16)],
        compiler_params=_cparams(("parallel",)),
    )(x, mod, gpre, w_up, conv_w, w_down, gpost)


class _ScanGeom:
    def __init__(self, b):
        self.b = b
        self.pairs = b * N_HEAD
        self.fold = max(1, LANES // self.pairs)
        self.ks = HEAD_DIM // self.fold
        self.pt = LANES // self.fold
        assert self.pairs % self.pt == 0 and self.ks % 8 == 0
        self.tiles = self.pairs // self.pt
        self.n_lane = self.tiles * LANES

    def key_operand(self, xs, t):
        lead = xs.shape[:-2]
        n = len(lead)
        ax = lambda *p: tuple(range(n)) + tuple(n + i for i in p)
        x = xs.reshape(lead + (self.b, t, N_HEAD, self.fold, self.ks)).transpose(ax(1, 4, 3, 0, 2))
        x = x.reshape(lead + (t, self.ks, self.fold, self.tiles, self.pt)).transpose(ax(0, 1, 3, 2, 4))
        return x.reshape(lead + (t, self.ks, self.n_lane))

    def value_operand(self, x, t):
        x = x.reshape(self.b, t, N_HEAD, HEAD_DIM).transpose(1, 3, 0, 2).reshape(t, HEAD_DIM, self.tiles, 1, self.pt)
        return jnp.broadcast_to(x, (t, HEAD_DIM, self.tiles, self.fold, self.pt)).reshape(t, HEAD_DIM, self.n_lane)

    def key_param(self, p):
        x = jnp.broadcast_to(p.reshape(1, N_HEAD, self.fold, self.ks), (self.b, N_HEAD, self.fold, self.ks))
        return x.reshape(self.tiles, self.pt, self.fold, self.ks).transpose(3, 0, 2, 1).reshape(self.ks, self.n_lane)

    def value_param(self, p):
        x = jnp.broadcast_to(p.reshape(1, N_HEAD, HEAD_DIM), (self.b, N_HEAD, HEAD_DIM))
        x = x.reshape(self.tiles, 1, self.pt, HEAD_DIM).transpose(3, 0, 1, 2)
        return jnp.broadcast_to(x, (HEAD_DIM, self.tiles, self.fold, self.pt)).reshape(HEAD_DIM, self.n_lane)

    def state_in(self, s0):
        x = s0.reshape(self.b, 2, N_HEAD, HEAD_DIM, self.fold, self.ks).transpose(1, 3, 5, 4, 0, 2)
        x = x.reshape(2, HEAD_DIM, self.ks, self.fold, self.tiles, self.pt).transpose(0, 1, 2, 4, 3, 5)
        return x.reshape(2, HEAD_DIM // 8, 8, self.ks, self.n_lane).transpose(0, 1, 3, 2, 4)

    def state_out(self, s):
        x = s.transpose(0, 1, 3, 2, 4).reshape(2, HEAD_DIM, self.ks, self.tiles, self.fold, self.pt)
        x = x.transpose(0, 1, 2, 4, 3, 5).reshape(2, HEAD_DIM, self.ks, self.fold, self.b, N_HEAD)
        return x.transpose(4, 0, 5, 1, 3, 2).reshape(self.b, 2, N_HEAD, HEAD_DIM, HEAD_DIM)

    def tokens_out(self, o):
        t = o.shape[0]
        x = o.reshape(t, HEAD_DIM, self.tiles, self.fold, self.pt)[:, :, :, 0]
        return x.reshape(t, HEAD_DIM, self.b, N_HEAD).transpose(2, 0, 3, 1).reshape(self.b * t, D_BRANCH)


def _rwkv_branch(rkv, dec, alpha, lp, s0, b, t):
    geo = _ScanGeom(b)
    r, k, w, a = (geo.key_operand(x, t) for x in (rkv[0], rkv[1], dec, alpha))
    v = geo.value_operand(rkv[2], t)
    scan_p = jnp.stack([geo.key_param(lp['k_k']), geo.key_param(lp['k_a'])])
    if s0 is None:
        s0_l = jnp.zeros((2, HEAD_DIM // 8, geo.ks, 8, geo.n_lane), F32)
    else:
        s0_l = geo.state_in(s0)
    yf, yb, s_fin = _wkv_scan(r, k, w, a, v, scan_p, s0_l, geo.fold)
    post_v = jnp.stack([geo.value_param(lp['ln_g']), geo.value_param(lp['ln_b'])])
    o = _rwkv_post(yf, yb, r, k, v, geo.key_param(lp['r_k']), post_v, geo.fold)
    return geo.tokens_out(o), geo.state_out(s_fin)


def kernel(x_prompt, x_sample, cache_na_k, cache_na_v, state_rwkv, state_ret, c, c_ctx, ada_w, ada_b, norm_mix_pre, norm_mix_post, norm_ffn_pre, norm_ffn_post, w_in, rwkv_conv, rwkv_w0, rwkv_w_up, rwkv_a0, rwkv_a_up, rwkv_g_up, rwkv_k_k, rwkv_k_a, rwkv_r_k, rwkv_ln_g, rwkv_ln_b, na_rpb, ret_decay, ret_gn, w_o_rwkv, w_o_na, w_o_ret, w_out, ffn_up, ffn_conv, ffn_down):
    bc, tc, _ = x_prompt.shape
    bl, tl, _ = x_sample.shape
    nc, nl = bc * tc, bl * tl
    assert tl == TOKEN_TILE and TOKEN_TILE % tc == 0 and nc % TOKEN_TILE == 0 and 1 + bl <= MOD_ROWS
    assert tc & (tc - 1) == 0 and tl % GRID_W == 0
    n_ctx_tiles = nc // TOKEN_TILE

    x = jnp.concatenate([x_prompt.reshape(nc, D_MODEL), x_sample.reshape(nl, D_MODEL)], axis=0)
    cvec = jnp.concatenate([c_ctx[None, :], c, jnp.zeros((MOD_ROWS - 1 - bl, D_MODEL), F32)], axis=0)
    mods = _ada_modulation(cvec, ada_w, ada_b).reshape(DEPTH, MOD_ROWS, 1, 6 * D_MODEL)
    rope_cos, rope_sin = _rope_tables(tl)
    row = lambda p: p.reshape(1, -1)

    new_k = new_v = new_rt = None
    new_rw = []
    for l in range(DEPTH):
        mod = mods[l]
        z = _in_proj(x, mod, row(norm_mix_pre[l]), w_in[l].astype(BF16), n_ctx_tiles)

        conv_w = jnp.pad(rwkv_conv[l], ((0, 0), (0, RWKV_PAD - RWKV_COLS)))
        wup_pad = jnp.pad(rwkv_w_up[l], ((0, 0), (0, LANES - W_LORA), (0, 0))).astype(BF16)
        aup_pad = jnp.pad(rwkv_a_up[l], ((0, 0), (W_LORA, LANES - W_LORA - A_LORA), (0, 0))).astype(BF16)
        lora_w = (conv_w[:, 3 * D_BRANCH:3 * D_BRANCH + 2 * LANES], wup_pad, aup_pad, rwkv_g_up[l].astype(BF16),
                  rwkv_w0[l], rwkv_a0[l])
        lp = {'k_k': rwkv_k_k[l], 'k_a': rwkv_k_a[l], 'ln_g': rwkv_ln_g[l], 'ln_b': rwkv_ln_b[l],
              'r_k': rwkv_r_k[l].reshape(-1)}
        n_lat_tiles = nl // TOKEN_TILE
        rkv_c = _rwkv_conv(z, conv_w, 0, n_ctx_tiles, tc)
        dec_c, alpha_c, g_rwkv = _rwkv_lora(z, lora_w, None, 0, n_ctx_tiles, tc)
        rkv_l = _rwkv_conv(z, conv_w, n_ctx_tiles, n_lat_tiles, tl)
        dec_l, alpha_l, g_rwkv = _rwkv_lora(z, lora_w, g_rwkv, n_ctx_tiles, n_lat_tiles, tl)
        o_a_ctx, st_rw = _rwkv_branch(rkv_c, dec_c, alpha_c, lp, None, bc, tc)
        o_a_lat, _ = _rwkv_branch(rkv_l, dec_l, alpha_l, lp, state_rwkv[:, l], bl, tl)
        o_rwkv = jnp.concatenate([o_a_ctx, o_a_lat], axis=0)

        o_na, new_k, new_v = _ctx_attention(z, nc + nl, bc, tc, l, None if l == 0 else (new_k, new_v))
        o_na = _na_attention(z, o_na, cache_na_k, cache_na_v, _na_bias_table(na_rpb[l]), l, n_ctx_tiles, bl, tl)

        dec_ret = jnp.broadcast_to(ret_decay[l][:, :, None, None], (2, N_HEAD, 8, tl))
        gn = row(ret_gn[l])
        o_ret, new_rt = _retention_ctx(z, dec_ret[..., :tc], gn, nc + nl, bc, tc, l, None if l == 0 else (new_rt,))
        o_ret = _retention_lat(z, o_ret, dec_ret, gn, rope_cos, rope_sin, state_ret, l, n_ctx_tiles, bl, tl)

        x = _mix_out(x, mod, o_rwkv, g_rwkv, o_na, o_ret, z, w_o_rwkv[l].astype(BF16), w_o_na[l].astype(BF16),
                     w_o_ret[l].astype(BF16), w_out[l].astype(BF16), row(norm_mix_post[l]), n_ctx_tiles)
        x = _ffn(x, mod, row(norm_ffn_pre[l]), ffn_up[l].astype(BF16), ffn_conv[l], ffn_down[l].astype(BF16),
                 row(norm_ffn_post[l]), n_ctx_tiles, tc, tl)

        new_rw.append(st_rw)

    return (x[:nc].reshape(bc, tc, D_MODEL), x[nc:].reshape(bl, tl, D_MODEL), new_k, new_v,
            jnp.stack(new_rw, axis=1), new_rt)
```

```python
import functools
import math

import jax
import jax.numpy as jnp
from jax import lax
from jax.experimental import pallas as pl
from jax.experimental.pallas import tpu as pltpu

F32 = jnp.float32
BF16 = jnp.bfloat16

D_MODEL = 1024
DEPTH = 2
N_HEAD = 8
HEAD_DIM = 64
D_BRANCH = N_HEAD * HEAD_DIM
GRID_W = 64
NA_KH = 8
NA_KW = 16
W_LORA = 64
A_LORA = 64
G_LORA = 128
D_FF = 2816
ROPE_BASE = 10000.0
NEG_BIG = -1e9
RWKV_LN_EPS = 64e-5
RET_GN_EPS = 1e-5
RMS_EPS = 1e-6

RWKV_COLS = 3 * D_BRANCH + W_LORA + A_LORA + G_LORA
RWKV_PAD = 2048
OFF_GATE = 0
OFF_RWKV = 3 * D_MODEL
OFF_NA = OFF_RWKV + RWKV_PAD
OFF_RET = OFF_NA + 3 * D_BRANCH
Z_COLS = OFF_RET + 4 * D_BRANCH

LANES = 128
TOKEN_TILE = 1024
MOD_ROWS = 8
VMEM_LIMIT = 56 * 1024 * 1024


def _cparams(sem):
    return pltpu.CompilerParams(dimension_semantics=sem, vmem_limit_bytes=VMEM_LIMIT)


def _rms(x, g):
    return x * lax.rsqrt(jnp.mean(x * x, axis=-1, keepdims=True) + RMS_EPS) * g


def _softplus(x):
    return jnp.maximum(x, 0.0) + jnp.log1p(jnp.exp(-jnp.abs(x)))


def _dot(a, b):
    return jnp.dot(a, b, preferred_element_type=F32)


def _dot_nt(a, b):
    return lax.dot_general(a, b, (((1,), (1,)), ((), ())), preferred_element_type=F32)


def _dot_tn(a, b):
    return lax.dot_general(a, b, (((0,), (0,)), ((), ())), preferred_element_type=F32)


def _mod_index(i, n_ctx_tiles, tiles_per_lat):
    return jnp.where(i < n_ctx_tiles, 0, 1 + jnp.maximum(i - n_ctx_tiles, 0) // tiles_per_lat)


def _seq_edges(i, n_ctx_tiles, t_ctx, t_lat, tm):
    seqlen = jnp.where(i < n_ctx_tiles, t_ctx, t_lat)
    pos = lax.broadcasted_iota(jnp.int32, (tm, 1), 0) & (seqlen - 1)
    return pos == 0, pos == seqlen - 1


def _dwconv3(u, w, first, last):
    tm = u.shape[0]
    prev = jnp.where(first, 0.0, pltpu.roll(u, 1, 0))
    nxt = jnp.where(last, 0.0, pltpu.roll(u, tm - 1, 0))
    return prev * w[0:1] + u * w[1:2] + nxt * w[2:3]


def _ada_kernel(c_ref, w_ref, b_ref, o_ref):
    c = c_ref[...]
    s = (c * jax.nn.sigmoid(c)).astype(BF16)
    o_ref[0] = _dot(s, w_ref[0].astype(BF16)) + b_ref[0]


def _ada_modulation(cvec, ada_w, ada_b):
    tn = 512
    n_out = 6 * D_MODEL
    return pl.pallas_call(
        _ada_kernel,
        grid=(DEPTH, n_out // tn),
        in_specs=[
            pl.BlockSpec((MOD_ROWS, D_MODEL), lambda l, j: (0, 0)),
            pl.BlockSpec((1, D_MODEL, tn), lambda l, j: (l, 0, j)),
            pl.BlockSpec((1, 1, tn), lambda l, j: (l, 0, j)),
        ],
        out_specs=pl.BlockSpec((1, MOD_ROWS, tn), lambda l, j: (l, 0, j)),
        out_shape=jax.ShapeDtypeStruct((DEPTH, MOD_ROWS, n_out), F32),
        compiler_params=_cparams(("parallel", "parallel")),
    )(cvec, ada_w, ada_b.reshape(DEPTH, 1, n_out))


def _resident(shape):
    return pl.BlockSpec(shape, lambda *_: (0,) * len(shape), pipeline_mode=pl.Buffered(1))


_W_RWKV, _W_NA, _W_RET, _W_GATE = 0, RWKV_COLS, RWKV_COLS + 3 * D_BRANCH, RWKV_COLS + 7 * D_BRANCH
IN_PROJ_GROUPS = ((_W_GATE, OFF_GATE, 3 * D_MODEL), (_W_RWKV, OFF_RWKV, RWKV_COLS),
                  (_W_NA, OFF_NA, 3 * D_BRANCH), (_W_RET, OFF_RET, 4 * D_BRANCH))
IN_COLS = _W_GATE + 3 * D_MODEL


def _in_proj_kernel(x_ref, m_ref, g_ref, w_ref, o_ref, *, tn):
    y = _rms(x_ref[...], g_ref[...])
    h = (y * (1.0 + m_ref[0, :, D_MODEL:2 * D_MODEL]) + m_ref[0, :, 0:D_MODEL]).astype(BF16)
    for src, dst, width in IN_PROJ_GROUPS:
        for c in range(0, width, tn):
            n = min(tn, width - c)
            o_ref[:, dst + c:dst + c + n] = _dot(h, w_ref[:, src + c:src + c + n]).astype(o_ref.dtype)
    pad = slice(OFF_RWKV + RWKV_COLS, OFF_RWKV + RWKV_PAD)
    o_ref[:, pad] = jnp.zeros((o_ref.shape[0], RWKV_PAD - RWKV_COLS), o_ref.dtype)


def _in_proj(x, mod, gain, w_bf16, n_ctx_tiles):
    n = x.shape[0]
    tm, tn = 512, 512
    per = TOKEN_TILE // tm
    midx = functools.partial(_mod_index, n_ctx_tiles=n_ctx_tiles * per, tiles_per_lat=per)
    return pl.pallas_call(
        functools.partial(_in_proj_kernel, tn=tn),
        grid=(n // tm,),
        in_specs=[
            pl.BlockSpec((tm, D_MODEL), lambda i: (i, 0)),
            pl.BlockSpec((1, 1, 6 * D_MODEL), lambda i: (midx(i), 0, 0)),
            pl.BlockSpec((1, D_MODEL), lambda i: (0, 0)),
            _resident((D_MODEL, IN_COLS)),
        ],
        out_specs=pl.BlockSpec((tm, Z_COLS), lambda i: (i, 0)),
        out_shape=jax.ShapeDtypeStruct((n, Z_COLS), BF16),
        compiler_params=_cparams(("parallel",)),
    )(x, mod, gain, w_bf16)


def _stream_edges(t_seq, tm):
    pos = lax.broadcasted_iota(jnp.int32, (tm, 1), 0) & (t_seq - 1)
    return pos == 0, pos == t_seq - 1


def _rwkv_conv_kernel(z_ref, w_ref, o_ref, *, t_seq):
    first, last = _stream_edges(t_seq, z_ref.shape[0])
    o_ref[0] = _dwconv3(z_ref[...].astype(F32), w_ref[...], first, last)


def _rwkv_conv(z, conv_w, row_tile0, n_tiles, t_seq):
    tm, tn = TOKEN_TILE, D_BRANCH
    cb = OFF_RWKV // tn
    return pl.pallas_call(
        functools.partial(_rwkv_conv_kernel, t_seq=t_seq),
        grid=(n_tiles, 3),
        in_specs=[
            pl.BlockSpec((tm, tn), lambda i, j: (row_tile0 + i, cb + j)),
            pl.BlockSpec((3, tn), lambda i, j: (0, j)),
        ],
        out_specs=pl.BlockSpec((1, tm, tn), lambda i, j: (j, i, 0)),
        out_shape=jax.ShapeDtypeStruct((3, n_tiles * tm, tn), F32),
        compiler_params=_cparams(("parallel", "parallel")),
    )(z, conv_w)


def _rwkv_lora_kernel(z_ref, cw_ref, wup_ref, aup_ref, gup_ref, w0_ref, a0_ref, *rest, t_seq):
    dec_ref, al_ref, g_ref = rest[-3:]
    first, last = _stream_edges(t_seq, z_ref.shape[0])
    u = _dwconv3(z_ref[...].astype(F32), cw_ref[...], first, last)
    wa = u[:, 0:LANES]
    tw = jnp.tanh(wa).astype(BF16)
    ab = wa.astype(BF16)
    sg = jax.nn.sigmoid(u[:, LANES:2 * LANES]).astype(BF16)
    for d in range(2):
        x = w0_ref[d:d + 1, :] + _dot(tw, wup_ref[d])
        w_log = -_softplus(-x) - 0.5
        dec_ref[d] = jnp.exp(-jnp.exp(w_log))
        al_ref[d] = jax.nn.sigmoid(a0_ref[d:d + 1, :] + _dot(ab, aup_ref[d]))
    g_ref[...] = _dot(sg, gup_ref[...])


def _rwkv_lora(z, lora_w, g_prev, row_tile0, n_tiles, t_seq):
    n = z.shape[0]
    tm = TOKEN_TILE
    wl = 2 * LANES
    cb = (OFF_RWKV + 3 * D_BRANCH) // wl
    out2 = jax.ShapeDtypeStruct((2, n_tiles * tm, D_BRANCH), F32)
    ospec2 = pl.BlockSpec((2, tm, D_BRANCH), lambda i: (0, i, 0))
    full2 = lambda shape: pl.BlockSpec(shape, lambda i: (0, 0))
    full3 = lambda shape: pl.BlockSpec(shape, lambda i: (0, 0, 0))
    in_specs = [
        pl.BlockSpec((tm, wl), lambda i: (row_tile0 + i, cb)),
        full2((3, wl)),
        full3((2, LANES, D_BRANCH)),
        full3((2, LANES, D_BRANCH)),
        full2((G_LORA, D_BRANCH)),
        full2((2, D_BRANCH)),
        full2((2, D_BRANCH)),
    ]
    args = (z,) + tuple(lora_w)
    aliases = {}
    if g_prev is not None:
        in_specs.append(pl.BlockSpec(memory_space=pl.ANY))
        args += (g_prev,)
        aliases = {len(args) - 1: 2}
    return pl.pallas_call(
        functools.partial(_rwkv_lora_kernel, t_seq=t_seq),
        grid=(n_tiles,),
        in_specs=in_specs,
        out_specs=[ospec2, ospec2, pl.BlockSpec((tm, D_BRANCH), lambda i: (row_tile0 + i, 0))],
        out_shape=[out2, out2, jax.ShapeDtypeStruct((n, D_BRANCH), F32)],
        input_output_aliases=aliases,
        compiler_params=_cparams(("parallel",)),
    )(*args)


def _fold_sum(x, lane, fold):
    s = LANES // fold
    while s < LANES:
        x = x + jnp.where((lane & s) != 0, pltpu.roll(x, s, 1), pltpu.roll(x, LANES - s, 1))
        s *= 2
    return x


SCAN_TB = 8


GROUPS_PER_PASS = 4


def _wkv_kernel(rf_ref, rb_ref, kf_ref, kb_ref, wf_ref, wb_ref, af_ref, ab_ref, vf_ref, vb_ref, p_ref, s0_ref,
                yf_ref, yb_ref, st_ref, s_scr, op_scr, *, fold, unroll):
    tb = SCAN_TB
    n_grp, ks = s_scr.shape[1], s_scr.shape[2]
    t = pl.program_id(1)

    @pl.when(t == 0)
    def _():
        s_scr[...] = s0_ref[...]

    lane = lax.broadcasted_iota(jnp.int32, (1, LANES), 1)
    k_k = p_ref[0]
    k_a = p_ref[1]
    dirs = ((rf_ref, kf_ref, wf_ref, af_ref, vf_ref, yf_ref), (rb_ref, kb_ref, wb_ref, ab_ref, vb_ref, yb_ref))
    row_of = lambda d, j: j if d == 0 else tb - 1 - j

    for d, (r_ref, k_ref, w_ref, a_ref, _, _) in enumerate(dirs):
        for j in range(tb):
            jj = row_of(d, j)
            k = k_ref[jj]
            a = a_ref[jj]
            kk = k * k_k
            kk = kk / jnp.maximum(jnp.sqrt(_fold_sum(jnp.sum(kk * kk, axis=0, keepdims=True), lane, fold)), 1e-12)
            op_scr[d, j, 0] = -kk
            op_scr[d, j, 1] = w_ref[jj]
            op_scr[d, j, 2] = kk * a
            op_scr[d, j, 3] = k * (1.0 + (a - 1.0) * k_a)
            op_scr[d, j, 4] = r_ref[jj]

    zeros = lambda n: tuple(jnp.zeros((8, LANES), F32) for _ in range(n))
    op_row = lambda d, j, i, k: op_scr[d, j, i, pl.ds(k, 1), :]
    rows = lambda g: slice(g * 8, (g + 1) * 8)

    def step(j, carry):
        sa = []
        for d in range(2):
            def sa_acc(k, acc):
                a = op_row(d, j, 0, k)
                return tuple(acc[g] + s_scr[d, g, k] * a for g in range(n_grp))

            sa.append(lax.fori_loop(0, ks, sa_acc, zeros(n_grp), unroll=unroll))
        sa = [tuple(_fold_sum(x, lane, fold) for x in sa_d) for sa_d in sa]

        for d in range(2):
            v_ref, y_ref = dirs[d][4], dirs[d][5]
            jj = row_of(d, j)
            for first in range(0, n_grp, GROUPS_PER_PASS):
                groups = range(first, first + GROUPS_PER_PASS)
                vs = [v_ref[jj, rows(g), :] for g in groups]

                def update(k, acc):
                    w, b, kd, r = (op_row(d, j, i, k) for i in (1, 2, 3, 4))
                    out = []
                    for i, g in enumerate(groups):
                        sn = s_scr[d, g, k] * w + sa[d][g] * b + vs[i] * kd
                        s_scr[d, g, k] = sn
                        out.append(acc[i] + sn * r)
                    return tuple(out)

                ys = lax.fori_loop(0, ks, update, zeros(GROUPS_PER_PASS), unroll=unroll)
                for i, g in enumerate(groups):
                    y_ref[jj, rows(g), :] = ys[i]
        return carry

    lax.fori_loop(0, tb, step, 0)

    @pl.when(t == pl.num_programs(1) - 1)
    def _():
        st_ref[...] = s_scr[...]


def _wkv_scan(r, k, w, a, v, params, s0, fold):
    t_len, ks, n_lane = r.shape
    tb = SCAN_TB
    nt = t_len // tb
    n_grp = HEAD_DIM // 8
    fwd = lambda rows: pl.BlockSpec((tb, rows, LANES), lambda c, t: (t, 0, c))
    bwd = lambda rows: pl.BlockSpec((tb, rows, LANES), lambda c, t: (nt - 1 - t, 0, c))
    fwd_d = lambda rows: pl.BlockSpec((None, tb, rows, LANES), lambda c, t: (0, t, 0, c))
    bwd_d = lambda rows: pl.BlockSpec((None, tb, rows, LANES), lambda c, t: (1, nt - 1 - t, 0, c))
    state = pl.BlockSpec((2, n_grp, ks, 8, LANES), lambda c, t: (0, 0, 0, 0, c))
    return pl.pallas_call(
        functools.partial(_wkv_kernel, fold=fold, unroll=ks if ks <= 16 else 8),
        grid=(n_lane // LANES, nt),
        in_specs=[fwd(ks), bwd(ks)] * 2 + [fwd_d(ks), bwd_d(ks)] * 2 + [
            fwd(HEAD_DIM), bwd(HEAD_DIM), pl.BlockSpec((2, ks, LANES), lambda c, t: (0, 0, c)), state],
        out_specs=[fwd(HEAD_DIM), bwd(HEAD_DIM), state],
        out_shape=[jax.ShapeDtypeStruct((t_len, HEAD_DIM, n_lane), F32)] * 2
        + [jax.ShapeDtypeStruct((2, n_grp, ks, 8, n_lane), F32)],
        scratch_shapes=[pltpu.VMEM((2, n_grp, ks, 8, LANES), F32), pltpu.VMEM((2, tb, 5, ks, LANES), F32)],
        compiler_params=_cparams(("parallel", "arbitrary")),
    )(r, r, k, k, w, w, a, a, v, v, params, s0)


def _rwkv_post_kernel(yf_ref, yb_ref, r_ref, k_ref, v_ref, pk_ref, pv_ref, o_ref, *, fold):
    lane = lax.broadcasted_iota(jnp.int32, (1, LANES), 1)
    ln_g = pv_ref[0]
    ln_b = pv_ref[1]
    r_k = pk_ref[...]
    for j in range(SCAN_TB):
        y = _fold_sum(yf_ref[j] + yb_ref[j], lane, fold)
        mu = jnp.mean(y, axis=0, keepdims=True)
        dlt = y - mu
        var = jnp.mean(dlt * dlt, axis=0, keepdims=True)
        o = dlt * lax.rsqrt(var + RWKV_LN_EPS) * ln_g + ln_b
        rk = _fold_sum(jnp.sum(r_ref[j] * k_ref[j] * r_k, axis=0, keepdims=True), lane, fold)
        o_ref[j] = o + rk * v_ref[j]


def _rwkv_post(yf, yb, r, k, v, pk, pv, fold):
    t_len, _, n_lane = yf.shape
    ks = r.shape[1]
    spec = lambda rows: pl.BlockSpec((SCAN_TB, rows, LANES), lambda c, i: (i, 0, c))
    return pl.pallas_call(
        functools.partial(_rwkv_post_kernel, fold=fold),
        grid=(n_lane // LANES, t_len // SCAN_TB),
        in_specs=[
            spec(HEAD_DIM), spec(HEAD_DIM), spec(ks), spec(ks), spec(HEAD_DIM),
            pl.BlockSpec((ks, LANES), lambda c, i: (0, c)),
            pl.BlockSpec((2, HEAD_DIM, LANES), lambda c, i: (0, 0, c)),
        ],
        out_specs=spec(HEAD_DIM),
        out_shape=jax.ShapeDtypeStruct((t_len, HEAD_DIM, n_lane), F32),
        compiler_params=_cparams(("parallel", "parallel")),
    )(yf, yb, r, k, v, pk, pv)


def _head(h):
    return slice(h * HEAD_DIM, (h + 1) * HEAD_DIM)


def _ctx_attn_kernel(q_ref, k_ref, v_ref, *rest):
    o_ref, ko_ref, vo_ref = rest[-3:]
    scale = HEAD_DIM ** -0.5
    for h in range(N_HEAD):
        k = k_ref[:, _head(h)]
        v = v_ref[:, _head(h)]
        ko_ref[0, h] = k.astype(F32)
        vo_ref[0, h] = v.astype(F32)
        s = _dot_nt(q_ref[:, _head(h)], k) * scale
        p = jnp.exp(s - jnp.max(s, axis=-1, keepdims=True))
        p = p / jnp.sum(p, axis=-1, keepdims=True)
        o_ref[:, _head(h)] = _dot(p.astype(BF16), v)


def _layer_slot(stacked_prev, n_in):
    if stacked_prev is None:
        return [], (), {}
    specs = [pl.BlockSpec(memory_space=pl.ANY)] * len(stacked_prev)
    return specs, tuple(stacked_prev), {n_in + i: 1 + i for i in range(len(stacked_prev))}


def _ctx_attention(z, n_rows, b, t, layer, kv_prev):
    cb = OFF_NA // D_BRANCH
    zspec = lambda c: pl.BlockSpec((t, D_BRANCH), lambda i: (i, cb + c))
    kv_shape = jax.ShapeDtypeStruct((b, DEPTH, N_HEAD, t, HEAD_DIM), F32)
    kv_spec = pl.BlockSpec((1, None, N_HEAD, t, HEAD_DIM), lambda i: (i, layer, 0, 0, 0))
    extra_specs, extra_args, aliases = _layer_slot(kv_prev, 3)
    return pl.pallas_call(
        _ctx_attn_kernel,
        grid=(b,),
        in_specs=[zspec(0), zspec(1), zspec(2)] + extra_specs,
        out_specs=[pl.BlockSpec((t, D_BRANCH), lambda i: (i, 0)), kv_spec, kv_spec],
        out_shape=[jax.ShapeDtypeStruct((n_rows, D_BRANCH), F32), kv_shape, kv_shape],
        input_output_aliases=aliases,
        compiler_params=_cparams(("parallel",)),
    )(z, z, z, *extra_args)


def _na_kernel(q_ref, k_ref, v_ref, kc_ref, vc_ref, tab_ref, prev_ref, o_ref, *, rows):
    del prev_ref
    scale = HEAD_DIM ** -0.5
    win = NA_KH * GRID_W
    qcol = lax.broadcasted_iota(jnp.int32, (GRID_W, win), 0)
    kcol = lax.broadcasted_iota(jnp.int32, (GRID_W, win), 1) & (GRID_W - 1)
    c_start = jnp.clip(qcol - NA_KW // 2, 0, GRID_W - NA_KW)
    col_valid = (kcol >= c_start) & (kcol < c_start + NA_KW)
    def row_block(r, carry):
        rs = jnp.clip(r - NA_KH // 2, 0, rows - NA_KH)
        q_rows = pl.ds(pl.multiple_of(r * GRID_W, GRID_W), GRID_W)
        w_rows = pl.ds(pl.multiple_of(rs * GRID_W, GRID_W), win)
        first = rs - r + NA_KH - 1
        for h in range(N_HEAD):
            q = q_ref[q_rows, _head(h)].astype(BF16)
            kw = k_ref[w_rows, _head(h)].astype(BF16)
            vw = v_ref[w_rows, _head(h)].astype(BF16)
            kc = kc_ref[0, 0, h].astype(BF16)
            vc = vc_ref[0, 0, h].astype(BF16)
            bias = jnp.concatenate([tab_ref[h, first + 2 * i] for i in range(NA_KH // 2)], axis=1)
            s_loc = jnp.where(col_valid, _dot_nt(q, kw) * scale + bias, NEG_BIG)
            s_ctx = _dot_nt(q, kc) * scale
            m = jnp.maximum(jnp.max(s_loc, axis=-1, keepdims=True), jnp.max(s_ctx, axis=-1, keepdims=True))
            p_loc = jnp.exp(s_loc - m)
            p_ctx = jnp.exp(s_ctx - m)
            den = jnp.sum(p_loc, axis=-1, keepdims=True) + jnp.sum(p_ctx, axis=-1, keepdims=True)
            o_ref[q_rows, _head(h)] = _dot((p_loc / den).astype(BF16), vw) + _dot((p_ctx / den).astype(BF16), vc)
        return carry

    lax.fori_loop(0, rows, row_block, 0)


def _na_attention(z, o_prev, cache_k, cache_v, bias_tab, layer, row_tile0, b, t):
    past = cache_k.shape[3]
    rows = t // GRID_W
    assert rows >= NA_KH
    cb = OFF_NA // D_BRANCH
    zspec = lambda c: pl.BlockSpec((t, D_BRANCH), lambda i: (row_tile0 + i, cb + c))
    cspec = pl.BlockSpec((1, 1, N_HEAD, past, HEAD_DIM), lambda i: (i, layer, 0, 0, 0))
    return pl.pallas_call(
        functools.partial(_na_kernel, rows=rows),
        grid=(b,),
        in_specs=[zspec(0), zspec(1), zspec(2), cspec, cspec,
                  pl.BlockSpec(bias_tab.shape, lambda i: (0, 0, 0, 0)),
                  pl.BlockSpec(memory_space=pl.ANY)],
        out_specs=pl.BlockSpec((t, D_BRANCH), lambda i: (row_tile0 + i, 0)),
        out_shape=jax.ShapeDtypeStruct(o_prev.shape, F32),
        input_output_aliases={6: 0},
        compiler_params=_cparams(("parallel",)),
    )(z, z, z, cache_k, cache_v, bias_tab, o_prev)


def _na_bias_table(rpb):
    h, ndr, _ = rpb.shape
    edge = GRID_W - NA_KW
    ext = jnp.concatenate([jnp.broadcast_to(rpb[..., :1], (h, ndr, edge)), rpb,
                           jnp.broadcast_to(rpb[..., -1:], (h, ndr, edge + 1))], axis=-1)
    skew = jnp.tile(ext, (1, 1, GRID_W))[..., :GRID_W * (2 * GRID_W - 1)].reshape(h, ndr, GRID_W, 2 * GRID_W - 1)
    toep = skew[..., GRID_W - 1:]
    return jnp.concatenate([toep[:, :-1], toep[:, 1:]], axis=-1)


def _rope_tables(t_len):
    half = HEAD_DIM // 2
    nf = half // 2
    inv = ROPE_BASE ** (-jnp.arange(nf, dtype=F32) / nf)
    t = jnp.arange(t_len)

    def tab(pos):
        ang = pos.astype(F32)[:, None] * inv[None, :]
        cos, sin = jnp.cos(ang), jnp.sin(ang)
        return jnp.concatenate([cos, cos], -1), jnp.concatenate([-sin, sin], -1)

    c_row, s_row = tab(t // GRID_W)
    c_col, s_col = tab(t % GRID_W)
    cos = jnp.tile(jnp.concatenate([c_row, c_col], -1), (1, N_HEAD))
    sin = jnp.tile(jnp.concatenate([s_row, s_col], -1), (1, N_HEAD))
    return cos, sin


def _log_sigmoid(x):
    return -_softplus(-x)


def _ret_heads(q_ref, k_ref, v_ref, g_ref, dec_ref, gn_ref, o_ref, *, t_len, qb, rope_refs=None, s0_ref=None,
               st_ref=None):
    lo = (lax.broadcasted_iota(jnp.int32, (1, LANES), 1) & 31) < 16
    for hp in range(N_HEAD // 2):
        pair = slice(hp * LANES, (hp + 1) * LANES)
        q2 = q_ref[:, pair].astype(F32)
        k2 = k_ref[:, pair].astype(F32)
        if rope_refs is not None:
            cos = rope_refs[0][:, pair]
            sin = rope_refs[1][:, pair]
            rot = lambda x: x * cos + jnp.where(lo, pltpu.roll(x, LANES - 16, 1), pltpu.roll(x, 16, 1)) * sin
            q2, k2 = rot(q2), rot(k2)
        k2 = k2 * (HEAD_DIM ** -0.5)
        for hh in range(2):
            h = 2 * hp + hh
            half = slice(hh * HEAD_DIM, (hh + 1) * HEAD_DIM)
            qh = q2[:, half].astype(BF16)
            k = k2[:, half]
            kb = k.astype(BF16)
            vb = v_ref[:, _head(h)].astype(BF16)
            lgf = _log_sigmoid(dec_ref[0, h])[0:1, :]
            lgb = _log_sigmoid(dec_ref[1, h])[0:1, :]
            lgf_h = lgf[:, 0:HEAD_DIM]
            lgb_h = lgb[:, 0:HEAD_DIM]
            for qi in range(t_len // qb):
                rows = slice(qi * qb, (qi + 1) * qb)
                q = qh[rows]
                s = _dot_nt(q, kb)
                diff = (lax.broadcasted_iota(jnp.int32, (qb, t_len), 0) + qi * qb
                        - lax.broadcasted_iota(jnp.int32, (qb, t_len), 1)).astype(F32)
                dmat = (jnp.where(diff >= 0, jnp.exp(lgf * jnp.maximum(diff, 0.0)), 0.0)
                        + jnp.where(diff <= 0, jnp.exp(lgb * jnp.maximum(-diff, 0.0)), 0.0))
                y = _dot((s * dmat).astype(BF16), vb)
                if s0_ref is not None:
                    pos = (lax.broadcasted_iota(jnp.int32, (qb, HEAD_DIM), 0) + qi * qb).astype(F32)
                    y = y + _dot(q, s0_ref[0, 0, h].astype(BF16)) * jnp.exp(lgf_h * (pos + 1.0))
                    y = y + _dot(q, s0_ref[0, 1, h].astype(BF16)) * jnp.exp(lgb_h * (t_len - pos))
                mu = jnp.mean(y, axis=-1, keepdims=True)
                dlt = y - mu
                var = jnp.mean(dlt * dlt, axis=-1, keepdims=True)
                yn = dlt * lax.rsqrt(var + RET_GN_EPS)
                g = g_ref[rows, _head(h)].astype(F32)
                o_ref[rows, _head(h)] = (yn * gn_ref[:, _head(h)]) * (g * jax.nn.sigmoid(g))
            if st_ref is not None:
                pos = lax.broadcasted_iota(jnp.int32, (t_len, HEAD_DIM), 0).astype(F32)
                kzf = (k * jnp.exp(lgf_h * (t_len - 1.0 - pos))).astype(BF16)
                kzb = (k * jnp.exp(lgb_h * pos)).astype(BF16)
                sf = _dot_tn(kzf, vb)
                sb = _dot_tn(kzb, vb)
                if s0_ref is not None:
                    sf = sf + s0_ref[0, 0, h] * jnp.exp(lgf_h * float(t_len))
                    sb = sb + s0_ref[0, 1, h] * jnp.exp(lgb_h * float(t_len))
                st_ref[0, 0, h] = sf
                st_ref[0, 1, h] = sb


def _ret_ctx_kernel(q_ref, k_ref, v_ref, g_ref, dec_ref, gn_ref, *rest, t_len, qb):
    o_ref, st_ref = rest[-2:]
    _ret_heads(q_ref, k_ref, v_ref, g_ref, dec_ref, gn_ref, o_ref, t_len=t_len, qb=qb, st_ref=st_ref)


def _ret_lat_kernel(q_ref, k_ref, v_ref, g_ref, dec_ref, gn_ref, cos_ref, sin_ref, s0_ref, prev_ref, o_ref,
                    *, t_len, qb):
    del prev_ref
    _ret_heads(q_ref, k_ref, v_ref, g_ref, dec_ref, gn_ref, o_ref, t_len=t_len, qb=qb,
               rope_refs=(cos_ref, sin_ref), s0_ref=s0_ref.at[0])


def _ret_specs(t, row_tile0):
    cb = OFF_RET // D_BRANCH
    zspec = lambda c: pl.BlockSpec((t, D_BRANCH), lambda i: (row_tile0 + i, cb + c))
    return [zspec(0), zspec(1), zspec(2), zspec(3),
            pl.BlockSpec((2, N_HEAD, 8, t), lambda i: (0, 0, 0, 0)),
            pl.BlockSpec((1, D_BRANCH), lambda i: (0, 0))]


def _retention_ctx(z, dec, gn, n_rows, b, t, layer, st_prev):
    st_shape = (b, DEPTH, 2, N_HEAD, HEAD_DIM, HEAD_DIM)
    extra_specs, extra_args, aliases = _layer_slot(st_prev, 6)
    return pl.pallas_call(
        functools.partial(_ret_ctx_kernel, t_len=t, qb=min(t, 256)),
        grid=(b,),
        in_specs=_ret_specs(t, 0) + extra_specs,
        out_specs=[pl.BlockSpec((t, D_BRANCH), lambda i: (i, 0)),
                   pl.BlockSpec((1, None) + st_shape[2:], lambda i: (i, layer, 0, 0, 0, 0))],
        out_shape=[jax.ShapeDtypeStruct((n_rows, D_BRANCH), F32), jax.ShapeDtypeStruct(st_shape, F32)],
        input_output_aliases=aliases,
        compiler_params=_cparams(("parallel",)),
    )(z, z, z, z, dec, gn, *extra_args)


def _retention_lat(z, o_prev, dec, gn, cos, sin, s0, layer, row_tile0, b, t):
    tab = pl.BlockSpec((t, D_BRANCH), lambda i: (0, 0))
    return pl.pallas_call(
        functools.partial(_ret_lat_kernel, t_len=t, qb=min(t, 256)),
        grid=(b,),
        in_specs=_ret_specs(t, row_tile0) + [
            tab, tab,
            pl.BlockSpec((1, 1, 2, N_HEAD, HEAD_DIM, HEAD_DIM), lambda i: (i, layer, 0, 0, 0, 0)),
            pl.BlockSpec(memory_space=pl.ANY)],
        out_specs=pl.BlockSpec((t, D_BRANCH), lambda i: (row_tile0 + i, 0)),
        out_shape=jax.ShapeDtypeStruct(o_prev.shape, F32),
        input_output_aliases={9: 0},
        compiler_params=_cparams(("parallel",)),
    )(z, z, z, z, dec, gn, cos, sin, s0, o_prev)


def _mix_out_kernel(x_ref, m_ref, oa_ref, ga_ref, ob_ref, oc_ref, g0_ref, g1_ref, g2_ref,
                    wa_ref, wb_ref, wc_ref, wo_ref, gain_ref, o_ref):
    out_a = _dot((oa_ref[...] * ga_ref[...]).astype(BF16), wa_ref[...])
    out_b = _dot(ob_ref[...].astype(BF16), wb_ref[...])
    out_c = _dot(oc_ref[...].astype(BF16), wc_ref[...])
    sig = lambda ref: jax.nn.sigmoid(ref[...].astype(F32))
    merged = sig(g0_ref) * out_a + sig(g1_ref) * out_b + sig(g2_ref) * out_c
    y = _dot(merged.astype(BF16), wo_ref[...])
    gate = m_ref[0, :, 2 * D_MODEL:3 * D_MODEL]
    o_ref[...] = x_ref[...] + gate * _rms(y, gain_ref[...])


def _mix_out(x, mod, o_rwkv, g_rwkv, o_na, o_ret, z, wa, wb, wc, wo, gain, n_ctx_tiles):
    n = x.shape[0]
    tm = 512
    per = TOKEN_TILE // tm
    midx = functools.partial(_mod_index, n_ctx_tiles=n_ctx_tiles * per, tiles_per_lat=per)
    row = lambda w: pl.BlockSpec((tm, w), lambda i: (i, 0))
    gate_spec = lambda g: pl.BlockSpec((tm, D_MODEL), lambda i: (i, OFF_GATE // D_MODEL + g))
    wspec = lambda a, b: pl.BlockSpec((a, b), lambda i: (0, 0))
    return pl.pallas_call(
        _mix_out_kernel,
        grid=(n // tm,),
        in_specs=[
            row(D_MODEL),
            pl.BlockSpec((1, 1, 6 * D_MODEL), lambda i: (midx(i), 0, 0)),
            row(D_BRANCH), row(D_BRANCH), row(D_BRANCH), row(D_BRANCH),
            gate_spec(0), gate_spec(1), gate_spec(2),
            wspec(D_BRANCH, D_MODEL), wspec(D_BRANCH, D_MODEL), wspec(D_BRANCH, D_MODEL),
            wspec(D_MODEL, D_MODEL), wspec(1, D_MODEL),
        ],
        out_specs=row(D_MODEL),
        out_shape=jax.ShapeDtypeStruct((n, D_MODEL), F32),
        compiler_params=_cparams(("parallel",)),
    )(x, mod, o_rwkv, g_rwkv, o_na, o_ret, z, z, z, wa, wb, wc, wo, gain)


def _gelu_tanh(x):
    return x * (0.5 * (1.0 + jnp.tanh(math.sqrt(2.0 / math.pi) * (x + 0.044715 * (x * x * x)))))


FFN_CHUNK = 256


def _ffn_kernel(x_ref, m_ref, gpre_ref, wup_ref, cw_ref, wd_ref, gpost_ref, o_ref, act_scr,
                *, n_ctx_tiles, t_ctx, t_lat):
    tm = x_ref.shape[0]
    y = _rms(x_ref[...], gpre_ref[...])
    h = (y * (1.0 + m_ref[0, :, 4 * D_MODEL:5 * D_MODEL]) + m_ref[0, :, 3 * D_MODEL:4 * D_MODEL]).astype(BF16)
    first, last = _seq_edges(pl.program_id(0), n_ctx_tiles, t_ctx, t_lat, tm)
    for c in range(D_FF // FFN_CHUNK):
        vcols = slice(c * FFN_CHUNK, (c + 1) * FFN_CHUNK)
        gcols = slice(D_FF + c * FFN_CHUNK, D_FF + (c + 1) * FFN_CHUNK)
        val = _dwconv3(_dot(h, wup_ref[:, vcols]), cw_ref[:, vcols], first, last)
        gate = _dwconv3(_dot(h, wup_ref[:, gcols]), cw_ref[:, gcols], first, last)
        act_scr[:, vcols] = (_gelu_tanh(gate) * val).astype(BF16)
    gate2 = m_ref[0, :, 5 * D_MODEL:6 * D_MODEL]
    for half in range(2):
        rows = slice(half * tm // 2, (half + 1) * tm // 2)
        down = _dot(act_scr[rows, :], wd_ref[...])
        o_ref[rows, :] = x_ref[rows, :] + gate2 * _rms(down, gpost_ref[...])


def _ffn(x, mod, gpre, w_up, conv_w, w_down, gpost, n_ctx_tiles, t_ctx, t_lat):
    n = x.shape[0]
    tm = TOKEN_TILE
    midx = functools.partial(_mod_index, n_ctx_tiles=n_ctx_tiles, tiles_per_lat=1)
    return pl.pallas_call(
        functools.partial(_ffn_kernel, n_ctx_tiles=n_ctx_tiles, t_ctx=t_ctx, t_lat=t_lat),
        grid=(n // tm,),
        in_specs=[
            pl.BlockSpec((tm, D_MODEL), lambda i: (i, 0)),
            pl.BlockSpec((1, 1, 6 * D_MODEL), lambda i: (midx(i), 0, 0)),
            pl.BlockSpec((1, D_MODEL), lambda i: (0, 0)),
            _resident((D_MODEL, 2 * D_FF)),
            _resident((3, 2 * D_FF)),
            _resident((D_FF, D_MODEL)),
            pl.BlockSpec((1, D_MODEL), lambda i: (0, 0)),
        ],
        out_specs=pl.BlockSpec((tm, D_MODEL), lambda i: (i, 0)),
        out_shape=jax.ShapeDtypeStruct((n, D_MODEL), F32),
        scratch_shapes=[pltpu.VMEM((tm, D_FF), BF16)],
        compiler_params=_cparams(("parallel",)),
    )(x, mod, gpre, w_up, conv_w, w_down, gpost)


class _ScanGeom:
    def __init__(self, b):
        self.b = b
        self.pairs = b * N_HEAD
        self.fold = max(1, LANES // self.pairs)
        self.ks = HEAD_DIM // self.fold
        self.pt = LANES // self.fold
        assert self.pairs % self.pt == 0 and self.ks % 8 == 0
        self.tiles = self.pairs // self.pt
        self.n_lane = self.tiles * LANES

    def key_operand(self, xs, t):
        lead = xs.shape[:-2]
        n = len(lead)
        ax = lambda *p: tuple(range(n)) + tuple(n + i for i in p)
        x = xs.reshape(lead + (self.b, t, N_HEAD, self.fold, self.ks)).transpose(ax(1, 4, 3, 0, 2))
        x = x.reshape(lead + (t, self.ks, self.fold, self.tiles, self.pt)).transpose(ax(0, 1, 3, 2, 4))
        return x.reshape(lead + (t, self.ks, self.n_lane))

    def value_operand(self, x, t):
        x = x.reshape(self.b, t, N_HEAD, HEAD_DIM).transpose(1, 3, 0, 2).reshape(t, HEAD_DIM, self.tiles, 1, self.pt)
        return jnp.broadcast_to(x, (t, HEAD_DIM, self.tiles, self.fold, self.pt)).reshape(t, HEAD_DIM, self.n_lane)

    def key_param(self, p):
        x = jnp.broadcast_to(p.reshape(1, N_HEAD, self.fold, self.ks), (self.b, N_HEAD, self.fold, self.ks))
        return x.reshape(self.tiles, self.pt, self.fold, self.ks).transpose(3, 0, 2, 1).reshape(self.ks, self.n_lane)

    def value_param(self, p):
        x = jnp.broadcast_to(p.reshape(1, N_HEAD, HEAD_DIM), (self.b, N_HEAD, HEAD_DIM))
        x = x.reshape(self.tiles, 1, self.pt, HEAD_DIM).transpose(3, 0, 1, 2)
        return jnp.broadcast_to(x, (HEAD_DIM, self.tiles, self.fold, self.pt)).reshape(HEAD_DIM, self.n_lane)

    def state_in(self, s0):
        x = s0.reshape(self.b, 2, N_HEAD, HEAD_DIM, self.fold, self.ks).transpose(1, 3, 5, 4, 0, 2)
        x = x.reshape(2, HEAD_DIM, self.ks, self.fold, self.tiles, self.pt).transpose(0, 1, 2, 4, 3, 5)
        return x.reshape(2, HEAD_DIM // 8, 8, self.ks, self.n_lane).transpose(0, 1, 3, 2, 4)

    def state_out(self, s):
        x = s.transpose(0, 1, 3, 2, 4).reshape(2, HEAD_DIM, self.ks, self.tiles, self.fold, self.pt)
        x = x.transpose(0, 1, 2, 4, 3, 5).reshape(2, HEAD_DIM, self.ks, self.fold, self.b, N_HEAD)
        return x.transpose(4, 0, 5, 1, 3, 2).reshape(self.b, 2, N_HEAD, HEAD_DIM, HEAD_DIM)

    def tokens_out(self, o):
        t = o.shape[0]
        x = o.reshape(t, HEAD_DIM, self.tiles, self.fold, self.pt)[:, :, :, 0]
        return x.reshape(t, HEAD_DIM, self.b, N_HEAD).transpose(2, 0, 3, 1).reshape(self.b * t, D_BRANCH)


def _rwkv_branch(rkv, dec, alpha, lp, s0, b, t):
    geo = _ScanGeom(b)
    r, k, w, a = (geo.key_operand(x, t) for x in (rkv[0], rkv[1], dec, alpha))
    v = geo.value_operand(rkv[2], t)
    scan_p = jnp.stack([geo.key_param(lp['k_k']), geo.key_param(lp['k_a'])])
    if s0 is None:
        s0_l = jnp.zeros((2, HEAD_DIM // 8, geo.ks, 8, geo.n_lane), F32)
    else:
        s0_l = geo.state_in(s0)
    yf, yb, s_fin = _wkv_scan(r, k, w, a, v, scan_p, s0_l, geo.fold)
    post_v = jnp.stack([geo.value_param(lp['ln_g']), geo.value_param(lp['ln_b'])])
    o = _rwkv_post(yf, yb, r, k, v, geo.key_param(lp['r_k']), post_v, geo.fold)
    return geo.tokens_out(o), geo.state_out(s_fin)


def kernel(x_prompt, x_sample, cache_na_k, cache_na_v, state_rwkv, state_ret, c, c_ctx, ada_w, ada_b, norm_mix_pre, norm_mix_post, norm_ffn_pre, norm_ffn_post, w_in, rwkv_conv, rwkv_w0, rwkv_w_up, rwkv_a0, rwkv_a_up, rwkv_g_up, rwkv_k_k, rwkv_k_a, rwkv_r_k, rwkv_ln_g, rwkv_ln_b, na_rpb, ret_decay, ret_gn, w_o_rwkv, w_o_na, w_o_ret, w_out, ffn_up, ffn_conv, ffn_down):
    bc, tc, _ = x_prompt.shape
    bl, tl, _ = x_sample.shape
    nc, nl = bc * tc, bl * tl
    assert tl == TOKEN_TILE and TOKEN_TILE % tc == 0 and nc % TOKEN_TILE == 0 and 1 + bl <= MOD_ROWS
    assert tc & (tc - 1) == 0 and tl % GRID_W == 0
    n_ctx_tiles = nc // TOKEN_TILE

    x = jnp.concatenate([x_prompt.reshape(nc, D_MODEL), x_sample.reshape(nl, D_MODEL)], axis=0)
    cvec = jnp.concatenate([c_ctx[None, :], c, jnp.zeros((MOD_ROWS - 1 - bl, D_MODEL), F32)], axis=0)
    mods = _ada_modulation(cvec, ada_w, ada_b).reshape(DEPTH, MOD_ROWS, 1, 6 * D_MODEL)
    rope_cos, rope_sin = _rope_tables(tl)
    row = lambda p: p.reshape(1, -1)

    new_k = new_v = new_rt = None
    new_rw = []
    for l in range(DEPTH):
        mod = mods[l]
        z = _in_proj(x, mod, row(norm_mix_pre[l]), w_in[l].astype(BF16), n_ctx_tiles)

        conv_w = jnp.pad(rwkv_conv[l], ((0, 0), (0, RWKV_PAD - RWKV_COLS)))
        wup_pad = jnp.pad(rwkv_w_up[l], ((0, 0), (0, LANES - W_LORA), (0, 0))).astype(BF16)
        aup_pad = jnp.pad(rwkv_a_up[l], ((0, 0), (W_LORA, LANES - W_LORA - A_LORA), (0, 0))).astype(BF16)
        lora_w = (conv_w[:, 3 * D_BRANCH:3 * D_BRANCH + 2 * LANES], wup_pad, aup_pad, rwkv_g_up[l].astype(BF16),
                  rwkv_w0[l], rwkv_a0[l])
        lp = {'k_k': rwkv_k_k[l], 'k_a': rwkv_k_a[l], 'ln_g': rwkv_ln_g[l], 'ln_b': rwkv_ln_b[l],
              'r_k': rwkv_r_k[l].reshape(-1)}
        n_lat_tiles = nl // TOKEN_TILE
        rkv_c = _rwkv_conv(z, conv_w, 0, n_ctx_tiles, tc)
        dec_c, alpha_c, g_rwkv = _rwkv_lora(z, lora_w, None, 0, n_ctx_tiles, tc)
        rkv_l = _rwkv_conv(z, conv_w, n_ctx_tiles, n_lat_tiles, tl)
        dec_l, alpha_l, g_rwkv = _rwkv_lora(z, lora_w, g_rwkv, n_ctx_tiles, n_lat_tiles, tl)
        o_a_ctx, st_rw = _rwkv_branch(rkv_c, dec_c, alpha_c, lp, None, bc, tc)
        o_a_lat, _ = _rwkv_branch(rkv_l, dec_l, alpha_l, lp, state_rwkv[:, l], bl, tl)
        o_rwkv = jnp.concatenate([o_a_ctx, o_a_lat], axis=0)

        o_na, new_k, new_v = _ctx_attention(z, nc + nl, bc, tc, l, None if l == 0 else (new_k, new_v))
        o_na = _na_attention(z, o_na, cache_na_k, cache_na_v, _na_bias_table(na_rpb[l]), l, n_ctx_tiles, bl, tl)

        dec_ret = jnp.broadcast_to(ret_decay[l][:, :, None, None], (2, N_HEAD, 8, tl))
        gn = row(ret_gn[l])
        o_ret, new_rt = _retention_ctx(z, dec_ret[..., :tc], gn, nc + nl, bc, tc, l, None if l == 0 else (new_rt,))
        o_ret = _retention_lat(z, o_ret, dec_ret, gn, rope_cos, rope_sin, state_ret, l, n_ctx_tiles, bl, tl)

        x = _mix_out(x, mod, o_rwkv, g_rwkv, o_na, o_ret, z, w_o_rwkv[l].astype(BF16), w_o_na[l].astype(BF16),
                     w_o_ret[l].astype(BF16), w_out[l].astype(BF16), row(norm_mix_post[l]), n_ctx_tiles)
        x = _ffn(x, mod, row(norm_ffn_pre[l]), ffn_up[l].astype(BF16), ffn_conv[l], ffn_down[l].astype(BF16),
                 row(norm_ffn_post[l]), n_ctx_tiles, tc, tl)

        new_rw.append(st_rw)

    return (x[:nc].reshape(bc, tc, D_MODEL), x[nc:].reshape(bl, tl, D_MODEL), new_k, new_v,
            jnp.stack(new_rw, axis=1), new_rt)
```

```python
import functools
import math

import jax
import jax.numpy as jnp
from jax import lax
from jax.experimental import pallas as pl
from jax.experimental.pallas import tpu as pltpu

F32 = jnp.float32
BF16 = jnp.bfloat16

D_MODEL = 1024
DEPTH = 2
N_HEAD = 8
HEAD_DIM = 64
D_BRANCH = N_HEAD * HEAD_DIM
GRID_W = 64
NA_KH = 8
NA_KW = 16
W_LORA = 64
A_LORA = 64
G_LORA = 128
D_FF = 2816
ROPE_BASE = 10000.0
NEG_BIG = -1e9
RWKV_LN_EPS = 64e-5
RET_GN_EPS = 1e-5
RMS_EPS = 1e-6

RWKV_COLS = 3 * D_BRANCH + W_LORA + A_LORA + G_LORA
RWKV_PAD = 2048
OFF_GATE = 0
OFF_RWKV = 3 * D_MODEL
OFF_NA = OFF_RWKV + RWKV_PAD
OFF_RET = OFF_NA + 3 * D_BRANCH
Z_COLS = OFF_RET + 4 * D_BRANCH

LANES = 128
TOKEN_TILE = 1024
MOD_ROWS = 8
VMEM_LIMIT = 56 * 1024 * 1024


def _cparams(sem):
    return pltpu.CompilerParams(dimension_semantics=sem, vmem_limit_bytes=VMEM_LIMIT)


def _rms(x, g):
    return x * lax.rsqrt(jnp.mean(x * x, axis=-1, keepdims=True) + RMS_EPS) * g


def _softplus(x):
    return jnp.maximum(x, 0.0) + jnp.log1p(jnp.exp(-jnp.abs(x)))


def _dot(a, b):
    return jnp.dot(a, b, preferred_element_type=F32)


def _dot_nt(a, b):
    return lax.dot_general(a, b, (((1,), (1,)), ((), ())), preferred_element_type=F32)


def _dot_tn(a, b):
    return lax.dot_general(a, b, (((0,), (0,)), ((), ())), preferred_element_type=F32)


def _mod_index(i, n_ctx_tiles, tiles_per_lat):
    return jnp.where(i < n_ctx_tiles, 0, 1 + jnp.maximum(i - n_ctx_tiles, 0) // tiles_per_lat)


def _seq_edges(i, n_ctx_tiles, t_ctx, t_lat, tm):
    seqlen = jnp.where(i < n_ctx_tiles, t_ctx, t_lat)
    pos = lax.broadcasted_iota(jnp.int32, (tm, 1), 0) & (seqlen - 1)
    return pos == 0, pos == seqlen - 1


def _dwconv3(u, w, first, last):
    tm = u.shape[0]
    prev = jnp.where(first, 0.0, pltpu.roll(u, 1, 0))
    nxt = jnp.where(last, 0.0, pltpu.roll(u, tm - 1, 0))
    return prev * w[0:1] + u * w[1:2] + nxt * w[2:3]


def _ada_kernel(c_ref, w_ref, b_ref, o_ref):
    c = c_ref[...]
    s = (c * jax.nn.sigmoid(c)).astype(BF16)
    o_ref[0] = _dot(s, w_ref[0].astype(BF16)) + b_ref[0]


def _ada_modulation(cvec, ada_w, ada_b):
    tn = 512
    n_out = 6 * D_MODEL
    return pl.pallas_call(
        _ada_kernel,
        grid=(DEPTH, n_out // tn),
        in_specs=[
            pl.BlockSpec((MOD_ROWS, D_MODEL), lambda l, j: (0, 0)),
            pl.BlockSpec((1, D_MODEL, tn), lambda l, j: (l, 0, j)),
            pl.BlockSpec((1, 1, tn), lambda l, j: (l, 0, j)),
        ],
        out_specs=pl.BlockSpec((1, MOD_ROWS, tn), lambda l, j: (l, 0, j)),
        out_shape=jax.ShapeDtypeStruct((DEPTH, MOD_ROWS, n_out), F32),
        compiler_params=_cparams(("parallel", "parallel")),
    )(cvec, ada_w, ada_b.reshape(DEPTH, 1, n_out))


def _resident(shape):
    return pl.BlockSpec(shape, lambda *_: (0,) * len(shape), pipeline_mode=pl.Buffered(1))


_W_RWKV, _W_NA, _W_RET, _W_GATE = 0, RWKV_COLS, RWKV_COLS + 3 * D_BRANCH, RWKV_COLS + 7 * D_BRANCH
IN_PROJ_GROUPS = ((_W_GATE, OFF_GATE, 3 * D_MODEL), (_W_RWKV, OFF_RWKV, RWKV_COLS),
                  (_W_NA, OFF_NA, 3 * D_BRANCH), (_W_RET, OFF_RET, 4 * D_BRANCH))
IN_COLS = _W_GATE + 3 * D_MODEL


def _in_proj_kernel(x_ref, m_ref, g_ref, w_ref, o_ref, *, tn):
    y = _rms(x_ref[...], g_ref[...])
    h = (y * (1.0 + m_ref[0, :, D_MODEL:2 * D_MODEL]) + m_ref[0, :, 0:D_MODEL]).astype(BF16)
    for src, dst, width in IN_PROJ_GROUPS:
        for c in range(0, width, tn):
            n = min(tn, width - c)
            o_ref[:, dst + c:dst + c + n] = _dot(h, w_ref[:, src + c:src + c + n]).astype(o_ref.dtype)
    pad = slice(OFF_RWKV + RWKV_COLS, OFF_RWKV + RWKV_PAD)
    o_ref[:, pad] = jnp.zeros((o_ref.shape[0], RWKV_PAD - RWKV_COLS), o_ref.dtype)


def _in_proj(x, mod, gain, w_bf16, n_ctx_tiles):
    n = x.shape[0]
    tm, tn = 512, 512
    per = TOKEN_TILE // tm
    midx = functools.partial(_mod_index, n_ctx_tiles=n_ctx_tiles * per, tiles_per_lat=per)
    return pl.pallas_call(
        functools.partial(_in_proj_kernel, tn=tn),
        grid=(n // tm,),
        in_specs=[
            pl.BlockSpec((tm, D_MODEL), lambda i: (i, 0)),
            pl.BlockSpec((1, 1, 6 * D_MODEL), lambda i: (midx(i), 0, 0)),
            pl.BlockSpec((1, D_MODEL), lambda i: (0, 0)),
            _resident((D_MODEL, IN_COLS)),
        ],
        out_specs=pl.BlockSpec((tm, Z_COLS), lambda i: (i, 0)),
        out_shape=jax.ShapeDtypeStruct((n, Z_COLS), BF16),
        compiler_params=_cparams(("parallel",)),
    )(x, mod, gain, w_bf16)


def _stream_edges(t_seq, tm):
    pos = lax.broadcasted_iota(jnp.int32, (tm, 1), 0) & (t_seq - 1)
    return pos == 0, pos == t_seq - 1


def _rwkv_conv_kernel(z_ref, w_ref, o_ref, *, t_seq):
    first, last = _stream_edges(t_seq, z_ref.shape[0])
    o_ref[0] = _dwconv3(z_ref[...].astype(F32), w_ref[...], first, last)


def _rwkv_conv(z, conv_w, row_tile0, n_tiles, t_seq):
    tm, tn = TOKEN_TILE, D_BRANCH
    cb = OFF_RWKV // tn
    return pl.pallas_call(
        functools.partial(_rwkv_conv_kernel, t_seq=t_seq),
        grid=(n_tiles, 3),
        in_specs=[
            pl.BlockSpec((tm, tn), lambda i, j: (row_tile0 + i, cb + j)),
            pl.BlockSpec((3, tn), lambda i, j: (0, j)),
        ],
        out_specs=pl.BlockSpec((1, tm, tn), lambda i, j: (j, i, 0)),
        out_shape=jax.ShapeDtypeStruct((3, n_tiles * tm, tn), F32),
        compiler_params=_cparams(("parallel", "parallel")),
    )(z, conv_w)


def _rwkv_lora_kernel(z_ref, cw_ref, wup_ref, aup_ref, gup_ref, w0_ref, a0_ref, *rest, t_seq):
    dec_ref, al_ref, g_ref = rest[-3:]
    first, last = _stream_edges(t_seq, z_ref.shape[0])
    u = _dwconv3(z_ref[...].astype(F32), cw_ref[...], first, last)
    wa = u[:, 0:LANES]
    tw = jnp.tanh(wa).astype(BF16)
    ab = wa.astype(BF16)
    sg = jax.nn.sigmoid(u[:, LANES:2 * LANES]).astype(BF16)
    for d in range(2):
        x = w0_ref[d:d + 1, :] + _dot(tw, wup_ref[d])
        w_log = -_softplus(-x) - 0.5
        dec_ref[d] = jnp.exp(-jnp.exp(w_log))
        al_ref[d] = jax.nn.sigmoid(a0_ref[d:d + 1, :] + _dot(ab, aup_ref[d]))
    g_ref[...] = _dot(sg, gup_ref[...])


def _rwkv_lora(z, lora_w, g_prev, row_tile0, n_tiles, t_seq):
    n = z.shape[0]
    tm = TOKEN_TILE
    wl = 2 * LANES
    cb = (OFF_RWKV + 3 * D_BRANCH) // wl
    out2 = jax.ShapeDtypeStruct((2, n_tiles * tm, D_BRANCH), F32)
    ospec2 = pl.BlockSpec((2, tm, D_BRANCH), lambda i: (0, i, 0))
    full2 = lambda shape: pl.BlockSpec(shape, lambda i: (0, 0))
    full3 = lambda shape: pl.BlockSpec(shape, lambda i: (0, 0, 0))
    in_specs = [
        pl.BlockSpec((tm, wl), lambda i: (row_tile0 + i, cb)),
        full2((3, wl)),
        full3((2, LANES, D_BRANCH)),
        full3((2, LANES, D_BRANCH)),
        full2((G_LORA, D_BRANCH)),
        full2((2, D_BRANCH)),
        full2((2, D_BRANCH)),
    ]
    args = (z,) + tuple(lora_w)
    aliases = {}
    if g_prev is not None:
        in_specs.append(pl.BlockSpec(memory_space=pl.ANY))
        args += (g_prev,)
        aliases = {len(args) - 1: 2}
    return pl.pallas_call(
        functools.partial(_rwkv_lora_kernel, t_seq=t_seq),
        grid=(n_tiles,),
        in_specs=in_specs,
        out_specs=[ospec2, ospec2, pl.BlockSpec((tm, D_BRANCH), lambda i: (row_tile0 + i, 0))],
        out_shape=[out2, out2, jax.ShapeDtypeStruct((n, D_BRANCH), F32)],
        input_output_aliases=aliases,
        compiler_params=_cparams(("parallel",)),
    )(*args)


def _fold_sum(x, lane, fold):
    s = LANES // fold
    while s < LANES:
        x = x + jnp.where((lane & s) != 0, pltpu.roll(x, s, 1), pltpu.roll(x, LANES - s, 1))
        s *= 2
    return x


SCAN_TB = 8


GROUPS_PER_PASS = 4


def _wkv_kernel(rf_ref, rb_ref, kf_ref, kb_ref, wf_ref, wb_ref, af_ref, ab_ref, vf_ref, vb_ref, p_ref, s0_ref,
                yf_ref, yb_ref, st_ref, s_scr, op_scr, *, fold, unroll):
    tb = SCAN_TB
    n_grp, ks = s_scr.shape[1], s_scr.shape[2]
    t = pl.program_id(1)

    @pl.when(t == 0)
    def _():
        s_scr[...] = s0_ref[...]

    lane = lax.broadcasted_iota(jnp.int32, (1, LANES), 1)
    k_k = p_ref[0]
    k_a = p_ref[1]
    dirs = ((rf_ref, kf_ref, wf_ref, af_ref, vf_ref, yf_ref), (rb_ref, kb_ref, wb_ref, ab_ref, vb_ref, yb_ref))
    row_of = lambda d, j: j if d == 0 else tb - 1 - j

    for d, (r_ref, k_ref, w_ref, a_ref, _, _) in enumerate(dirs):
        for j in range(tb):
            jj = row_of(d, j)
            k = k_ref[jj]
            a = a_ref[jj]
            kk = k * k_k
            kk = kk / jnp.maximum(jnp.sqrt(_fold_sum(jnp.sum(kk * kk, axis=0, keepdims=True), lane, fold)), 1e-12)
            op_scr[d, j, 0] = -kk
            op_scr[d, j, 1] = w_ref[jj]
            op_scr[d, j, 2] = kk * a
            op_scr[d, j, 3] = k * (1.0 + (a - 1.0) * k_a)
            op_scr[d, j, 4] = r_ref[jj]

    zeros = lambda n: tuple(jnp.zeros((8, LANES), F32) for _ in range(n))
    op_row = lambda d, j, i, k: op_scr[d, j, i, pl.ds(k, 1), :]
    rows = lambda g: slice(g * 8, (g + 1) * 8)

    def step(j, carry):
        sa = []
        for d in range(2):
            def sa_acc(k, acc):
                a = op_row(d, j, 0, k)
                return tuple(acc[g] + s_scr[d, g, k] * a for g in range(n_grp))

            sa.append(lax.fori_loop(0, ks, sa_acc, zeros(n_grp), unroll=unroll))
        sa = [tuple(_fold_sum(x, lane, fold) for x in sa_d) for sa_d in sa]

        for d in range(2):
            v_ref, y_ref = dirs[d][4], dirs[d][5]
            jj = row_of(d, j)
            for first in range(0, n_grp, GROUPS_PER_PASS):
                groups = range(first, first + GROUPS_PER_PASS)
                vs = [v_ref[jj, rows(g), :] for g in groups]

                def update(k, acc):
                    w, b, kd, r = (op_row(d, j, i, k) for i in (1, 2, 3, 4))
                    out = []
                    for i, g in enumerate(groups):
                        sn = s_scr[d, g, k] * w + sa[d][g] * b + vs[i] * kd
                        s_scr[d, g, k] = sn
                        out.append(acc[i] + sn * r)
                    return tuple(out)

                ys = lax.fori_loop(0, ks, update, zeros(GROUPS_PER_PASS), unroll=unroll)
                for i, g in enumerate(groups):
                    y_ref[jj, rows(g), :] = ys[i]
        return carry

    lax.fori_loop(0, tb, step, 0)

    @pl.when(t == pl.num_programs(1) - 1)
    def _():
        st_ref[...] = s_scr[...]


def _wkv_scan(r, k, w, a, v, params, s0, fold):
    t_len, ks, n_lane = r.shape
    tb = SCAN_TB
    nt = t_len // tb
    n_grp = HEAD_DIM // 8
    fwd = lambda rows: pl.BlockSpec((tb, rows, LANES), lambda c, t: (t, 0, c))
    bwd = lambda rows: pl.BlockSpec((tb, rows, LANES), lambda c, t: (nt - 1 - t, 0, c))
    fwd_d = lambda rows: pl.BlockSpec((None, tb, rows, LANES), lambda c, t: (0, t, 0, c))
    bwd_d = lambda rows: pl.BlockSpec((None, tb, rows, LANES), lambda c, t: (1, nt - 1 - t, 0, c))
    state = pl.BlockSpec((2, n_grp, ks, 8, LANES), lambda c, t: (0, 0, 0, 0, c))
    return pl.pallas_call(
        functools.partial(_wkv_kernel, fold=fold, unroll=min(ks, 16)),
        grid=(n_lane // LANES, nt),
        in_specs=[fwd(ks), bwd(ks)] * 2 + [fwd_d(ks), bwd_d(ks)] * 2 + [
            fwd(HEAD_DIM), bwd(HEAD_DIM), pl.BlockSpec((2, ks, LANES), lambda c, t: (0, 0, c)), state],
        out_specs=[fwd(HEAD_DIM), bwd(HEAD_DIM), state],
        out_shape=[jax.ShapeDtypeStruct((t_len, HEAD_DIM, n_lane), F32)] * 2
        + [jax.ShapeDtypeStruct((2, n_grp, ks, 8, n_lane), F32)],
        scratch_shapes=[pltpu.VMEM((2, n_grp, ks, 8, LANES), F32), pltpu.VMEM((2, tb, 5, ks, LANES), F32)],
        compiler_params=_cparams(("parallel", "arbitrary")),
    )(r, r, k, k, w, w, a, a, v, v, params, s0)


def _rwkv_post_kernel(yf_ref, yb_ref, r_ref, k_ref, v_ref, pk_ref, pv_ref, o_ref, *, fold):
    lane = lax.broadcasted_iota(jnp.int32, (1, LANES), 1)
    ln_g = pv_ref[0]
    ln_b = pv_ref[1]
    r_k = pk_ref[...]
    for j in range(SCAN_TB):
        y = _fold_sum(yf_ref[j] + yb_ref[j], lane, fold)
        mu = jnp.mean(y, axis=0, keepdims=True)
        dlt = y - mu
        var = jnp.mean(dlt * dlt, axis=0, keepdims=True)
        o = dlt * lax.rsqrt(var + RWKV_LN_EPS) * ln_g + ln_b
        rk = _fold_sum(jnp.sum(r_ref[j] * k_ref[j] * r_k, axis=0, keepdims=True), lane, fold)
        o_ref[j] = o + rk * v_ref[j]


def _rwkv_post(yf, yb, r, k, v, pk, pv, fold):
    t_len, _, n_lane = yf.shape
    ks = r.shape[1]
    spec = lambda rows: pl.BlockSpec((SCAN_TB, rows, LANES), lambda c, i: (i, 0, c))
    return pl.pallas_call(
        functools.partial(_rwkv_post_kernel, fold=fold),
        grid=(n_lane // LANES, t_len // SCAN_TB),
        in_specs=[
            spec(HEAD_DIM), spec(HEAD_DIM), spec(ks), spec(ks), spec(HEAD_DIM),
            pl.BlockSpec((ks, LANES), lambda c, i: (0, c)),
            pl.BlockSpec((2, HEAD_DIM, LANES), lambda c, i: (0, 0, c)),
        ],
        out_specs=spec(HEAD_DIM),
        out_shape=jax.ShapeDtypeStruct((t_len, HEAD_DIM, n_lane), F32),
        compiler_params=_cparams(("parallel", "parallel")),
    )(yf, yb, r, k, v, pk, pv)


def _head(h):
    return slice(h * HEAD_DIM, (h + 1) * HEAD_DIM)


def _ctx_attn_kernel(q_ref, k_ref, v_ref, *rest):
    o_ref, ko_ref, vo_ref = rest[-3:]
    scale = HEAD_DIM ** -0.5
    for h in range(N_HEAD):
        k = k_ref[:, _head(h)]
        v = v_ref[:, _head(h)]
        ko_ref[0, h] = k.astype(F32)
        vo_ref[0, h] = v.astype(F32)
        s = _dot_nt(q_ref[:, _head(h)], k) * scale
        p = jnp.exp(s - jnp.max(s, axis=-1, keepdims=True))
        p = p / jnp.sum(p, axis=-1, keepdims=True)
        o_ref[:, _head(h)] = _dot(p.astype(BF16), v)


def _layer_slot(stacked_prev, n_in):
    if stacked_prev is None:
        return [], (), {}
    specs = [pl.BlockSpec(memory_space=pl.ANY)] * len(stacked_prev)
    return specs, tuple(stacked_prev), {n_in + i: 1 + i for i in range(len(stacked_prev))}


def _ctx_attention(z, n_rows, b, t, layer, kv_prev):
    cb = OFF_NA // D_BRANCH
    zspec = lambda c: pl.BlockSpec((t, D_BRANCH), lambda i: (i, cb + c))
    kv_shape = jax.ShapeDtypeStruct((b, DEPTH, N_HEAD, t, HEAD_DIM), F32)
    kv_spec = pl.BlockSpec((1, None, N_HEAD, t, HEAD_DIM), lambda i: (i, layer, 0, 0, 0))
    extra_specs, extra_args, aliases = _layer_slot(kv_prev, 3)
    return pl.pallas_call(
        _ctx_attn_kernel,
        grid=(b,),
        in_specs=[zspec(0), zspec(1), zspec(2)] + extra_specs,
        out_specs=[pl.BlockSpec((t, D_BRANCH), lambda i: (i, 0)), kv_spec, kv_spec],
        out_shape=[jax.ShapeDtypeStruct((n_rows, D_BRANCH), F32), kv_shape, kv_shape],
        input_output_aliases=aliases,
        compiler_params=_cparams(("parallel",)),
    )(z, z, z, *extra_args)


def _na_kernel(q_ref, k_ref, v_ref, kc_ref, vc_ref, tab_ref, prev_ref, o_ref, *, rows):
    del prev_ref
    scale = HEAD_DIM ** -0.5
    win = NA_KH * GRID_W
    qcol = lax.broadcasted_iota(jnp.int32, (GRID_W, win), 0)
    kcol = lax.broadcasted_iota(jnp.int32, (GRID_W, win), 1) & (GRID_W - 1)
    c_start = jnp.clip(qcol - NA_KW // 2, 0, GRID_W - NA_KW)
    col_valid = (kcol >= c_start) & (kcol < c_start + NA_KW)
    def row_block(r, carry):
        rs = jnp.clip(r - NA_KH // 2, 0, rows - NA_KH)
        q_rows = pl.ds(pl.multiple_of(r * GRID_W, GRID_W), GRID_W)
        w_rows = pl.ds(pl.multiple_of(rs * GRID_W, GRID_W), win)
        first = rs - r + NA_KH - 1
        for h in range(N_HEAD):
            q = q_ref[q_rows, _head(h)].astype(BF16)
            kw = k_ref[w_rows, _head(h)].astype(BF16)
            vw = v_ref[w_rows, _head(h)].astype(BF16)
            kc = kc_ref[0, 0, h].astype(BF16)
            vc = vc_ref[0, 0, h].astype(BF16)
            bias = jnp.concatenate([tab_ref[h, first + 2 * i] for i in range(NA_KH // 2)], axis=1)
            s_loc = jnp.where(col_valid, _dot_nt(q, kw) * scale + bias, NEG_BIG)
            s_ctx = _dot_nt(q, kc) * scale
            m = jnp.maximum(jnp.max(s_loc, axis=-1, keepdims=True), jnp.max(s_ctx, axis=-1, keepdims=True))
            p_loc = jnp.exp(s_loc - m)
            p_ctx = jnp.exp(s_ctx - m)
            den = jnp.sum(p_loc, axis=-1, keepdims=True) + jnp.sum(p_ctx, axis=-1, keepdims=True)
            o_ref[q_rows, _head(h)] = _dot((p_loc / den).astype(BF16), vw) + _dot((p_ctx / den).astype(BF16), vc)
        return carry

    lax.fori_loop(0, rows, row_block, 0)


def _na_attention(z, o_prev, cache_k, cache_v, bias_tab, layer, row_tile0, b, t):
    past = cache_k.shape[3]
    rows = t // GRID_W
    assert rows >= NA_KH
    cb = OFF_NA // D_BRANCH
    zspec = lambda c: pl.BlockSpec((t, D_BRANCH), lambda i: (row_tile0 + i, cb + c))
    cspec = pl.BlockSpec((1, 1, N_HEAD, past, HEAD_DIM), lambda i: (i, layer, 0, 0, 0))
    return pl.pallas_call(
        functools.partial(_na_kernel, rows=rows),
        grid=(b,),
        in_specs=[zspec(0), zspec(1), zspec(2), cspec, cspec,
                  pl.BlockSpec(bias_tab.shape, lambda i: (0, 0, 0, 0)),
                  pl.BlockSpec(memory_space=pl.ANY)],
        out_specs=pl.BlockSpec((t, D_BRANCH), lambda i: (row_tile0 + i, 0)),
        out_shape=jax.ShapeDtypeStruct(o_prev.shape, F32),
        input_output_aliases={6: 0},
        compiler_params=_cparams(("parallel",)),
    )(z, z, z, cache_k, cache_v, bias_tab, o_prev)


def _na_bias_table(rpb):
    h, ndr, _ = rpb.shape
    edge = GRID_W - NA_KW
    ext = jnp.concatenate([jnp.broadcast_to(rpb[..., :1], (h, ndr, edge)), rpb,
                           jnp.broadcast_to(rpb[..., -1:], (h, ndr, edge + 1))], axis=-1)
    skew = jnp.tile(ext, (1, 1, GRID_W))[..., :GRID_W * (2 * GRID_W - 1)].reshape(h, ndr, GRID_W, 2 * GRID_W - 1)
    toep = skew[..., GRID_W - 1:]
    return jnp.concatenate([toep[:, :-1], toep[:, 1:]], axis=-1)


def _rope_tables(t_len):
    half = HEAD_DIM // 2
    nf = half // 2
    inv = ROPE_BASE ** (-jnp.arange(nf, dtype=F32) / nf)
    t = jnp.arange(t_len)

    def tab(pos):
        ang = pos.astype(F32)[:, None] * inv[None, :]
        cos, sin = jnp.cos(ang), jnp.sin(ang)
        return jnp.concatenate([cos, cos], -1), jnp.concatenate([-sin, sin], -1)

    c_row, s_row = tab(t // GRID_W)
    c_col, s_col = tab(t % GRID_W)
    cos = jnp.tile(jnp.concatenate([c_row, c_col], -1), (1, N_HEAD))
    sin = jnp.tile(jnp.concatenate([s_row, s_col], -1), (1, N_HEAD))
    return cos, sin


def _log_sigmoid(x):
    return -_softplus(-x)


def _ret_heads(q_ref, k_ref, v_ref, g_ref, dec_ref, gn_ref, o_ref, *, t_len, qb, rope_refs=None, s0_ref=None,
               st_ref=None):
    lo = (lax.broadcasted_iota(jnp.int32, (1, LANES), 1) & 31) < 16
    for hp in range(N_HEAD // 2):
        pair = slice(hp * LANES, (hp + 1) * LANES)
        q2 = q_ref[:, pair].astype(F32)
        k2 = k_ref[:, pair].astype(F32)
        if rope_refs is not None:
            cos = rope_refs[0][:, pair]
            sin = rope_refs[1][:, pair]
            rot = lambda x: x * cos + jnp.where(lo, pltpu.roll(x, LANES - 16, 1), pltpu.roll(x, 16, 1)) * sin
            q2, k2 = rot(q2), rot(k2)
        k2 = k2 * (HEAD_DIM ** -0.5)
        for hh in range(2):
            h = 2 * hp + hh
            half = slice(hh * HEAD_DIM, (hh + 1) * HEAD_DIM)
            qh = q2[:, half].astype(BF16)
            k = k2[:, half]
            kb = k.astype(BF16)
            vb = v_ref[:, _head(h)].astype(BF16)
            lgf = _log_sigmoid(dec_ref[0, h])[0:1, :]
            lgb = _log_sigmoid(dec_ref[1, h])[0:1, :]
            lgf_h = lgf[:, 0:HEAD_DIM]
            lgb_h = lgb[:, 0:HEAD_DIM]
            for qi in range(t_len // qb):
                rows = slice(qi * qb, (qi + 1) * qb)
                q = qh[rows]
                s = _dot_nt(q, kb)
                diff = (lax.broadcasted_iota(jnp.int32, (qb, t_len), 0) + qi * qb
                        - lax.broadcasted_iota(jnp.int32, (qb, t_len), 1)).astype(F32)
                dmat = (jnp.where(diff >= 0, jnp.exp(lgf * jnp.maximum(diff, 0.0)), 0.0)
                        + jnp.where(diff <= 0, jnp.exp(lgb * jnp.maximum(-diff, 0.0)), 0.0))
                y = _dot((s * dmat).astype(BF16), vb)
                if s0_ref is not None:
                    pos = (lax.broadcasted_iota(jnp.int32, (qb, HEAD_DIM), 0) + qi * qb).astype(F32)
                    y = y + _dot(q, s0_ref[0, 0, h].astype(BF16)) * jnp.exp(lgf_h * (pos + 1.0))
                    y = y + _dot(q, s0_ref[0, 1, h].astype(BF16)) * jnp.exp(lgb_h * (t_len - pos))
                mu = jnp.mean(y, axis=-1, keepdims=True)
                dlt = y - mu
                var = jnp.mean(dlt * dlt, axis=-1, keepdims=True)
                yn = dlt * lax.rsqrt(var + RET_GN_EPS)
                g = g_ref[rows, _head(h)].astype(F32)
                o_ref[rows, _head(h)] = (yn * gn_ref[:, _head(h)]) * (g * jax.nn.sigmoid(g))
            if st_ref is not None:
                pos = lax.broadcasted_iota(jnp.int32, (t_len, HEAD_DIM), 0).astype(F32)
                kzf = (k * jnp.exp(lgf_h * (t_len - 1.0 - pos))).astype(BF16)
                kzb = (k * jnp.exp(lgb_h * pos)).astype(BF16)
                sf = _dot_tn(kzf, vb)
                sb = _dot_tn(kzb, vb)
                if s0_ref is not None:
                    sf = sf + s0_ref[0, 0, h] * jnp.exp(lgf_h * float(t_len))
                    sb = sb + s0_ref[0, 1, h] * jnp.exp(lgb_h * float(t_len))
                st_ref[0, 0, h] = sf
                st_ref[0, 1, h] = sb


def _ret_ctx_kernel(q_ref, k_ref, v_ref, g_ref, dec_ref, gn_ref, *rest, t_len, qb):
    o_ref, st_ref = rest[-2:]
    _ret_heads(q_ref, k_ref, v_ref, g_ref, dec_ref, gn_ref, o_ref, t_len=t_len, qb=qb, st_ref=st_ref)


def _ret_lat_kernel(q_ref, k_ref, v_ref, g_ref, dec_ref, gn_ref, cos_ref, sin_ref, s0_ref, prev_ref, o_ref,
                    *, t_len, qb):
    del prev_ref
    _ret_heads(q_ref, k_ref, v_ref, g_ref, dec_ref, gn_ref, o_ref, t_len=t_len, qb=qb,
               rope_refs=(cos_ref, sin_ref), s0_ref=s0_ref.at[0])


def _ret_specs(t, row_tile0):
    cb = OFF_RET // D_BRANCH
    zspec = lambda c: pl.BlockSpec((t, D_BRANCH), lambda i: (row_tile0 + i, cb + c))
    return [zspec(0), zspec(1), zspec(2), zspec(3),
            pl.BlockSpec((2, N_HEAD, 8, t), lambda i: (0, 0, 0, 0)),
            pl.BlockSpec((1, D_BRANCH), lambda i: (0, 0))]


def _retention_ctx(z, dec, gn, n_rows, b, t, layer, st_prev):
    st_shape = (b, DEPTH, 2, N_HEAD, HEAD_DIM, HEAD_DIM)
    extra_specs, extra_args, aliases = _layer_slot(st_prev, 6)
    return pl.pallas_call(
        functools.partial(_ret_ctx_kernel, t_len=t, qb=min(t, 256)),
        grid=(b,),
        in_specs=_ret_specs(t, 0) + extra_specs,
        out_specs=[pl.BlockSpec((t, D_BRANCH), lambda i: (i, 0)),
                   pl.BlockSpec((1, None) + st_shape[2:], lambda i: (i, layer, 0, 0, 0, 0))],
        out_shape=[jax.ShapeDtypeStruct((n_rows, D_BRANCH), F32), jax.ShapeDtypeStruct(st_shape, F32)],
        input_output_aliases=aliases,
        compiler_params=_cparams(("parallel",)),
    )(z, z, z, z, dec, gn, *extra_args)


def _retention_lat(z, o_prev, dec, gn, cos, sin, s0, layer, row_tile0, b, t):
    tab = pl.BlockSpec((t, D_BRANCH), lambda i: (0, 0))
    return pl.pallas_call(
        functools.partial(_ret_lat_kernel, t_len=t, qb=min(t, 256)),
        grid=(b,),
        in_specs=_ret_specs(t, row_tile0) + [
            tab, tab,
            pl.BlockSpec((1, 1, 2, N_HEAD, HEAD_DIM, HEAD_DIM), lambda i: (i, layer, 0, 0, 0, 0)),
            pl.BlockSpec(memory_space=pl.ANY)],
        out_specs=pl.BlockSpec((t, D_BRANCH), lambda i: (row_tile0 + i, 0)),
        out_shape=jax.ShapeDtypeStruct(o_prev.shape, F32),
        input_output_aliases={9: 0},
        compiler_params=_cparams(("parallel",)),
    )(z, z, z, z, dec, gn, cos, sin, s0, o_prev)


def _mix_out_kernel(x_ref, m_ref, oa_ref, ga_ref, ob_ref, oc_ref, g0_ref, g1_ref, g2_ref,
                    wa_ref, wb_ref, wc_ref, wo_ref, gain_ref, o_ref):
    out_a = _dot((oa_ref[...] * ga_ref[...]).astype(BF16), wa_ref[...])
    out_b = _dot(ob_ref[...].astype(BF16), wb_ref[...])
    out_c = _dot(oc_ref[...].astype(BF16), wc_ref[...])
    sig = lambda ref: jax.nn.sigmoid(ref[...].astype(F32))
    merged = sig(g0_ref) * out_a + sig(g1_ref) * out_b + sig(g2_ref) * out_c
    y = _dot(merged.astype(BF16), wo_ref[...])
    gate = m_ref[0, :, 2 * D_MODEL:3 * D_MODEL]
    o_ref[...] = x_ref[...] + gate * _rms(y, gain_ref[...])


def _mix_out(x, mod, o_rwkv, g_rwkv, o_na, o_ret, z, wa, wb, wc, wo, gain, n_ctx_tiles):
    n = x.shape[0]
    tm = 512
    per = TOKEN_TILE // tm
    midx = functools.partial(_mod_index, n_ctx_tiles=n_ctx_tiles * per, tiles_per_lat=per)
    row = lambda w: pl.BlockSpec((tm, w), lambda i: (i, 0))
    gate_spec = lambda g: pl.BlockSpec((tm, D_MODEL), lambda i: (i, OFF_GATE // D_MODEL + g))
    wspec = lambda a, b: pl.BlockSpec((a, b), lambda i: (0, 0))
    return pl.pallas_call(
        _mix_out_kernel,
        grid=(n // tm,),
        in_specs=[
            row(D_MODEL),
            pl.BlockSpec((1, 1, 6 * D_MODEL), lambda i: (midx(i), 0, 0)),
            row(D_BRANCH), row(D_BRANCH), row(D_BRANCH), row(D_BRANCH),
            gate_spec(0), gate_spec(1), gate_spec(2),
            wspec(D_BRANCH, D_MODEL), wspec(D_BRANCH, D_MODEL), wspec(D_BRANCH, D_MODEL),
            wspec(D_MODEL, D_MODEL), wspec(1, D_MODEL),
        ],
        out_specs=row(D_MODEL),
        out_shape=jax.ShapeDtypeStruct((n, D_MODEL), F32),
        compiler_params=_cparams(("parallel",)),
    )(x, mod, o_rwkv, g_rwkv, o_na, o_ret, z, z, z, wa, wb, wc, wo, gain)


def _gelu_tanh(x):
    return x * (0.5 * (1.0 + jnp.tanh(math.sqrt(2.0 / math.pi) * (x + 0.044715 * (x * x * x)))))


FFN_CHUNK = 256


def _ffn_kernel(x_ref, m_ref, gpre_ref, wup_ref, cw_ref, wd_ref, gpost_ref, o_ref, act_scr,
                *, n_ctx_tiles, t_ctx, t_lat):
    tm = x_ref.shape[0]
    y = _rms(x_ref[...], gpre_ref[...])
    h = (y * (1.0 + m_ref[0, :, 4 * D_MODEL:5 * D_MODEL]) + m_ref[0, :, 3 * D_MODEL:4 * D_MODEL]).astype(BF16)
    first, last = _seq_edges(pl.program_id(0), n_ctx_tiles, t_ctx, t_lat, tm)
    for c in range(D_FF // FFN_CHUNK):
        vcols = slice(c * FFN_CHUNK, (c + 1) * FFN_CHUNK)
        gcols = slice(D_FF + c * FFN_CHUNK, D_FF + (c + 1) * FFN_CHUNK)
        val = _dwconv3(_dot(h, wup_ref[:, vcols]), cw_ref[:, vcols], first, last)
        gate = _dwconv3(_dot(h, wup_ref[:, gcols]), cw_ref[:, gcols], first, last)
        act_scr[:, vcols] = (_gelu_tanh(gate) * val).astype(BF16)
    gate2 = m_ref[0, :, 5 * D_MODEL:6 * D_MODEL]
    for half in range(2):
        rows = slice(half * tm // 2, (half + 1) * tm // 2)
        down = _dot(act_scr[rows, :], wd_ref[...])
        o_ref[rows, :] = x_ref[rows, :] + gate2 * _rms(down, gpost_ref[...])


def _ffn(x, mod, gpre, w_up, conv_w, w_down, gpost, n_ctx_tiles, t_ctx, t_lat):
    n = x.shape[0]
    tm = TOKEN_TILE
    midx = functools.partial(_mod_index, n_ctx_tiles=n_ctx_tiles, tiles_per_lat=1)
    return pl.pallas_call(
        functools.partial(_ffn_kernel, n_ctx_tiles=n_ctx_tiles, t_ctx=t_ctx, t_lat=t_lat),
        grid=(n // tm,),
        in_specs=[
            pl.BlockSpec((tm, D_MODEL), lambda i: (i, 0)),
            pl.BlockSpec((1, 1, 6 * D_MODEL), lambda i: (midx(i), 0, 0)),
            pl.BlockSpec((1, D_MODEL), lambda i: (0, 0)),
            _resident((D_MODEL, 2 * D_FF)),
            _resident((3, 2 * D_FF)),
            _resident((D_FF, D_MODEL)),
            pl.BlockSpec((1, D_MODEL), lambda i: (0, 0)),
        ],
        out_specs=pl.BlockSpec((tm, D_MODEL), lambda i: (i, 0)),
        out_shape=jax.ShapeDtypeStruct((n, D_MODEL), F32),
        scratch_shapes=[pltpu.VMEM((tm, D_FF), BF16)],
        compiler_params=_cparams(("parallel",)),
    )(x, mod, gpre, w_up, conv_w, w_down, gpost)


class _ScanGeom:
    def __init__(self, b):
        self.b = b
        self.pairs = b * N_HEAD
        self.fold = max(1, LANES // self.pairs)
        self.ks = HEAD_DIM // self.fold
        self.pt = LANES // self.fold
        assert self.pairs % self.pt == 0 and self.ks % 8 == 0
        self.tiles = self.pairs // self.pt
        self.n_lane = self.tiles * LANES

    def key_operand(self, xs, t):
        lead = xs.shape[:-2]
        n = len(lead)
        ax = lambda *p: tuple(range(n)) + tuple(n + i for i in p)
        x = xs.reshape(lead + (self.b, t, N_HEAD, self.fold, self.ks)).transpose(ax(1, 4, 3, 0, 2))
        x = x.reshape(lead + (t, self.ks, self.fold, self.tiles, self.pt)).transpose(ax(0, 1, 3, 2, 4))
        return x.reshape(lead + (t, self.ks, self.n_lane))

    def value_operand(self, x, t):
        x = x.reshape(self.b, t, N_HEAD, HEAD_DIM).transpose(1, 3, 0, 2).reshape(t, HEAD_DIM, self.tiles, 1, self.pt)
        return jnp.broadcast_to(x, (t, HEAD_DIM, self.tiles, self.fold, self.pt)).reshape(t, HEAD_DIM, self.n_lane)

    def key_param(self, p):
        x = jnp.broadcast_to(p.reshape(1, N_HEAD, self.fold, self.ks), (self.b, N_HEAD, self.fold, self.ks))
        return x.reshape(self.tiles, self.pt, self.fold, self.ks).transpose(3, 0, 2, 1).reshape(self.ks, self.n_lane)

    def value_param(self, p):
        x = jnp.broadcast_to(p.reshape(1, N_HEAD, HEAD_DIM), (self.b, N_HEAD, HEAD_DIM))
        x = x.reshape(self.tiles, 1, self.pt, HEAD_DIM).transpose(3, 0, 1, 2)
        return jnp.broadcast_to(x, (HEAD_DIM, self.tiles, self.fold, self.pt)).reshape(HEAD_DIM, self.n_lane)

    def state_in(self, s0):
        x = s0.reshape(self.b, 2, N_HEAD, HEAD_DIM, self.fold, self.ks).transpose(1, 3, 5, 4, 0, 2)
        x = x.reshape(2, HEAD_DIM, self.ks, self.fold, self.tiles, self.pt).transpose(0, 1, 2, 4, 3, 5)
        return x.reshape(2, HEAD_DIM // 8, 8, self.ks, self.n_lane).transpose(0, 1, 3, 2, 4)

    def state_out(self, s):
        x = s.transpose(0, 1, 3, 2, 4).reshape(2, HEAD_DIM, self.ks, self.tiles, self.fold, self.pt)
        x = x.transpose(0, 1, 2, 4, 3, 5).reshape(2, HEAD_DIM, self.ks, self.fold, self.b, N_HEAD)
        return x.transpose(4, 0, 5, 1, 3, 2).reshape(self.b, 2, N_HEAD, HEAD_DIM, HEAD_DIM)

    def tokens_out(self, o):
        t = o.shape[0]
        x = o.reshape(t, HEAD_DIM, self.tiles, self.fold, self.pt)[:, :, :, 0]
        return x.reshape(t, HEAD_DIM, self.b, N_HEAD).transpose(2, 0, 3, 1).reshape(self.b * t, D_BRANCH)


def _rwkv_branch(rkv, dec, alpha, lp, s0, b, t):
    geo = _ScanGeom(b)
    r, k, w, a = (geo.key_operand(x, t) for x in (rkv[0], rkv[1], dec, alpha))
    v = geo.value_operand(rkv[2], t)
    scan_p = jnp.stack([geo.key_param(lp['k_k']), geo.key_param(lp['k_a'])])
    if s0 is None:
        s0_l = jnp.zeros((2, HEAD_DIM // 8, geo.ks, 8, geo.n_lane), F32)
    else:
        s0_l = geo.state_in(s0)
    yf, yb, s_fin = _wkv_scan(r, k, w, a, v, scan_p, s0_l, geo.fold)
    post_v = jnp.stack([geo.value_param(lp['ln_g']), geo.value_param(lp['ln_b'])])
    o = _rwkv_post(yf, yb, r, k, v, geo.key_param(lp['r_k']), post_v, geo.fold)
    return geo.tokens_out(o), geo.state_out(s_fin)


def kernel(x_prompt, x_sample, cache_na_k, cache_na_v, state_rwkv, state_ret, c, c_ctx, ada_w, ada_b, norm_mix_pre, norm_mix_post, norm_ffn_pre, norm_ffn_post, w_in, rwkv_conv, rwkv_w0, rwkv_w_up, rwkv_a0, rwkv_a_up, rwkv_g_up, rwkv_k_k, rwkv_k_a, rwkv_r_k, rwkv_ln_g, rwkv_ln_b, na_rpb, ret_decay, ret_gn, w_o_rwkv, w_o_na, w_o_ret, w_out, ffn_up, ffn_conv, ffn_down):
    bc, tc, _ = x_prompt.shape
    bl, tl, _ = x_sample.shape
    nc, nl = bc * tc, bl * tl
    assert tl == TOKEN_TILE and TOKEN_TILE % tc == 0 and nc % TOKEN_TILE == 0 and 1 + bl <= MOD_ROWS
    assert tc & (tc - 1) == 0 and tl % GRID_W == 0
    n_ctx_tiles = nc // TOKEN_TILE

    x = jnp.concatenate([x_prompt.reshape(nc, D_MODEL), x_sample.reshape(nl, D_MODEL)], axis=0)
    cvec = jnp.concatenate([c_ctx[None, :], c, jnp.zeros((MOD_ROWS - 1 - bl, D_MODEL), F32)], axis=0)
    mods = _ada_modulation(cvec, ada_w, ada_b).reshape(DEPTH, MOD_ROWS, 1, 6 * D_MODEL)
    rope_cos, rope_sin = _rope_tables(tl)
    row = lambda p: p.reshape(1, -1)

    new_k = new_v = new_rt = None
    new_rw = []
    for l in range(DEPTH):
        mod = mods[l]
        z = _in_proj(x, mod, row(norm_mix_pre[l]), w_in[l].astype(BF16), n_ctx_tiles)

        conv_w = jnp.pad(rwkv_conv[l], ((0, 0), (0, RWKV_PAD - RWKV_COLS)))
        wup_pad = jnp.pad(rwkv_w_up[l], ((0, 0), (0, LANES - W_LORA), (0, 0))).astype(BF16)
        aup_pad = jnp.pad(rwkv_a_up[l], ((0, 0), (W_LORA, LANES - W_LORA - A_LORA), (0, 0))).astype(BF16)
        lora_w = (conv_w[:, 3 * D_BRANCH:3 * D_BRANCH + 2 * LANES], wup_pad, aup_pad, rwkv_g_up[l].astype(BF16),
                  rwkv_w0[l], rwkv_a0[l])
        lp = {'k_k': rwkv_k_k[l], 'k_a': rwkv_k_a[l], 'ln_g': rwkv_ln_g[l], 'ln_b': rwkv_ln_b[l],
              'r_k': rwkv_r_k[l].reshape(-1)}
        n_lat_tiles = nl // TOKEN_TILE
        rkv_c = _rwkv_conv(z, conv_w, 0, n_ctx_tiles, tc)
        dec_c, alpha_c, g_rwkv = _rwkv_lora(z, lora_w, None, 0, n_ctx_tiles, tc)
        rkv_l = _rwkv_conv(z, conv_w, n_ctx_tiles, n_lat_tiles, tl)
        dec_l, alpha_l, g_rwkv = _rwkv_lora(z, lora_w, g_rwkv, n_ctx_tiles, n_lat_tiles, tl)
        o_a_ctx, st_rw = _rwkv_branch(rkv_c, dec_c, alpha_c, lp, None, bc, tc)
        o_a_lat, _ = _rwkv_branch(rkv_l, dec_l, alpha_l, lp, state_rwkv[:, l], bl, tl)
        o_rwkv = jnp.concatenate([o_a_ctx, o_a_lat], axis=0)

        o_na, new_k, new_v = _ctx_attention(z, nc + nl, bc, tc, l, None if l == 0 else (new_k, new_v))
        o_na = _na_attention(z, o_na, cache_na_k, cache_na_v, _na_bias_table(na_rpb[l]), l, n_ctx_tiles, bl, tl)

        dec_ret = jnp.broadcast_to(ret_decay[l][:, :, None, None], (2, N_HEAD, 8, tl))
        gn = row(ret_gn[l])
        o_ret, new_rt = _retention_ctx(z, dec_ret[..., :tc], gn, nc + nl, bc, tc, l, None if l == 0 else (new_rt,))
        o_ret = _retention_lat(z, o_ret, dec_ret, gn, rope_cos, rope_sin, state_ret, l, n_ctx_tiles, bl, tl)

        x = _mix_out(x, mod, o_rwkv, g_rwkv, o_na, o_ret, z, w_o_rwkv[l].astype(BF16), w_o_na[l].astype(BF16),
                     w_o_ret[l].astype(BF16), w_out[l].astype(BF16), row(norm_mix_post[l]), n_ctx_tiles)
        x = _ffn(x, mod, row(norm_ffn_pre[l]), ffn_up[l].astype(BF16), ffn_conv[l], ffn_down[l].astype(BF16),
                 row(norm_ffn_post[l]), n_ctx_tiles, tc, tl)

        new_rw.append(st_rw)

    return (x[:nc].reshape(bc, tc, D_MODEL), x[nc:].reshape(bl, tl, D_MODEL), new_k, new_v,
            jnp.stack(new_rw, axis=1), new_rt)
```
